```python
import math
import jax
import jax.numpy as jnp
from jax import lax
import numpy as np

D_MODEL = 1024
BATCH = 32
SEQ = 256
DEPTH = 4
DEC_BATCH = 4
DEC_SEQ = 2048
PAST_LEN = 256

GRID_W = 64
D_MIX = D_MODEL
W_GROUP = D_MIX // 4
H_RET = 4
DK_RET = W_GROUP // H_RET
W_HY = W_GROUP
HY_EMB = 33
HY_HIDDEN = 64
HY_DECAY_TARGET = 1e-2
HY_FAST_PCT = 0.3
HY_SLOW_PCT = 1.5
H_GDN = 4
DK_GDN = W_GROUP // H_GDN
DV_GDN = W_GROUP // H_GDN
W_GDN = H_GDN * DV_GDN
H_SSD = 4
P_SSD = W_GROUP // H_SSD
W_SSD = H_SSD * P_SSD
G_SSD = 2
N_SSD = 128
SHORT_CONV = 3
CHUNK = 64
D_FF = 4 * D_MODEL
ROPE_BASE = 10000.0
EPS = 1e-6
RET_COLS = 4 * W_GROUP
HY_COLS = 3 * W_HY
GDN_COLS = 4 * W_GDN + 4 * H_GDN
SSD_COLS = 2 * W_SSD + 2 * G_SSD * N_SSD + 2 * H_SSD
IN_COLS = RET_COLS + HY_COLS + GDN_COLS + SSD_COLS
SPLITS = (RET_COLS, RET_COLS + HY_COLS, RET_COLS + HY_COLS + GDN_COLS)

kernel_name = "hybrid_parallel_groups_diffusion_step"


def rms_norm(x, w):
    xf = x.astype(jnp.float32)
    y = xf * lax.rsqrt(jnp.mean(xf * xf, axis=-1, keepdims=True) + EPS)
    return (y * w.astype(jnp.float32)).astype(x.dtype)


def head_rms(o, w):
    h, d = o.shape[-2:]
    return rms_norm(o, w.reshape(h, d)).reshape(*o.shape[:-2], h * d)


def l2norm(x):
    xf = x.astype(jnp.float32)
    return (xf * lax.rsqrt(jnp.sum(xf * xf, axis=-1, keepdims=True) + EPS)).astype(x.dtype)


def _rev(t):
    return jnp.flip(t, axis=1)


def centred_conv(x, w, b=None):
    k_w = w.shape[0]
    pad = k_w // 2
    length = x.shape[1]
    xp = jnp.pad(x, ((0, 0), (pad, pad), (0, 0)))
    y = xp[:, 0:length] * w[0]
    for i in range(1, k_w):
        y = y + xp[:, i:i + length] * w[i]
    return y if b is None else y + b


def axial_rope(n_tok):
    rows = n_tok // GRID_W
    r = jnp.repeat(jnp.arange(rows), GRID_W).astype(jnp.float32)
    col = jnp.tile(jnp.arange(GRID_W), rows).astype(jnp.float32)
    quarter = DK_RET // 4
    inv = ROPE_BASE ** (-jnp.arange(quarter, dtype=jnp.float32) / quarter)
    ang = jnp.concatenate([r[:, None] * inv, col[:, None] * inv], axis=-1)
    return jnp.cos(ang), jnp.sin(ang)


def apply_rope(x, cos, sin):
    x1, x2 = jnp.split(x, 2, axis=-1)
    c = cos[None, :, None, :]
    s = sin[None, :, None, :]
    return jnp.concatenate([x1 * c - x2 * s, x1 * s + x2 * c], axis=-1).astype(x.dtype)


def _to_chunks(t):
    b, length, h = t.shape[:3]
    t = t.astype(jnp.float32).reshape(b, length // CHUNK, CHUNK, h, *t.shape[3:])
    return jnp.moveaxis(jnp.moveaxis(t, 1, 0), 3, 2)


def _from_chunks(o):
    n, b, h, c, v = o.shape
    return jnp.swapaxes(jnp.moveaxis(o, 0, 1), 2, 3).reshape(b, n * c, h, v)


def decay_scan(q, k, v, log_a, s0):
    incl = jnp.tril(jnp.ones((CHUNK, CHUNK), dtype=bool))

    def step(S, inp):
        qc, kc, vc, gc = inp
        cum = jnp.cumsum(gc, axis=-1)
        dec = jnp.exp(jnp.where(incl, cum[..., :, None] - cum[..., None, :], -jnp.inf))
        att = jnp.einsum('bhik,bhjk->bhij', qc, kc) * dec
        o = (jnp.einsum('bhij,bhjv->bhiv', att, vc)
             + jnp.einsum('bhck,bhkv->bhcv', qc * jnp.exp(cum)[..., None], S))
        last = cum[..., -1:]
        S = (jnp.exp(last)[..., None] * S
             + jnp.einsum('bhck,bhcv->bhkv', kc * jnp.exp(last - cum)[..., None], vc))
        return S, o

    S, o = lax.scan(step, s0.astype(jnp.float32),
                    (_to_chunks(q), _to_chunks(k), _to_chunks(v), _to_chunks(log_a)))
    return _from_chunks(o).astype(v.dtype), S.astype(v.dtype)


def gated_delta_scan(q, k, v, log_a, beta, s0):
    dk = q.shape[-1]
    incl = jnp.tril(jnp.ones((CHUNK, CHUNK), dtype=bool))
    strict = jnp.tril(jnp.ones((CHUNK, CHUNK), dtype=bool), -1)
    eye = jnp.eye(CHUNK, dtype=jnp.float32)

    def step(S, inp):
        qc, kc, vc, gc, bc = inp
        cum = jnp.cumsum(gc, axis=-1)
        dec = jnp.exp(jnp.where(incl, cum[..., :, None] - cum[..., None, :], -jnp.inf))
        a_mat = jnp.where(strict, dec * jnp.einsum('bhik,bhjk->bhij', kc, kc), 0.0) * bc[..., :, None]
        rhs = jnp.concatenate([kc * (bc * jnp.exp(cum))[..., None], vc * bc[..., None]], axis=-1)
        sol = lax.linalg.triangular_solve(a_mat + eye, rhs, left_side=True, lower=True,
                                          unit_diagonal=True)
        w_c, u_c = sol[..., :dk], sol[..., dk:]
        v_new = u_c - jnp.einsum('bhck,bhkv->bhcv', w_c, S)
        att = jnp.einsum('bhik,bhjk->bhij', qc, kc) * dec
        o = (jnp.einsum('bhck,bhkv->bhcv', qc * jnp.exp(cum)[..., None], S)
             + jnp.einsum('bhij,bhjv->bhiv', att, v_new))
        last = cum[..., -1:]
        S = (jnp.exp(last)[..., None] * S
             + jnp.einsum('bhck,bhcv->bhkv', kc * jnp.exp(last - cum)[..., None], v_new))
        return S, o

    S, o = lax.scan(step, s0.astype(jnp.float32),
                    (_to_chunks(q), _to_chunks(k), _to_chunks(v), _to_chunks(log_a), _to_chunks(beta)))
    return _from_chunks(o).astype(v.dtype), S.astype(v.dtype)


def retention_mixer(p, log_decay, norm_w, s0, rope):
    b, length, _ = p.shape
    q, k, v, g = jnp.split(p, 4, axis=-1)
    q = q.reshape(b, length, H_RET, DK_RET)
    k = k.reshape(b, length, H_RET, DK_RET) * (DK_RET ** -0.5)
    v = v.reshape(b, length, H_RET, DK_RET)
    if rope is not None:
        q = apply_rope(q, rope[0], rope[1])
        k = apply_rope(k, rope[0], rope[1])
    la = jnp.broadcast_to(log_decay, (b, length, 2, H_RET))
    o_f, s_f = decay_scan(q, k, v, la[:, :, 0], s0[:, 0])
    o_b, s_b = decay_scan(_rev(q), _rev(k), _rev(v), la[:, :, 1], s0[:, 1])
    o = head_rms(o_f + _rev(o_b), norm_w) * jax.nn.silu(g)
    return o, jnp.stack([s_f, s_b], axis=1)


def hyena_filter(length, freq, w1, b1, w2, b2, w3):
    t = jnp.linspace(0.0, 1.0, length, dtype=jnp.float32)[:, None]
    bands = (HY_EMB - 1) // 2
    w = 2.0 * math.pi * jnp.arange(length, dtype=jnp.float32)[:, None] / length
    f = jnp.linspace(1e-4, bands - 1, bands, dtype=jnp.float32)[None, :]
    z = jnp.concatenate([t, jnp.cos(f * w), -jnp.sin(f * w)], axis=-1)
    hdn = jnp.sin(freq * (z @ w1 + b1))
    hdn = jnp.sin(freq * (hdn @ w2 + b2))
    taps = (hdn @ w3).astype(jnp.float32).reshape(length, 2, W_HY)
    deltas = jnp.abs(jnp.linspace(math.log(HY_DECAY_TARGET) / HY_SLOW_PCT,
                                  math.log(HY_DECAY_TARGET) / HY_FAST_PCT, W_HY, dtype=jnp.float32))
    return taps * jnp.exp(-t * deltas)[:, None, :]


def bidir_long_conv(u, taps, bias):
    length = u.shape[1]
    circ_taps = jnp.concatenate([taps[:, 0], jnp.zeros_like(taps[:1, 0]), taps[:0:-1, 1]], axis=0)
    u_f = jnp.fft.rfft(u.astype(jnp.float32), n=2 * length, axis=1)
    t_f = jnp.fft.rfft(circ_taps, n=2 * length, axis=0)
    y = jnp.fft.irfft(u_f * t_f[None], n=2 * length, axis=1)[:, :length]
    return (y + u * bias).astype(u.dtype)


def hyena_mixer(p, conv_w, conv_b, taps, bias, norm_w):
    x0, x1, v = jnp.split(centred_conv(p, conv_w, conv_b), 3, axis=-1)
    y = x0 * bidir_long_conv(v * x1, taps, bias)
    return rms_norm(y, norm_w)


def gdn_mixer(p, conv_w, a_log, dt_bias, norm_w, s0):
    b, length, _ = p.shape
    qkv, z, a, bb = jnp.split(p, (3 * W_GDN, 4 * W_GDN, 4 * W_GDN + 2 * H_GDN), axis=-1)
    q, k, v = jnp.split(jax.nn.silu(centred_conv(qkv, conv_w)), 3, axis=-1)
    q = l2norm(q.reshape(b, length, H_GDN, DK_GDN)) * (DK_GDN ** -0.5)
    k = l2norm(k.reshape(b, length, H_GDN, DK_GDN))
    v = v.reshape(b, length, H_GDN, DV_GDN)
    log_a = -jnp.exp(a_log) * jax.nn.softplus(a.reshape(b, length, 2, H_GDN) + dt_bias)
    beta = jax.nn.sigmoid(bb.reshape(b, length, 2, H_GDN))
    o_f, s_f = gated_delta_scan(q, k, v, log_a[:, :, 0], beta[:, :, 0], s0[:, 0])
    o_b, s_b = gated_delta_scan(_rev(q), _rev(k), _rev(v), _rev(log_a[:, :, 1]),
                                _rev(beta[:, :, 1]), s0[:, 1])
    o = head_rms(o_f + _rev(o_b), norm_w) * jax.nn.silu(z)
    return o, jnp.stack([s_f, s_b], axis=1)


def ssd_mixer(p, conv_w, conv_b, a_log, dt_bias, d_skip, norm_w, s0):
    b, length, _ = p.shape
    gn = G_SSD * N_SSD
    xbc, z, dt = jnp.split(p, (W_SSD + 2 * gn, 2 * W_SSD + 2 * gn), axis=-1)
    xs, bm, cm = jnp.split(jax.nn.silu(centred_conv(xbc, conv_w, conv_b)), (W_SSD, W_SSD + gn), axis=-1)
    xs = xs.reshape(b, length, H_SSD, P_SSD)
    bm = jnp.repeat(bm.reshape(b, length, G_SSD, N_SSD), H_SSD // G_SSD, axis=2)
    cm = jnp.repeat(cm.reshape(b, length, G_SSD, N_SSD), H_SSD // G_SSD, axis=2)
    dt = jax.nn.softplus(dt.reshape(b, length, 2, H_SSD) + dt_bias)
    log_a = dt * -jnp.exp(a_log)
    xdt = xs[:, :, None] * dt[..., None]
    y_f, s_f = decay_scan(cm, bm, xdt[:, :, 0], log_a[:, :, 0], s0[:, 0])
    y_b, s_b = decay_scan(_rev(cm), _rev(bm), _rev(xdt[:, :, 1]), _rev(log_a[:, :, 1]), s0[:, 1])
    y = (y_f + _rev(y_b) + xs * d_skip[:, None]).reshape(b, length, W_SSD) * jax.nn.silu(z)
    y = rms_norm(y.reshape(b, length, G_SSD, W_SSD // G_SSD),
                 norm_w.reshape(G_SSD, W_SSD // G_SSD)).reshape(b, length, W_SSD)
    return y, jnp.stack([s_f, s_b], axis=1)


def layer(x, mod, s_ret, s_gdn, s_ssd, rope, lw):
    length = x.shape[1]
    sh1, sc1, g1, sh2, sc2, g2 = jnp.split(mod[:, None, :], 6, axis=-1)
    h = rms_norm(x, lw['norm1']) * (1 + sc1) + sh1
    p_ret, p_hy, p_gdn, p_ssd = jnp.split(h @ lw['w_in'], SPLITS, axis=-1)
    o_ret, f_ret = retention_mixer(p_ret, lw['ret_log_decay'], lw['ret_norm'], s_ret, rope)
    taps = hyena_filter(length, lw['hy_freq'], lw['hy_w1'], lw['hy_b1'], lw['hy_w2'], lw['hy_b2'], lw['hy_w3'])
    o_hy = hyena_mixer(p_hy, lw['hy_conv_w'], lw['hy_conv_b'], taps, lw['hy_bias'], lw['hy_norm'])
    o_gdn, f_gdn = gdn_mixer(p_gdn, lw['gdn_conv_w'], lw['gdn_A_log'], lw['gdn_dt_bias'], lw['gdn_norm'], s_gdn)
    o_ssd, f_ssd = ssd_mixer(p_ssd, lw['ssd_conv_w'], lw['ssd_conv_b'], lw['ssd_A_log'], lw['ssd_dt_bias'],
                             lw['ssd_D'], lw['ssd_norm'], s_ssd)
    mixed = jnp.concatenate([o_ret, o_hy, o_gdn, o_ssd], axis=-1)
    x = x + g1 * (mixed @ lw['w_out'])
    h2 = rms_norm(x, lw['norm2']) * (1 + sc2) + sh2
    x = x + g2 * (jnp.square(jax.nn.relu(h2 @ lw['mlp_w1'])) @ lw['mlp_w2'])
    return x, f_ret, f_gdn, f_ssd


def trunk(x, cond, st_ret, st_gdn, st_ssd, rope, layers, final_norm_w):
    finals = []
    for l in range(DEPTH):
        lw = layers[l]
        mod = jax.nn.silu(cond) @ lw['w_mod'] + lw['b_mod']
        x, f_ret, f_gdn, f_ssd = layer(x, mod, st_ret[:, l], st_gdn[:, l], st_ssd[:, l], rope, lw)
        finals.append((f_ret, f_gdn, f_ssd))
    return rms_norm(x, final_norm_w), finals


def setup_inputs(seed: int = 0) -> dict:
    key = jax.random.key(seed)
    keys = jax.random.split(key, 64)
    counter = [0]

    def nk():
        counter[0] += 1
        return keys[counter[0] - 1]

    def nrm(shape, scale=1.0):
        return scale * jax.random.normal(nk(), shape, jnp.float32)

    def gain(shape):
        return 1.0 + 0.02 * jax.random.normal(nk(), shape, jnp.float32)

    def loguni(shape, lo, hi):
        return jnp.exp(jax.random.uniform(nk(), shape, jnp.float32, math.log(lo), math.log(hi)))

    def inv_softplus_dt(shape):
        dt = loguni(shape, 1e-3, 1e-1)
        return dt + jnp.log(-jnp.expm1(-dt))

    ret_base = jnp.log(1.0 - 2.0 ** (-5.0 - jnp.arange(H_RET, dtype=jnp.float32)))
    return {
        'x_prompt': nrm((BATCH, SEQ, D_MODEL)),
        'x_sample': nrm((DEC_BATCH, DEC_SEQ, D_MODEL)),
        'state_ret': nrm((DEC_BATCH, DEPTH, 2, H_RET, DK_RET, DK_RET), 0.1),
        'state_gdn': nrm((DEC_BATCH, DEPTH, 2, H_GDN, DK_GDN, DV_GDN), 0.1),
        'state_ssd': nrm((DEC_BATCH, DEPTH, 2, H_SSD, N_SSD, P_SSD), 0.1),
        'c': nrm((DEC_BATCH, D_MODEL)),
        'c_ctx': nrm((D_MODEL,)),
        'norm1_w': gain((DEPTH, D_MODEL)),
        'norm2_w': gain((DEPTH, D_MODEL)),
        'w_mod': nrm((DEPTH, D_MODEL, 6 * D_MODEL), 0.5 * D_MODEL ** -0.5),
        'b_mod': nrm((DEPTH, 6 * D_MODEL), 0.01),
        'w_in': nrm((DEPTH, D_MODEL, IN_COLS), D_MODEL ** -0.5),
        'w_out': nrm((DEPTH, D_MIX, D_MODEL), D_MIX ** -0.5),
        'ret_log_decay': ret_base * (1.0 + 0.05 * jax.random.normal(nk(), (DEPTH, 2, H_RET), jnp.float32)),
        'ret_norm_w': gain((DEPTH, W_GROUP)),
        'hy_conv_w': nrm((DEPTH, SHORT_CONV, HY_COLS), SHORT_CONV ** -0.5),
        'hy_conv_b': nrm((DEPTH, HY_COLS), 0.01),
        'hy_freq': gain((DEPTH, HY_HIDDEN)),
        'hy_w1': nrm((DEPTH, HY_EMB, HY_HIDDEN), HY_EMB ** -0.5),
        'hy_b1': nrm((DEPTH, HY_HIDDEN), 0.1),
        'hy_w2': nrm((DEPTH, HY_HIDDEN, HY_HIDDEN), HY_HIDDEN ** -0.5),
        'hy_b2': nrm((DEPTH, HY_HIDDEN), 0.1),
        'hy_w3': nrm((DEPTH, HY_HIDDEN, 2 * W_HY), HY_HIDDEN ** -0.5),
        'hy_bias': nrm((DEPTH, W_HY)),
        'hy_norm_w': gain((DEPTH, W_HY)),
        'gdn_conv_w': nrm((DEPTH, SHORT_CONV, 3 * W_GDN), SHORT_CONV ** -0.5),
        'gdn_A_log': jnp.log(jax.random.uniform(nk(), (DEPTH, 2, H_GDN), jnp.float32, 1.0, 16.0)),
        'gdn_dt_bias': inv_softplus_dt((DEPTH, 2, H_GDN)),
        'gdn_norm_w': gain((DEPTH, W_GDN)),
        'ssd_conv_w': nrm((DEPTH, SHORT_CONV, W_SSD + 2 * G_SSD * N_SSD), SHORT_CONV ** -0.5),
        'ssd_conv_b': nrm((DEPTH, W_SSD + 2 * G_SSD * N_SSD), 0.01),
        'ssd_A_log': jnp.log(jax.random.uniform(nk(), (DEPTH, 2, H_SSD), jnp.float32, 1.0, 16.0)),
        'ssd_dt_bias': inv_softplus_dt((DEPTH, 2, H_SSD)),
        'ssd_D': gain((DEPTH, H_SSD)),
        'ssd_norm_w': gain((DEPTH, W_SSD)),
        'mlp_w1': nrm((DEPTH, D_MODEL, D_FF), D_MODEL ** -0.5),
        'mlp_w2': nrm((DEPTH, D_FF, D_MODEL), D_FF ** -0.5),
        'final_norm_w': gain((D_MODEL,)),
    }


def reference(x_prompt, x_sample, state_ret, state_gdn, state_ssd, c, c_ctx,
              norm1_w, norm2_w, w_mod, b_mod, w_in, w_out,
              ret_log_decay, ret_norm_w,
              hy_conv_w, hy_conv_b, hy_freq, hy_w1, hy_b1, hy_w2, hy_b2, hy_w3, hy_bias, hy_norm_w,
              gdn_conv_w, gdn_A_log, gdn_dt_bias, gdn_norm_w,
              ssd_conv_w, ssd_conv_b, ssd_A_log, ssd_dt_bias, ssd_D, ssd_norm_w,
              mlp_w1, mlp_w2, final_norm_w):
    layers = [dict(
        norm1=norm1_w[l], norm2=norm2_w[l], w_mod=w_mod[l], b_mod=b_mod[l],
        w_in=w_in[l], w_out=w_out[l],
        ret_log_decay=ret_log_decay[l], ret_norm=ret_norm_w[l],
        hy_conv_w=hy_conv_w[l], hy_conv_b=hy_conv_b[l], hy_freq=hy_freq[l],
        hy_w1=hy_w1[l], hy_b1=hy_b1[l], hy_w2=hy_w2[l], hy_b2=hy_b2[l], hy_w3=hy_w3[l],
        hy_bias=hy_bias[l], hy_norm=hy_norm_w[l],
        gdn_conv_w=gdn_conv_w[l], gdn_A_log=gdn_A_log[l], gdn_dt_bias=gdn_dt_bias[l], gdn_norm=gdn_norm_w[l],
        ssd_conv_w=ssd_conv_w[l], ssd_conv_b=ssd_conv_b[l], ssd_A_log=ssd_A_log[l],
        ssd_dt_bias=ssd_dt_bias[l], ssd_D=ssd_D[l], ssd_norm=ssd_norm_w[l],
        mlp_w1=mlp_w1[l], mlp_w2=mlp_w2[l]) for l in range(DEPTH)]

    bsz = x_prompt.shape[0]
    dt = x_prompt.dtype
    z_ret = jnp.zeros((bsz, DEPTH, 2, H_RET, DK_RET, DK_RET), dt)
    z_gdn = jnp.zeros((bsz, DEPTH, 2, H_GDN, DK_GDN, DV_GDN), dt)
    z_ssd = jnp.zeros((bsz, DEPTH, 2, H_SSD, N_SSD, P_SSD), dt)
    y_prompt, ctx_finals = trunk(x_prompt, c_ctx[None, :], z_ret, z_gdn, z_ssd, None, layers, final_norm_w)
    new_state_ret = jnp.stack([f[0] for f in ctx_finals], axis=1)
    new_state_gdn = jnp.stack([f[1] for f in ctx_finals], axis=1)
    new_state_ssd = jnp.stack([f[2] for f in ctx_finals], axis=1)

    rope = axial_rope(x_sample.shape[1])
    y_sample, _ = trunk(x_sample, c, state_ret, state_gdn, state_ssd, rope, layers, final_norm_w)

    return (y_prompt, y_sample, new_state_ret, new_state_gdn, new_state_ssd)
```

```python
import functools
import math

import jax
import jax.numpy as jnp
from jax import lax
from jax.experimental import pallas as pl
from jax.experimental.pallas import tpu as pltpu

F32 = jnp.float32
BF16 = jnp.bfloat16

D_MODEL = 1024
DEPTH = 4
GRID_W = 64
W_GROUP = 256
N_HEADS = 4
D_HEAD = 64
N_SSD = 128
G_SSD = 2
HY_EMB = 33
HY_HIDDEN = 64
HY_DECAY_TARGET = 1e-2
HY_FAST_PCT = 0.3
HY_SLOW_PCT = 1.5
D_FF = 4 * D_MODEL
ROPE_BASE = 10000.0
EPS = 1e-6
IN_MAIN = 3840
SMALL_W = 128
MOD_ROWS = 8
CTX_MOD_ROW = 4

TOK_TILE = 256
SCAN_CHUNK = 256
GDN_CHUNK = 64
HY_FTILE = 256
NEG_BIG = -1e30
VMEM_LIMIT = 56 * 1024 * 1024

NN = (((1,), (0,)), ((), ()))
NT = (((1,), (1,)), ((), ()))
TN = (((0,), (0,)), ((), ()))


def _dot(a, b, dims=NN):
    return lax.dot_general(a.astype(BF16), b.astype(BF16), dims, preferred_element_type=F32)


def _dot_hp(a, b, dims=NN):
    return lax.dot_general(a.astype(F32), b.astype(F32), dims, precision=lax.Precision.HIGHEST,
                           preferred_element_type=F32)


def _split(x):
    hi = x.astype(BF16)
    lo = (x - hi.astype(F32)).astype(BF16)
    return hi, lo


def _dot3(a_hi, a_lo, b_hi, b_lo, dims=NN):
    d = functools.partial(lax.dot_general, dimension_numbers=dims, preferred_element_type=F32)
    return d(a_hi, b_hi) + d(a_hi, b_lo) + d(a_lo, b_hi)


def _sigmoid(x):
    return 1.0 / (1.0 + jnp.exp(-x))


def _silu(x):
    return x * _sigmoid(x)


def _softplus(x):
    return jnp.maximum(x, 0.0) + jnp.log1p(jnp.exp(-jnp.abs(x)))


def _iota(shape, dim):
    return lax.broadcasted_iota(jnp.int32, shape, dim)


def _params(sem):
    return pltpu.CompilerParams(dimension_semantics=sem, vmem_limit_bytes=VMEM_LIMIT)


def _head_ones(n=W_GROUP, width=D_HEAD):
    return jnp.where(_iota((n, n), 0) // width == _iota((n, n), 1) // width, 1.0, 0.0).astype(F32)


def _mod_kernel(cond_ref, w_ref, b_ref, o_ref):
    s = _silu(cond_ref[...])
    o_ref[0] = _dot(s, w_ref[0]) + b_ref[0]


def _modulation(cond8, w_mod, b_mod):
    tn = 1536
    n_out = 6 * D_MODEL
    return pl.pallas_call(
        _mod_kernel,
        out_shape=jax.ShapeDtypeStruct((DEPTH, MOD_ROWS, n_out), F32),
        grid=(DEPTH, n_out // tn),
        in_specs=[
            pl.BlockSpec((MOD_ROWS, D_MODEL), lambda l, j: (0, 0)),
            pl.BlockSpec((1, D_MODEL, tn), lambda l, j: (l, 0, j)),
            pl.BlockSpec((1, 1, tn), lambda l, j: (l, 0, j)),
        ],
        out_specs=pl.BlockSpec((1, MOD_ROWS, tn), lambda l, j: (l, 0, j)),
        compiler_params=_params(("arbitrary", "arbitrary")),
        name="modulation",
    )(cond8, w_mod, b_mod.reshape(DEPTH, 1, n_out))


def _mod_spec(layer, k, row_fn):
    return pl.BlockSpec((1, 1, D_MODEL), lambda i: ((layer * MOD_ROWS + row_fn(i)) * 6 + k, 0, 0))


def _rms(x, w):
    return x * lax.rsqrt(jnp.mean(x * x, axis=-1, keepdims=True) + EPS) * w


def _in_kernel(x_ref, nw_ref, sh_ref, sc_ref, wm_ref, ws_ref, pret, phy, pgdn, pssd, psm):
    h = _rms(x_ref[...], nw_ref[0]) * (1.0 + sc_ref[0]) + sh_ref[0]
    hb = h.astype(BF16)
    d = functools.partial(jnp.dot, preferred_element_type=F32)
    pret[...] = d(hb, wm_ref[0, :, 0:1024])
    phy[...] = d(hb, wm_ref[0, :, 1024:1792])
    pgdn[...] = d(hb, wm_ref[0, :, 1792:2816])
    pssd[...] = d(hb, wm_ref[0, :, 2816:3840])
    psm[...] = d(hb, ws_ref[0])


def _in_proj(x, mod3, norm1_w3, w_main, w_small, layer, row_fn):
    n = x.shape[0]
    widths = (1024, 768, 1024, 1024, SMALL_W)
    return pl.pallas_call(
        _in_kernel,
        out_shape=tuple(jax.ShapeDtypeStruct((n, w), F32) for w in widths),
        grid=(n // TOK_TILE,),
        in_specs=[
            pl.BlockSpec((TOK_TILE, D_MODEL), lambda i: (i, 0)),
            pl.BlockSpec((1, 1, D_MODEL), lambda i: (layer, 0, 0)),
            _mod_spec(layer, 0, row_fn),
            _mod_spec(layer, 1, row_fn),
            pl.BlockSpec((1, D_MODEL, IN_MAIN), lambda i: (layer, 0, 0)),
            pl.BlockSpec((1, D_MODEL, SMALL_W), lambda i: (layer, 0, 0)),
        ],
        out_specs=tuple(pl.BlockSpec((TOK_TILE, w), lambda i: (i, 0)) for w in widths),
        compiler_params=_params(("arbitrary",)),
        name="in_proj",
    )(x, norm1_w3, mod3, mod3, w_main, w_small)


def _out_kernel(final, oret, ohy, ogdn, ossd, x_ref, g1, sh2, sc2, g2, nw2, wout, w1, w2, *rest):
    if final:
        fnw, y_ref = rest
    else:
        (y_ref,) = rest
    acc = _dot(oret[...], wout[0, 0:256, :])
    acc = acc + _dot(ohy[...], wout[0, 256:512, :])
    acc = acc + _dot(ogdn[...], wout[0, 512:768, :])
    acc = acc + _dot(ossd[...], wout[0, 768:1024, :])
    x1 = x_ref[...] + g1[0] * acc
    h2 = _rms(x1, nw2[0]) * (1.0 + sc2[0]) + sh2[0]
    a = _dot(h2, w1[0])
    a = jnp.square(jnp.maximum(a, 0.0))
    x2 = x1 + g2[0] * _dot(a, w2[0])
    if final:
        x2 = _rms(x2, fnw[...])
    y_ref[...] = x2


def _out_proj(mix, x, mod3, norm2_w3, w_out, w1, w2, final_norm_w, layer, row_fn, final):
    n = x.shape[0]
    tok = lambda w: pl.BlockSpec((TOK_TILE, w), lambda i: (i, 0))
    lay = lambda a, b: pl.BlockSpec((1, a, b), lambda i: (layer, 0, 0))
    in_specs = [tok(W_GROUP)] * 4 + [
        tok(D_MODEL),
        _mod_spec(layer, 2, row_fn), _mod_spec(layer, 3, row_fn),
        _mod_spec(layer, 4, row_fn), _mod_spec(layer, 5, row_fn),
        lay(1, D_MODEL), lay(D_MODEL, D_MODEL), lay(D_MODEL, D_FF), lay(D_FF, D_MODEL),
    ]
    args = list(mix) + [x, mod3, mod3, mod3, mod3, norm2_w3, w_out, w1, w2]
    if final:
        in_specs.append(pl.BlockSpec((1, D_MODEL), lambda i: (0, 0)))
        args.append(final_norm_w.reshape(1, D_MODEL))
    return pl.pallas_call(
        functools.partial(_out_kernel, final),
        out_shape=jax.ShapeDtypeStruct((n, D_MODEL), F32),
        grid=(n // TOK_TILE,),
        in_specs=in_specs,
        out_specs=tok(D_MODEL),
        compiler_params=_params(("arbitrary",)),
        name="out_proj_mlp",
    )(*args)


def _conv3_chunk(p_ref, r0, rows, seq_len, c0, width, w_ref, first, last):
    x = p_ref[pl.ds(r0, rows), c0:c0 + width]
    prev_i = jnp.maximum(r0 - 1, 0)
    next_i = jnp.minimum(r0 + rows, seq_len - 1)
    xp = p_ref[pl.ds(prev_i, 1), c0:c0 + width] * jnp.where(first, 0.0, 1.0)
    xn = p_ref[pl.ds(next_i, 1), c0:c0 + width] * jnp.where(last, 0.0, 1.0)
    row = _iota((rows, width), 0)
    x_prev = jnp.where(row == 0, xp, pltpu.roll(x, 1, 0))
    x_next = jnp.where(row == rows - 1, xn, pltpu.roll(x, rows - 1, 0))
    return x_prev * w_ref[0:1, :] + x * w_ref[1:2, :] + x_next * w_ref[2:3, :]


def _tri(n, lower):
    i, j = _iota((n, n), 0), _iota((n, n), 1)
    return jnp.where((j <= i) if lower else (j >= i), 1.0, 0.0).astype(F32)


def _lane_expand(first_row, width):
    r, c = _iota((SMALL_W, W_GROUP), 0), _iota((SMALL_W, W_GROUP), 1)
    return jnp.where(r == first_row + c // width, 1.0, 0.0).astype(F32)


def _ret_kernel(n_chunks, rope, has_s0, *refs):
    C = SCAN_CHUNK
    refs = list(refs)
    p_ref, la_ref, nw_ref = refs[:3]
    refs = refs[3:]
    if rope:
        cos_ref, sin_ref = refs[:2]
        refs = refs[2:]
    if has_s0:
        s0_ref = refs[0]
        refs = refs[1:]
    o_ref, st_ref, sf, sb, dmat, qs, ks, oscr = refs

    la_f = la_ref[0:1, :]
    la_b = la_ref[1:2, :]
    ri = _iota((C, W_GROUP), 0).astype(F32)
    e_f = jnp.exp((ri + 1.0) * la_f)
    e_b = jnp.exp((C - ri) * la_b)
    w_f = jnp.exp((C - 1.0 - ri) * la_f)
    w_b = jnp.exp(ri * la_b)
    dc_f = jnp.exp(C * la_f)
    dc_b = jnp.exp(C * la_b)
    bd = _head_ones()
    lane = _iota((1, W_GROUP), 1)
    ii, jj = _iota((C, C), 0), _iota((C, C), 1)
    diff = (ii - jj).astype(F32)
    for h in range(N_HEADS):
        laf_h = la_ref[0:1, D_HEAD * h:D_HEAD * h + 1]
        lab_h = la_ref[1:2, D_HEAD * h:D_HEAD * h + 1]
        dmat[h] = (jnp.exp(jnp.where(jj <= ii, diff * laf_h, NEG_BIG))
                   + jnp.exp(jnp.where(jj >= ii, -diff * lab_h, NEG_BIG)))

    sf[...] = jnp.zeros_like(sf)
    sb[...] = jnp.zeros_like(sb)
    if has_s0:
        for h in range(N_HEADS):
            sl = slice(D_HEAD * h, D_HEAD * (h + 1))
            sf[sl, sl] = s0_ref[0, 0, 0, h]
            sb[sl, sl] = s0_ref[0, 0, 1, h]

    def rot(x, cs, sn):
        halves = []
        for a in range(2):
            xh = x[:, 128 * a:128 * (a + 1)]
            lh = _iota(xh.shape, 1)
            halves.append(jnp.where(lh % D_HEAD < D_HEAD // 2, pltpu.roll(xh, 96, 1), pltpu.roll(xh, 32, 1)))
        return x * cs + jnp.concatenate(halves, axis=1) * sn

    def fwd(c, carry):
        r0 = pl.multiple_of(c * C, C)
        rows = pl.ds(r0, C)
        q = p_ref[rows, 0:256]
        k = p_ref[rows, 256:512] * (D_HEAD ** -0.5)
        v = p_ref[rows, 512:768]
        if rope:
            cs, sn = cos_ref[rows, :], sin_ref[rows, :]
            q, k = rot(q, cs, sn), rot(k, cs, sn)
        qs[rows, :] = q
        ks[rows, :] = k
        o = _dot(q * e_f, sf[...])
        for h in range(N_HEADS):
            mh = jnp.where(lane // D_HEAD == h, 1.0, 0.0)
            att = _dot(q * mh, k, NT) * dmat[h]
            o = o + _dot(att, v * mh)
        oscr[rows, :] = o
        sf[...] = sf[...] * dc_f + bd * _dot(k * w_f, v, TN)
        return carry

    lax.fori_loop(0, n_chunks, fwd, 0)

    def bwd(t, carry):
        c = n_chunks - 1 - t
        r0 = pl.multiple_of(c * C, C)
        rows = pl.ds(r0, C)
        q, k = qs[rows, :], ks[rows, :]
        v = p_ref[rows, 512:768]
        g = p_ref[rows, 768:1024]
        o = oscr[rows, :] + _dot(q * e_b, sb[...])
        sb[...] = sb[...] * dc_b + bd * _dot(k * w_b, v, TN)
        ms = _dot_hp(o * o, bd) * (1.0 / D_HEAD)
        o_ref[rows, :] = o * lax.rsqrt(ms + EPS) * nw_ref[...] * _silu(g)
        return carry

    lax.fori_loop(0, n_chunks, bwd, 0)

    for h in range(N_HEADS):
        sl = slice(D_HEAD * h, D_HEAD * (h + 1))
        st_ref[0, 0, h] = sf[sl, sl]
        st_ref[0, 1, h] = sb[sl, sl]


def _retention(p_ret, la_lane, nw, bsz, seq_len, rope_tabs, s0, layer):
    n_chunks = seq_len // SCAN_CHUNK
    rope = rope_tabs is not None
    has_s0 = s0 is not None
    const = lambda a, b: pl.BlockSpec((a, b), lambda i: (0, 0))
    in_specs = [pl.BlockSpec((seq_len, 1024), lambda i: (i, 0)), const(2, W_GROUP), const(1, W_GROUP)]
    args = [p_ret, la_lane, nw]
    if rope:
        in_specs += [const(seq_len, W_GROUP)] * 2
        args += list(rope_tabs)
    if has_s0:
        in_specs.append(pl.BlockSpec((1, 1, 2, N_HEADS, D_HEAD, D_HEAD), lambda i: (i, layer, 0, 0, 0, 0)))
        args.append(s0)
    return pl.pallas_call(
        functools.partial(_ret_kernel, n_chunks, rope, has_s0),
        out_shape=(jax.ShapeDtypeStruct((bsz * seq_len, W_GROUP), F32),
                   jax.ShapeDtypeStruct((bsz, 2, N_HEADS, D_HEAD, D_HEAD), F32)),
        grid=(bsz,),
        in_specs=in_specs,
        out_specs=(pl.BlockSpec((seq_len, W_GROUP), lambda i: (i, 0)),
                   pl.BlockSpec((1, 2, N_HEADS, D_HEAD, D_HEAD), lambda i: (i, 0, 0, 0, 0))),
        scratch_shapes=[pltpu.VMEM((W_GROUP, W_GROUP), F32), pltpu.VMEM((W_GROUP, W_GROUP), F32),
                        pltpu.VMEM((N_HEADS, SCAN_CHUNK, SCAN_CHUNK), F32),
                        pltpu.VMEM((seq_len, W_GROUP), F32), pltpu.VMEM((seq_len, W_GROUP), F32),
                        pltpu.VMEM((seq_len, W_GROUP), F32)],
        compiler_params=_params(("arbitrary",)),
        name="retention",
    )(*args)


def _ssd_kernel(n_chunks, has_s0, *refs):
    C = SCAN_CHUNK
    refs = list(refs)
    p_ref, ps_ref, cw_ref, cb_ref, dtb_ref, alog_ref, dl_ref, nw_ref = refs[:8]
    refs = refs[8:]
    if has_s0:
        s0_ref = refs[0]
        refs = refs[1:]
    o_ref, st_ref, sf, sb, xs_s, bm_s, cm_s, oscr = refs
    seq_len = n_chunks * C

    def prep(c, carry):
        r0 = pl.multiple_of(c * C, C)
        y = _conv3_chunk(p_ref, r0, C, seq_len, 0, 768, cw_ref, c == 0, c == n_chunks - 1) + cb_ref[...]
        y = _silu(y)
        rows = pl.ds(r0, C)
        xs_s[rows, :] = y[:, 0:256]
        bm_s[rows, :] = y[:, 256:512]
        cm_s[rows, :] = y[:, 512:768]
        return carry

    lax.fori_loop(0, n_chunks, prep, 0)

    sf[...] = jnp.zeros_like(sf)
    sb[...] = jnp.zeros_like(sb)
    if has_s0:
        for h in range(N_HEADS):
            g = h // 2
            sf[N_SSD * g:N_SSD * (g + 1), D_HEAD * h:D_HEAD * (h + 1)] = s0_ref[0, 0, 0, h]
            sb[N_SSD * g:N_SSD * (g + 1), D_HEAD * h:D_HEAD * (h + 1)] = s0_ref[0, 0, 1, h]

    ltri, utri = _tri(C, True), _tri(C, False)
    exp_f, exp_b = _lane_expand(16, D_HEAD), _lane_expand(20, D_HEAD)
    gmask = _head_ones(W_GROUP, N_SSD)
    lane_s = _iota((1, SMALL_W), 1)
    dt_mask = jnp.where((lane_s >= 16) & (lane_s < 24), 1.0, 0.0)
    lane = _iota((1, W_GROUP), 1)
    ii, jj = _iota((C, C), 0), _iota((C, C), 1)

    def gates(rows):
        dt = _softplus(ps_ref[rows, :] + dtb_ref[...]) * dt_mask
        la = dt * (-jnp.exp(alog_ref[...]))
        return dt, la

    def fwd(c, carry):
        r0 = pl.multiple_of(c * C, C)
        rows = pl.ds(r0, C)
        dt, la = gates(rows)
        cum_p = _dot_hp(ltri, la)
        cum_s = _dot_hp(utri, la)
        cum_pt, cum_st, dt_t = cum_p.T, cum_s.T, dt.T
        x_f = _dot_hp(cum_p, exp_f)
        dt_f = _dot_hp(dt, exp_f)
        xs, bm, cm = xs_s[rows, :], bm_s[rows, :], cm_s[rows, :]
        s_g = [_dot(cm[:, N_SSD * g:N_SSD * (g + 1)], bm[:, N_SSD * g:N_SSD * (g + 1)], NT) for g in range(G_SSD)]
        o = _dot(cm, sf[...]) * jnp.exp(x_f)
        for h in range(N_HEADS):
            f, b = 16 + h, 20 + h
            dec = (jnp.exp(jnp.where(jj <= ii, cum_p[:, f:f + 1] - cum_pt[f:f + 1, :], NEG_BIG)) * dt_t[f:f + 1, :]
                   + jnp.exp(jnp.where(jj >= ii, cum_s[:, b:b + 1] - cum_st[b:b + 1, :], NEG_BIG)) * dt_t[b:b + 1, :])
            mh = jnp.where(lane // D_HEAD == h, 1.0, 0.0)
            o = o + _dot(s_g[h // 2] * dec, xs * mh)
        oscr[rows, :] = o
        last = x_f[C - 1:C, :]
        sf[...] = sf[...] * jnp.exp(last) + gmask * _dot(bm, xs * dt_f * jnp.exp(last - x_f), TN)
        return carry

    lax.fori_loop(0, n_chunks, fwd, 0)

    def bwd(t, carry):
        c = n_chunks - 1 - t
        r0 = pl.multiple_of(c * C, C)
        rows = pl.ds(r0, C)
        dt, la = gates(rows)
        cum_s = _dot_hp(utri, la)
        x_b = _dot_hp(cum_s, exp_b)
        dt_b = _dot_hp(dt, exp_b)
        xs, bm, cm = xs_s[rows, :], bm_s[rows, :], cm_s[rows, :]
        o = oscr[rows, :] + _dot(cm, sb[...]) * jnp.exp(x_b)
        first = x_b[0:1, :]
        sb[...] = sb[...] * jnp.exp(first) + gmask * _dot(bm, xs * dt_b * jnp.exp(first - x_b), TN)
        y = (o + xs * dl_ref[...]) * _silu(p_ref[rows, 768:1024])
        halves = [y[:, 128 * g:128 * (g + 1)] for g in range(G_SSD)]
        halves = [yh * lax.rsqrt(jnp.mean(yh * yh, axis=-1, keepdims=True) + EPS) for yh in halves]
        o_ref[rows, :] = jnp.concatenate(halves, axis=1) * nw_ref[...]
        return carry

    lax.fori_loop(0, n_chunks, bwd, 0)

    for h in range(N_HEADS):
        g = h // 2
        st_ref[0, 0, h] = sf[N_SSD * g:N_SSD * (g + 1), D_HEAD * h:D_HEAD * (h + 1)]
        st_ref[0, 1, h] = sb[N_SSD * g:N_SSD * (g + 1), D_HEAD * h:D_HEAD * (h + 1)]


def _ssd(p_ssd, p_small, conv_w, conv_b, dtb, alog, d_lane, nw, bsz, seq_len, s0, layer):
    n_chunks = seq_len // SCAN_CHUNK
    has_s0 = s0 is not None
    const = lambda a, b: pl.BlockSpec((a, b), lambda i: (0, 0))
    in_specs = [pl.BlockSpec((seq_len, 1024), lambda i: (i, 0)), pl.BlockSpec((seq_len, SMALL_W), lambda i: (i, 0)),
                const(3, 768), const(1, 768), const(1, SMALL_W), const(1, SMALL_W), const(1, W_GROUP),
                const(1, W_GROUP)]
    args = [p_ssd, p_small, conv_w, conv_b, dtb, alog, d_lane, nw]
    if has_s0:
        in_specs.append(pl.BlockSpec((1, 1, 2, N_HEADS, N_SSD, D_HEAD), lambda i: (i, layer, 0, 0, 0, 0)))
        args.append(s0)
    return pl.pallas_call(
        functools.partial(_ssd_kernel, n_chunks, has_s0),
        out_shape=(jax.ShapeDtypeStruct((bsz * seq_len, W_GROUP), F32),
                   jax.ShapeDtypeStruct((bsz, 2, N_HEADS, N_SSD, D_HEAD), F32)),
        grid=(bsz,),
        in_specs=in_specs,
        out_specs=(pl.BlockSpec((seq_len, W_GROUP), lambda i: (i, 0)),
                   pl.BlockSpec((1, 2, N_HEADS, N_SSD, D_HEAD), lambda i: (i, 0, 0, 0, 0))),
        scratch_shapes=[pltpu.VMEM((W_GROUP, W_GROUP), F32), pltpu.VMEM((W_GROUP, W_GROUP), F32)]
        + [pltpu.VMEM((seq_len, W_GROUP), F32)] * 4,
        compiler_params=_params(("arbitrary",)),
        name="ssd",
    )(*args)


def _gdn_kernel(n_pre, has_s0, *refs):
    C = GDN_CHUNK
    CP = SCAN_CHUNK
    R = N_HEADS * C
    refs = list(refs)
    p_ref, ps_ref, cw_ref, dtb_ref, alog_ref, nw_ref = refs[:6]
    refs = refs[6:]
    if has_s0:
        s0_ref = refs[0]
        refs = refs[1:]
    o_ref, st_ref, sf, sb, q_s, k_s, v_s, oscr = refs
    seq_len = n_pre * CP
    n_chunks = seq_len // C
    bd = _head_ones()

    def prep(c, carry):
        r0 = pl.multiple_of(c * CP, CP)
        y = _silu(_conv3_chunk(p_ref, r0, CP, seq_len, 0, 768, cw_ref, c == 0, c == n_pre - 1))
        q, k, v = y[:, 0:256], y[:, 256:512], y[:, 512:768]
        q = q * lax.rsqrt(_dot_hp(q * q, bd) + EPS) * (D_HEAD ** -0.5)
        k = k * lax.rsqrt(_dot_hp(k * k, bd) + EPS)
        rows = pl.ds(r0, CP)
        q_s[rows, :] = q
        k_s[rows, :] = k
        v_s[rows, :] = v
        return carry

    lax.fori_loop(0, n_pre, prep, 0)

    sf[...] = jnp.zeros_like(sf)
    sb[...] = jnp.zeros_like(sb)
    if has_s0:
        for h in range(N_HEADS):
            sl = slice(D_HEAD * h, D_HEAD * (h + 1))
            sf[sl, sl] = s0_ref[0, 0, 0, h]
            sb[sl, sl] = s0_ref[0, 0, 1, h]

    lane_s = _iota((1, SMALL_W), 1)
    la_mask = jnp.where(lane_s < 8, 1.0, 0.0)
    rr, cc = _iota((R, R), 0), _iota((R, R), 1)
    same = (rr // C) == (cc // C)
    head_lanes = jnp.where(_iota((R, W_GROUP), 0) // C == _iota((R, W_GROUP), 1) // D_HEAD, 1.0, 0.0)
    row_head = _iota((R, SMALL_W), 0) // C
    lane_sm = _iota((R, SMALL_W), 1)
    eye = jnp.where(rr == cc, 1.0, 0.0)

    def stack(x):
        return jnp.concatenate([x] * N_HEADS, axis=0)

    def chunk(c, direction, s_ref):
        fwd = direction == 0
        r0 = pl.multiple_of(c * C, C)
        rows = pl.ds(r0, C)
        g = ps_ref[rows, :]
        la = -jnp.exp(alog_ref[...]) * _softplus(g + dtb_ref[...]) * la_mask
        beta = _sigmoid(g)
        cum = _dot_hp(_tri(C, fwd), la)
        sel_la = jnp.where(lane_sm == 4 * direction + row_head, 1.0, 0.0)
        sel_beta = jnp.where(lane_sm == 8 + 4 * direction + row_head, 1.0, 0.0)
        cum_st = jnp.sum(stack(cum) * sel_la, axis=1, keepdims=True)
        beta_st = jnp.sum(stack(beta) * sel_beta, axis=1, keepdims=True)
        last_row = cum[C - 1:C, :] if fwd else cum[0:1, :]
        last_st = jnp.sum(jnp.broadcast_to(last_row, (R, SMALL_W)) * sel_la, axis=1, keepdims=True)
        last_lane = _dot_hp(jnp.broadcast_to(last_row, (8, SMALL_W)), _lane_expand(4 * direction, D_HEAD))[0:1, :]
        lhs = jnp.where(lane_sm == 0, cum_st, jnp.where(lane_sm == 1, 1.0, 0.0))
        rhs = jnp.where(lane_sm == 0, 1.0, jnp.where(lane_sm == 1, -cum_st, 0.0))
        dm = _dot_hp(lhs, rhs, NT)
        tri_incl = (cc <= rr) if fwd else (cc >= rr)
        tri_strict = (cc < rr) if fwd else (cc > rr)
        dec = jnp.exp(jnp.where(same & tri_incl, dm, NEG_BIG))
        qst = stack(q_s[rows, :]) * head_lanes
        kst = stack(k_s[rows, :]) * head_lanes
        vst = stack(v_s[rows, :]) * head_lanes
        kk = _dot(kst, kst, NT)
        qk = _dot(qst, kst, NT)
        a = jnp.where(same & tri_strict, dec * kk, 0.0) * beta_st
        x = eye - jnp.where((rr // 2) == (cc // 2), a, 0.0)
        s = 2
        while s < C:
            blk = ((rr // (2 * s)) == (cc // (2 * s))) & ((rr // s) != (cc // s))
            x = x - _dot_hp(x, _dot_hp(jnp.where(blk, a, 0.0), x))
            s *= 2
        e_cum = jnp.exp(cum_st)
        wm = _dot_hp(x, kst * (beta_st * e_cum))
        um = _dot_hp(x, vst * beta_st)
        st = s_ref[...]
        v_new = um - _dot(wm, st)
        ost = _dot(qst * e_cum, st) + _dot(qk * dec, v_new)
        o = ost[0:C] + ost[C:2 * C] + ost[2 * C:3 * C] + ost[3 * C:4 * C]
        s_ref[...] = st * jnp.exp(last_lane) + _dot(kst * jnp.exp(last_st - cum_st), v_new, TN)
        return rows, o

    def fwd_body(c, carry):
        rows, o = chunk(c, 0, sf)
        oscr[rows, :] = o
        return carry

    lax.fori_loop(0, n_chunks, fwd_body, 0)

    def bwd_body(t, carry):
        rows, o = chunk(n_chunks - 1 - t, 1, sb)
        o = o + oscr[rows, :]
        ms = _dot_hp(o * o, bd) * (1.0 / D_HEAD)
        o_ref[rows, :] = o * lax.rsqrt(ms + EPS) * nw_ref[...] * _silu(p_ref[rows, 768:1024])
        return carry

    lax.fori_loop(0, n_chunks, bwd_body, 0)

    for h in range(N_HEADS):
        sl = slice(D_HEAD * h, D_HEAD * (h + 1))
        st_ref[0, 0, h] = sf[sl, sl]
        st_ref[0, 1, h] = sb[sl, sl]


def _gdn(p_gdn, p_small, conv_w, dtb, alog, nw, bsz, seq_len, s0, layer):
    n_pre = seq_len // SCAN_CHUNK
    has_s0 = s0 is not None
    const = lambda a, b: pl.BlockSpec((a, b), lambda i: (0, 0))
    in_specs = [pl.BlockSpec((seq_len, 1024), lambda i: (i, 0)), pl.BlockSpec((seq_len, SMALL_W), lambda i: (i, 0)),
                const(3, 768), const(1, SMALL_W), const(1, SMALL_W), const(1, W_GROUP)]
    args = [p_gdn, p_small, conv_w, dtb, alog, nw]
    if has_s0:
        in_specs.append(pl.BlockSpec((1, 1, 2, N_HEADS, D_HEAD, D_HEAD), lambda i: (i, layer, 0, 0, 0, 0)))
        args.append(s0)
    return pl.pallas_call(
        functools.partial(_gdn_kernel, n_pre, has_s0),
        out_shape=(jax.ShapeDtypeStruct((bsz * seq_len, W_GROUP), F32),
                   jax.ShapeDtypeStruct((bsz, 2, N_HEADS, D_HEAD, D_HEAD), F32)),
        grid=(bsz,),
        in_specs=in_specs,
        out_specs=(pl.BlockSpec((seq_len, W_GROUP), lambda i: (i, 0)),
                   pl.BlockSpec((1, 2, N_HEADS, D_HEAD, D_HEAD), lambda i: (i, 0, 0, 0, 0))),
        scratch_shapes=[pltpu.VMEM((W_GROUP, W_GROUP), F32), pltpu.VMEM((W_GROUP, W_GROUP), F32)]
        + [pltpu.VMEM((seq_len, W_GROUP), F32)] * 4,
        compiler_params=_params(("arbitrary",)),
        name="gated_delta",
    )(*args)


def _dft_tables(seq_len):
    m = 2 * seq_len
    f = jnp.arange(seq_len, dtype=jnp.int32)
    k = ((2 * f[:, None] + 1) * (2 * f[None, :] + 1)) % (4 * m)
    ang = k.astype(F32) * (math.pi / (2 * m))
    c2, s2 = jnp.cos(ang), jnp.sin(ang)
    half = (2 * f + 1).astype(F32) * (math.pi / (2 * m))
    return _split(c2) + _split(s2) + (jnp.cos(half)[:, None], jnp.sin(half)[:, None])


def _filter_features(seq_len):
    t = jnp.linspace(0.0, 1.0, seq_len, dtype=F32)[:, None]
    bands = (HY_EMB - 1) // 2
    w = 2.0 * math.pi * jnp.arange(seq_len, dtype=F32)[:, None] / seq_len
    f = jnp.linspace(1e-4, bands - 1, bands, dtype=F32)[None, :]
    z = jnp.concatenate([t, jnp.cos(f * w), -jnp.sin(f * w)], axis=-1)
    z = jnp.pad(z, ((0, 0), (0, 128 - HY_EMB)))
    deltas = jnp.abs(jnp.linspace(math.log(HY_DECAY_TARGET) / HY_SLOW_PCT,
                                  math.log(HY_DECAY_TARGET) / HY_FAST_PCT, W_GROUP, dtype=F32))
    return z, jnp.exp(-t * deltas)


def _filter_kernel(z_ref, dec_ref, fr_ref, w1_ref, b1_ref, w2_ref, b2_ref, w3_ref,
                   chi, clo, shi, slo, cw_ref, sw_ref, hre_ref, him_ref, ps, pd):
    @pl.when(pl.program_id(0) == 0)
    def _():
        fr = fr_ref[0]
        h = jnp.sin(fr * (_dot_hp(z_ref[...], w1_ref[0]) + b1_ref[0]))
        h = jnp.sin(fr * (_dot_hp(h, w2_ref[0]) + b2_ref[0]))
        taps = _dot_hp(h, w3_ref[0])
        hf = taps[:, 0:W_GROUP] * dec_ref[...]
        hb = taps[:, W_GROUP:2 * W_GROUP] * dec_ref[...]
        hb = jnp.where(_iota(hb.shape, 0) == 0, 0.0, hb)
        s_hi, s_lo = _split(hf + hb)
        d_hi, d_lo = _split(hf - hb)
        ps[0], ps[1] = s_hi, s_lo
        pd[0], pd[1] = d_hi, d_lo

    c_sum = _dot3(chi[...], clo[...], ps[0], ps[1])
    s_sum = _dot3(shi[...], slo[...], ps[0], ps[1])
    c_dif = _dot3(chi[...], clo[...], pd[0], pd[1])
    s_dif = _dot3(shi[...], slo[...], pd[0], pd[1])
    cw, sw = cw_ref[...], sw_ref[...]
    hre_ref[...] = cw * c_sum + sw * s_sum
    him_ref[...] = sw * c_dif - cw * s_dif


def _hyena_spectrum(tabs, feats, freq, w1, b1, w2, b2, w3, seq_len, layer):
    chi, clo, shi, slo, cwv, swv = tabs
    z, dec = feats
    ft = min(HY_FTILE, seq_len)
    const = lambda a, b: pl.BlockSpec((a, b), lambda j: (0, 0))
    lay = lambda a, b: pl.BlockSpec((1, a, b), lambda j: (layer, 0, 0))
    rowt = lambda w: pl.BlockSpec((ft, w), lambda j: (j, 0))
    return pl.pallas_call(
        _filter_kernel,
        out_shape=(jax.ShapeDtypeStruct((seq_len, W_GROUP), F32),) * 2,
        grid=(seq_len // ft,),
        in_specs=[const(seq_len, 128), const(seq_len, W_GROUP), lay(1, HY_HIDDEN), lay(128, HY_HIDDEN),
                  lay(1, HY_HIDDEN), lay(HY_HIDDEN, HY_HIDDEN), lay(1, HY_HIDDEN), lay(HY_HIDDEN, 2 * W_GROUP),
                  rowt(seq_len), rowt(seq_len), rowt(seq_len), rowt(seq_len), rowt(1), rowt(1)],
        out_specs=(rowt(W_GROUP), rowt(W_GROUP)),
        scratch_shapes=[pltpu.VMEM((2, seq_len, W_GROUP), BF16), pltpu.VMEM((2, seq_len, W_GROUP), BF16)],
        compiler_params=_params(("arbitrary",)),
        name="hyena_filter",
    )(z, dec, freq, w1, b1, w2, b2, w3, chi, clo, shi, slo, cwv, swv)


def _hyena_kernel(n_chunks, n_ft, p_ref, cw_ref, cb_ref, bias_ref, nw_ref, hre_ref, him_ref,
                  rchi, rclo, rshi, rslo, cchi, cclo, cshi, cslo, o_ref, x0_s, u_s, us_s, yacc):
    C = SCAN_CHUNK
    seq_len = n_chunks * C
    j = pl.program_id(1)

    @pl.when(j == 0)
    def _():
        def prep(c, carry):
            r0 = pl.multiple_of(c * C, C)
            y = _conv3_chunk(p_ref, r0, C, seq_len, 0, 768, cw_ref, c == 0, c == n_chunks - 1) + cb_ref[...]
            rows = pl.ds(r0, C)
            u = y[:, 512:768] * y[:, 256:512]
            x0_s[rows, :] = y[:, 0:256]
            u_s[rows, :] = u
            u_hi, u_lo = _split(u)
            us_s[0, rows, :] = u_hi
            us_s[1, rows, :] = u_lo
            return carry

        lax.fori_loop(0, n_chunks, prep, 0)
        yacc[...] = jnp.zeros_like(yacc)

    u_re = _dot3(rchi[...], rclo[...], us_s[0], us_s[1])
    u_im = -_dot3(rshi[...], rslo[...], us_s[0], us_s[1])
    hre, him = hre_ref[...], him_ref[...]
    yre_hi, yre_lo = _split(u_re * hre - u_im * him)
    yim_hi, yim_lo = _split(u_re * him + u_im * hre)
    yacc[...] += (_dot3(cchi[...], cclo[...], yre_hi, yre_lo)
                  - _dot3(cshi[...], cslo[...], yim_hi, yim_lo))

    @pl.when(j == n_ft - 1)
    def _():
        def fin(c, carry):
            rows = pl.ds(pl.multiple_of(c * C, C), C)
            u = u_s[rows, :]
            y = x0_s[rows, :] * (yacc[rows, :] * (1.0 / seq_len) + u * bias_ref[...])
            o_ref[rows, :] = _rms(y, nw_ref[...])
            return carry

        lax.fori_loop(0, n_chunks, fin, 0)


def _hyena(p_hy, conv_w, conv_b, bias, nw, hre, him, tabs, bsz, seq_len):
    chi, clo, shi, slo, _, _ = tabs
    ft = min(HY_FTILE, seq_len)
    n_ft = seq_len // ft
    const = lambda a, b: pl.BlockSpec((a, b), lambda i, j: (0, 0))
    rowt = lambda w: pl.BlockSpec((ft, w), lambda i, j: (j, 0))
    colt = pl.BlockSpec((seq_len, ft), lambda i, j: (0, j))
    return pl.pallas_call(
        functools.partial(_hyena_kernel, seq_len // SCAN_CHUNK, n_ft),
        out_shape=jax.ShapeDtypeStruct((bsz * seq_len, W_GROUP), F32),
        grid=(bsz, n_ft),
        in_specs=[pl.BlockSpec((seq_len, 768), lambda i, j: (i, 0)), const(3, 768), const(1, 768),
                  const(1, W_GROUP), const(1, W_GROUP), rowt(W_GROUP), rowt(W_GROUP),
                  rowt(seq_len), rowt(seq_len), rowt(seq_len), rowt(seq_len), colt, colt, colt, colt],
        out_specs=pl.BlockSpec((seq_len, W_GROUP), lambda i, j: (i, 0)),
        scratch_shapes=[pltpu.VMEM((seq_len, W_GROUP), F32), pltpu.VMEM((seq_len, W_GROUP), F32),
                        pltpu.VMEM((2, seq_len, W_GROUP), BF16), pltpu.VMEM((seq_len, W_GROUP), F32)],
        compiler_params=_params(("arbitrary", "arbitrary")),
        name="hyena_longconv",
    )(p_hy, conv_w, conv_b, bias, nw, hre, him, chi, clo, shi, slo, chi, clo, shi, slo)


def _rope_tables(seq_len):
    rows = seq_len // GRID_W
    r = jnp.repeat(jnp.arange(rows), GRID_W).astype(F32)
    col = jnp.tile(jnp.arange(GRID_W), rows).astype(F32)
    quarter = D_HEAD // 4
    inv = ROPE_BASE ** (-jnp.arange(quarter, dtype=F32) / quarter)
    ang = jnp.concatenate([r[:, None] * inv, col[:, None] * inv], axis=-1)
    cos, sin = jnp.cos(ang), jnp.sin(ang)
    cos_full = jnp.tile(jnp.concatenate([cos, cos], axis=-1), (1, N_HEADS))
    sin_signed = jnp.tile(jnp.concatenate([-sin, sin], axis=-1), (1, N_HEADS))
    return cos_full, sin_signed


def _small_vec(first_lane, vals):
    flat = vals.reshape(-1)
    return jnp.zeros((1, SMALL_W), F32).at[0, first_lane:first_lane + flat.shape[0]].set(flat)


def _trunk(x, bsz, seq_len, row_fn, states, rope_tabs, hy_tabs, hy_feats, mod3, w):
    st_ret, st_gdn, st_ssd = states
    finals = []
    for l in range(DEPTH):
        p_ret, p_hy, p_gdn, p_ssd, p_small = _in_proj(x, mod3, w['norm1'], w['w_main'], w['w_small'], l, row_fn)
        o_ret, f_ret = _retention(p_ret, w['ret_la'][l], w['ret_norm'][l], bsz, seq_len, rope_tabs, st_ret, l)
        hre, him = _hyena_spectrum(hy_tabs, hy_feats, w['hy_freq'], w['hy_w1'], w['hy_b1'], w['hy_w2'],
                                   w['hy_b2'], w['hy_w3'], seq_len, l)
        o_hy = _hyena(p_hy, w['hy_conv_w'][l], w['hy_conv_b'][l], w['hy_bias'][l], w['hy_norm'][l],
                      hre, him, hy_tabs, bsz, seq_len)
        o_gdn, f_gdn = _gdn(p_gdn, p_small, w['gdn_conv_w'][l], w['gdn_dtb'][l], w['gdn_alog'][l],
                            w['gdn_norm'][l], bsz, seq_len, st_gdn, l)
        o_ssd, f_ssd = _ssd(p_ssd, p_small, w['ssd_conv_w'][l], w['ssd_conv_b'][l], w['ssd_dtb'][l],
                            w['ssd_alog'][l], w['ssd_dl'][l], w['ssd_norm'][l], bsz, seq_len, st_ssd, l)
        x = _out_proj((o_ret, o_hy, o_gdn, o_ssd), x, mod3, w['norm2'], w['w_out'], w['mlp_w1'], w['mlp_w2'],
                      w['final_norm'], l, row_fn, l == DEPTH - 1)
        finals.append((f_ret, f_gdn, f_ssd))
    return x, finals


def kernel(x_prompt, x_sample, state_ret, state_gdn, state_ssd, c, c_ctx, norm1_w, norm2_w, w_mod, b_mod, w_in, w_out, ret_log_decay, ret_norm_w, hy_conv_w, hy_conv_b, hy_freq, hy_w1, hy_b1, hy_w2, hy_b2, hy_w3, hy_bias, hy_norm_w, gdn_conv_w, gdn_A_log, gdn_dt_bias, gdn_norm_w, ssd_conv_w, ssd_conv_b, ssd_A_log, ssd_dt_bias, ssd_D, ssd_norm_w, mlp_w1, mlp_w2, final_norm_w):
    bsz, seq, _ = x_prompt.shape
    dbsz, dseq, _ = x_sample.shape

    cond8 = jnp.concatenate([c, c_ctx[None, :], jnp.zeros((MOD_ROWS - dbsz - 1, D_MODEL), F32)], axis=0)
    mod = _modulation(cond8, w_mod, b_mod)
    mod3 = mod.reshape(DEPTH * MOD_ROWS * 6, 1, D_MODEL)

    w_main = jnp.concatenate([w_in[:, :, 0:2816], w_in[:, :, 2832:3856]], axis=-1).astype(BF16)
    w_small = jnp.concatenate([w_in[:, :, 2816:2832], w_in[:, :, 3856:3864],
                               jnp.zeros((DEPTH, D_MODEL, SMALL_W - 24), F32)], axis=-1).astype(BF16)
    w = dict(
        norm1=norm1_w.reshape(DEPTH, 1, D_MODEL), norm2=norm2_w.reshape(DEPTH, 1, D_MODEL),
        w_main=w_main, w_small=w_small, w_out=w_out.astype(BF16),
        mlp_w1=mlp_w1.astype(BF16), mlp_w2=mlp_w2.astype(BF16), final_norm=final_norm_w,
        ret_la=[jnp.repeat(ret_log_decay[l], D_HEAD, axis=-1) for l in range(DEPTH)],
        ret_norm=[ret_norm_w[l][None, :] for l in range(DEPTH)],
        hy_freq=hy_freq.reshape(DEPTH, 1, HY_HIDDEN),
        hy_w1=jnp.pad(hy_w1, ((0, 0), (0, 128 - HY_EMB), (0, 0))),
        hy_b1=hy_b1.reshape(DEPTH, 1, HY_HIDDEN), hy_w2=hy_w2, hy_b2=hy_b2.reshape(DEPTH, 1, HY_HIDDEN),
        hy_w3=hy_w3,
        hy_conv_w=hy_conv_w, hy_conv_b=[hy_conv_b[l][None, :] for l in range(DEPTH)],
        hy_bias=[hy_bias[l][None, :] for l in range(DEPTH)], hy_norm=[hy_norm_w[l][None, :] for l in range(DEPTH)],
        gdn_conv_w=gdn_conv_w,
        gdn_dtb=[_small_vec(0, gdn_dt_bias[l]) for l in range(DEPTH)],
        gdn_alog=[_small_vec(0, gdn_A_log[l]) for l in range(DEPTH)],
        gdn_norm=[gdn_norm_w[l][None, :] for l in range(DEPTH)],
        ssd_conv_w=ssd_conv_w, ssd_conv_b=[ssd_conv_b[l][None, :] for l in range(DEPTH)],
        ssd_dtb=[_small_vec(16, ssd_dt_bias[l]) for l in range(DEPTH)],
        ssd_alog=[_small_vec(16, ssd_A_log[l]) for l in range(DEPTH)],
        ssd_dl=[jnp.repeat(ssd_D[l], D_HEAD)[None, :] for l in range(DEPTH)],
        ssd_norm=[ssd_norm_w[l][None, :] for l in range(DEPTH)],
    )

    y_p, fin = _trunk(x_prompt.reshape(bsz * seq, D_MODEL), bsz, seq, lambda i: CTX_MOD_ROW,
                      (None, None, None), None, _dft_tables(seq), _filter_features(seq), mod3, w)
    new_ret = jnp.stack([f[0] for f in fin], axis=1)
    new_gdn = jnp.stack([f[1] for f in fin], axis=1)
    new_ssd = jnp.stack([f[2] for f in fin], axis=1)

    tiles_per_seq = dseq // TOK_TILE
    y_s, _ = _trunk(x_sample.reshape(dbsz * dseq, D_MODEL), dbsz, dseq, lambda i: i // tiles_per_seq,
                    (state_ret, state_gdn, state_ssd), _rope_tables(dseq), _dft_tables(dseq),
                    _filter_features(dseq), mod3, w)

    return (y_p.reshape(bsz, seq, D_MODEL), y_s.reshape(dbsz, dseq, D_MODEL), new_ret, new_gdn, new_ssd)
```

```python
import functools
import math

import jax
import jax.numpy as jnp
from jax import lax
from jax.experimental import pallas as pl
from jax.experimental.pallas import tpu as pltpu

F32 = jnp.float32
BF16 = jnp.bfloat16

D_MODEL = 1024
DEPTH = 4
GRID_W = 64
W_GROUP = 256
N_HEADS = 4
D_HEAD = 64
N_SSD = 128
G_SSD = 2
HY_EMB = 33
HY_HIDDEN = 64
HY_DECAY_TARGET = 1e-2
HY_FAST_PCT = 0.3
HY_SLOW_PCT = 1.5
D_FF = 4 * D_MODEL
ROPE_BASE = 10000.0
EPS = 1e-6
IN_MAIN = 3840
SMALL_W = 128
MOD_ROWS = 8
CTX_MOD_ROW = 4

TOK_TILE = 256
SCAN_CHUNK = 256
GDN_CHUNK = 64
HY_FTILE = 256
NEG_BIG = -1e30
GDN_INV_PASSES = 1
GDN_SEQS = 2
VMEM_LIMIT = 56 * 1024 * 1024

NN = (((1,), (0,)), ((), ()))
NT = (((1,), (1,)), ((), ()))
TN = (((0,), (0,)), ((), ()))


def _dot(a, b, dims=NN):
    return lax.dot_general(a.astype(BF16), b.astype(BF16), dims, preferred_element_type=F32)


def _dot_hp(a, b, dims=NN):
    return lax.dot_general(a.astype(F32), b.astype(F32), dims, precision=lax.Precision.HIGHEST,
                           preferred_element_type=F32)


def _split(x):
    hi = x.astype(BF16)
    lo = (x - hi.astype(F32)).astype(BF16)
    return hi, lo


def _dot3(a_hi, a_lo, b_hi, b_lo, dims=NN):
    d = functools.partial(lax.dot_general, dimension_numbers=dims, preferred_element_type=F32)
    return d(a_hi, b_hi) + d(a_hi, b_lo) + d(a_lo, b_hi)


def _dot01(m, x, terms=3, m_left=True):
    mb = m.astype(BF16)
    acc, r = None, x.astype(F32)
    for _ in range(terms):
        t = r.astype(BF16)
        p = lax.dot_general(mb, t, NN, preferred_element_type=F32) if m_left else \
            lax.dot_general(t, mb, NN, preferred_element_type=F32)
        acc = p if acc is None else acc + p
        r = r - t.astype(F32)
    return acc


def _dot_inv(a, b):
    if GDN_INV_PASSES == 1:
        return _dot(a, b)
    a_hi, a_lo = _split(a)
    b_hi, b_lo = _split(b)
    return _dot3(a_hi, a_lo, b_hi, b_lo)


def _sigmoid(x):
    return 1.0 / (1.0 + jnp.exp(-x))


def _silu(x):
    return x * _sigmoid(x)


def _softplus(x):
    return jnp.maximum(x, 0.0) + jnp.log1p(jnp.exp(-jnp.abs(x)))


def _iota(shape, dim):
    return lax.broadcasted_iota(jnp.int32, shape, dim)


def _params(sem):
    return pltpu.CompilerParams(dimension_semantics=sem, vmem_limit_bytes=VMEM_LIMIT)


def _head_ones(n=W_GROUP, width=D_HEAD):
    return jnp.where(_iota((n, n), 0) // width == _iota((n, n), 1) // width, 1.0, 0.0).astype(F32)


def _mod_kernel(cond_ref, w_ref, b_ref, o_ref):
    s = _silu(cond_ref[...])
    o_ref[0] = _dot(s, w_ref[0]) + b_ref[0]


def _modulation(cond8, w_mod, b_mod):
    tn = 1536
    n_out = 6 * D_MODEL
    return pl.pallas_call(
        _mod_kernel,
        out_shape=jax.ShapeDtypeStruct((DEPTH, MOD_ROWS, n_out), F32),
        grid=(DEPTH, n_out // tn),
        in_specs=[
            pl.BlockSpec((MOD_ROWS, D_MODEL), lambda l, j: (0, 0)),
            pl.BlockSpec((1, D_MODEL, tn), lambda l, j: (l, 0, j)),
            pl.BlockSpec((1, 1, tn), lambda l, j: (l, 0, j)),
        ],
        out_specs=pl.BlockSpec((1, MOD_ROWS, tn), lambda l, j: (l, 0, j)),
        compiler_params=_params(("arbitrary", "arbitrary")),
        name="modulation",
    )(cond8, w_mod, b_mod.reshape(DEPTH, 1, n_out))


def _mod_spec(layer, k, row_fn):
    return pl.BlockSpec((1, 1, D_MODEL), lambda i: ((layer * MOD_ROWS + row_fn(i)) * 6 + k, 0, 0))


def _rms(x, w):
    return x * lax.rsqrt(jnp.mean(x * x, axis=-1, keepdims=True) + EPS) * w


def _in_kernel(x_ref, nw_ref, sh_ref, sc_ref, wm_ref, ws_ref, pret, phy, pgdn, pssd, psm):
    h = _rms(x_ref[...], nw_ref[0]) * (1.0 + sc_ref[0]) + sh_ref[0]
    hb = h.astype(BF16)
    d = functools.partial(jnp.dot, preferred_element_type=F32)
    pret[...] = d(hb, wm_ref[0, :, 0:1024])
    phy[...] = d(hb, wm_ref[0, :, 1024:1792])
    pgdn[...] = d(hb, wm_ref[0, :, 1792:2816])
    pssd[...] = d(hb, wm_ref[0, :, 2816:3840])
    psm[...] = d(hb, ws_ref[0])


def _in_proj(x, mod3, norm1_w3, w_main, w_small, layer, row_fn):
    n = x.shape[0]
    widths = (1024, 768, 1024, 1024, SMALL_W)
    return pl.pallas_call(
        _in_kernel,
        out_shape=tuple(jax.ShapeDtypeStruct((n, w), F32) for w in widths),
        grid=(n // TOK_TILE,),
        in_specs=[
            pl.BlockSpec((TOK_TILE, D_MODEL), lambda i: (i, 0)),
            pl.BlockSpec((1, 1, D_MODEL), lambda i: (layer, 0, 0)),
            _mod_spec(layer, 0, row_fn),
            _mod_spec(layer, 1, row_fn),
            pl.BlockSpec((1, D_MODEL, IN_MAIN), lambda i: (layer, 0, 0)),
            pl.BlockSpec((1, D_MODEL, SMALL_W), lambda i: (layer, 0, 0)),
        ],
        out_specs=tuple(pl.BlockSpec((TOK_TILE, w), lambda i: (i, 0)) for w in widths),
        compiler_params=_params(("arbitrary",)),
        name="in_proj",
    )(x, norm1_w3, mod3, mod3, w_main, w_small)


def _out_kernel(final, oret, ohy, ogdn, ossd, x_ref, g1, sh2, sc2, g2, nw2, wout, w1, w2, *rest):
    if final:
        fnw, y_ref = rest
    else:
        (y_ref,) = rest
    acc = _dot(oret[...], wout[0, 0:256, :])
    acc = acc + _dot(ohy[...], wout[0, 256:512, :])
    acc = acc + _dot(ogdn[...], wout[0, 512:768, :])
    acc = acc + _dot(ossd[...], wout[0, 768:1024, :])
    x1 = x_ref[...] + g1[0] * acc
    h2 = _rms(x1, nw2[0]) * (1.0 + sc2[0]) + sh2[0]
    a = _dot(h2, w1[0])
    a = jnp.square(jnp.maximum(a, 0.0))
    x2 = x1 + g2[0] * _dot(a, w2[0])
    if final:
        x2 = _rms(x2, fnw[...])
    y_ref[...] = x2


def _out_proj(mix, x, mod3, norm2_w3, w_out, w1, w2, final_norm_w, layer, row_fn, final):
    n = x.shape[0]
    tok = lambda w: pl.BlockSpec((TOK_TILE, w), lambda i: (i, 0))
    lay = lambda a, b: pl.BlockSpec((1, a, b), lambda i: (layer, 0, 0))
    in_specs = [tok(W_GROUP)] * 4 + [
        tok(D_MODEL),
        _mod_spec(layer, 2, row_fn), _mod_spec(layer, 3, row_fn),
        _mod_spec(layer, 4, row_fn), _mod_spec(layer, 5, row_fn),
        lay(1, D_MODEL), lay(D_MODEL, D_MODEL), lay(D_MODEL, D_FF), lay(D_FF, D_MODEL),
    ]
    args = list(mix) + [x, mod3, mod3, mod3, mod3, norm2_w3, w_out, w1, w2]
    if final:
        in_specs.append(pl.BlockSpec((1, D_MODEL), lambda i: (0, 0)))
        args.append(final_norm_w.reshape(1, D_MODEL))
    return pl.pallas_call(
        functools.partial(_out_kernel, final),
        out_shape=jax.ShapeDtypeStruct((n, D_MODEL), F32),
        grid=(n // TOK_TILE,),
        in_specs=in_specs,
        out_specs=tok(D_MODEL),
        compiler_params=_params(("arbitrary",)),
        name="out_proj_mlp",
    )(*args)


def _conv3_chunk(p_ref, r0, rows, seq_len, c0, width, w_ref, first, last):
    x = p_ref[pl.ds(r0, rows), c0:c0 + width]
    prev_i = jnp.maximum(r0 - 1, 0)
    next_i = jnp.minimum(r0 + rows, seq_len - 1)
    xp = p_ref[pl.ds(prev_i, 1), c0:c0 + width] * jnp.where(first, 0.0, 1.0)
    xn = p_ref[pl.ds(next_i, 1), c0:c0 + width] * jnp.where(last, 0.0, 1.0)
    row = _iota((rows, width), 0)
    x_prev = jnp.where(row == 0, xp, pltpu.roll(x, 1, 0))
    x_next = jnp.where(row == rows - 1, xn, pltpu.roll(x, rows - 1, 0))
    return x_prev * w_ref[0:1, :] + x * w_ref[1:2, :] + x_next * w_ref[2:3, :]


def _tri(n, lower):
    i, j = _iota((n, n), 0), _iota((n, n), 1)
    return jnp.where((j <= i) if lower else (j >= i), 1.0, 0.0).astype(F32)


def _lane_expand(first_row, width):
    r, c = _iota((SMALL_W, W_GROUP), 0), _iota((SMALL_W, W_GROUP), 1)
    return jnp.where(r == first_row + c // width, 1.0, 0.0).astype(F32)


def _ret_kernel(n_chunks, rope, has_s0, *refs):
    C = SCAN_CHUNK
    refs = list(refs)
    p_ref, la_ref, nw_ref = refs[:3]
    refs = refs[3:]
    if rope:
        cos_ref, sin_ref = refs[:2]
        refs = refs[2:]
    if has_s0:
        s0_ref = refs[0]
        refs = refs[1:]
    o_ref, st_ref, sf, sb, dmat, qs, ks, oscr = refs

    la_f = la_ref[0:1, :]
    la_b = la_ref[1:2, :]
    ri = _iota((C, W_GROUP), 0).astype(F32)
    e_f = jnp.exp((ri + 1.0) * la_f)
    e_b = jnp.exp((C - ri) * la_b)
    w_f = jnp.exp((C - 1.0 - ri) * la_f)
    w_b = jnp.exp(ri * la_b)
    dc_f = jnp.exp(C * la_f)
    dc_b = jnp.exp(C * la_b)
    bd = _head_ones()
    lane = _iota((1, W_GROUP), 1)
    ii, jj = _iota((C, C), 0), _iota((C, C), 1)
    diff = (ii - jj).astype(F32)
    for h in range(N_HEADS):
        laf_h = la_ref[0:1, D_HEAD * h:D_HEAD * h + 1]
        lab_h = la_ref[1:2, D_HEAD * h:D_HEAD * h + 1]
        dmat[h] = (jnp.exp(jnp.where(jj <= ii, diff * laf_h, NEG_BIG))
                   + jnp.exp(jnp.where(jj >= ii, -diff * lab_h, NEG_BIG)))

    sf[...] = jnp.zeros_like(sf)
    sb[...] = jnp.zeros_like(sb)
    if has_s0:
        for h in range(N_HEADS):
            sl = slice(D_HEAD * h, D_HEAD * (h + 1))
            sf[sl, sl] = s0_ref[0, 0, 0, h]
            sb[sl, sl] = s0_ref[0, 0, 1, h]

    def rot(x, cs, sn):
        halves = []
        for a in range(2):
            xh = x[:, 128 * a:128 * (a + 1)]
            lh = _iota(xh.shape, 1)
            halves.append(jnp.where(lh % D_HEAD < D_HEAD // 2, pltpu.roll(xh, 96, 1), pltpu.roll(xh, 32, 1)))
        return x * cs + jnp.concatenate(halves, axis=1) * sn

    def fwd(c, carry):
        r0 = pl.multiple_of(c * C, C)
        rows = pl.ds(r0, C)
        q = p_ref[rows, 0:256]
        k = p_ref[rows, 256:512] * (D_HEAD ** -0.5)
        v = p_ref[rows, 512:768]
        if rope:
            cs, sn = cos_ref[rows, :], sin_ref[rows, :]
            q, k = rot(q, cs, sn), rot(k, cs, sn)
        qs[rows, :] = q
        ks[rows, :] = k
        o = _dot(q * e_f, sf[...])
        for h in range(N_HEADS):
            mh = jnp.where(lane // D_HEAD == h, 1.0, 0.0)
            att = _dot(q * mh, k, NT) * dmat[h]
            o = o + _dot(att, v * mh)
        oscr[rows, :] = o
        sf[...] = sf[...] * dc_f + bd * _dot(k * w_f, v, TN)
        return carry

    lax.fori_loop(0, n_chunks, fwd, 0)

    def bwd(t, carry):
        c = n_chunks - 1 - t
        r0 = pl.multiple_of(c * C, C)
        rows = pl.ds(r0, C)
        q, k = qs[rows, :], ks[rows, :]
        v = p_ref[rows, 512:768]
        g = p_ref[rows, 768:1024]
        o = oscr[rows, :] + _dot(q * e_b, sb[...])
        sb[...] = sb[...] * dc_b + bd * _dot(k * w_b, v, TN)
        ms = _dot01(bd, o * o, m_left=False) * (1.0 / D_HEAD)
        o_ref[rows, :] = o * lax.rsqrt(ms + EPS) * nw_ref[...] * _silu(g)
        return carry

    lax.fori_loop(0, n_chunks, bwd, 0)

    for h in range(N_HEADS):
        sl = slice(D_HEAD * h, D_HEAD * (h + 1))
        st_ref[0, 0, h] = sf[sl, sl]
        st_ref[0, 1, h] = sb[sl, sl]


def _retention(p_ret, la_lane, nw, bsz, seq_len, rope_tabs, s0, layer):
    n_chunks = seq_len // SCAN_CHUNK
    rope = rope_tabs is not None
    has_s0 = s0 is not None
    const = lambda a, b: pl.BlockSpec((a, b), lambda i: (0, 0))
    in_specs = [pl.BlockSpec((seq_len, 1024), lambda i: (i, 0)), const(2, W_GROUP), const(1, W_GROUP)]
    args = [p_ret, la_lane, nw]
    if rope:
        in_specs += [const(seq_len, W_GROUP)] * 2
        args += list(rope_tabs)
    if has_s0:
        in_specs.append(pl.BlockSpec((1, 1, 2, N_HEADS, D_HEAD, D_HEAD), lambda i: (i, layer, 0, 0, 0, 0)))
        args.append(s0)
    return pl.pallas_call(
        functools.partial(_ret_kernel, n_chunks, rope, has_s0),
        out_shape=(jax.ShapeDtypeStruct((bsz * seq_len, W_GROUP), F32),
                   jax.ShapeDtypeStruct((bsz, 2, N_HEADS, D_HEAD, D_HEAD), F32)),
        grid=(bsz,),
        in_specs=in_specs,
        out_specs=(pl.BlockSpec((seq_len, W_GROUP), lambda i: (i, 0)),
                   pl.BlockSpec((1, 2, N_HEADS, D_HEAD, D_HEAD), lambda i: (i, 0, 0, 0, 0))),
        scratch_shapes=[pltpu.VMEM((W_GROUP, W_GROUP), F32), pltpu.VMEM((W_GROUP, W_GROUP), F32),
                        pltpu.VMEM((N_HEADS, SCAN_CHUNK, SCAN_CHUNK), F32),
                        pltpu.VMEM((seq_len, W_GROUP), F32), pltpu.VMEM((seq_len, W_GROUP), F32),
                        pltpu.VMEM((seq_len, W_GROUP), F32)],
        compiler_params=_params(("arbitrary",)),
        name="retention",
    )(*args)


def _ssd_kernel(n_chunks, has_s0, *refs):
    C = SCAN_CHUNK
    refs = list(refs)
    p_ref, ps_ref, cw_ref, cb_ref, dtb_ref, alog_ref, dl_ref, nw_ref = refs[:8]
    refs = refs[8:]
    if has_s0:
        s0_ref = refs[0]
        refs = refs[1:]
    o_ref, st_ref, sf, sb, xs_s, bm_s, cm_s, oscr = refs
    seq_len = n_chunks * C

    def prep(c, carry):
        r0 = pl.multiple_of(c * C, C)
        y = _conv3_chunk(p_ref, r0, C, seq_len, 0, 768, cw_ref, c == 0, c == n_chunks - 1) + cb_ref[...]
        y = _silu(y)
        rows = pl.ds(r0, C)
        xs_s[rows, :] = y[:, 0:256]
        bm_s[rows, :] = y[:, 256:512]
        cm_s[rows, :] = y[:, 512:768]
        return carry

    lax.fori_loop(0, n_chunks, prep, 0)

    sf[...] = jnp.zeros_like(sf)
    sb[...] = jnp.zeros_like(sb)
    if has_s0:
        for h in range(N_HEADS):
            g = h // 2
            sf[N_SSD * g:N_SSD * (g + 1), D_HEAD * h:D_HEAD * (h + 1)] = s0_ref[0, 0, 0, h]
            sb[N_SSD * g:N_SSD * (g + 1), D_HEAD * h:D_HEAD * (h + 1)] = s0_ref[0, 0, 1, h]

    ltri, utri = _tri(C, True), _tri(C, False)
    exp_f, exp_b = _lane_expand(16, D_HEAD), _lane_expand(20, D_HEAD)
    gmask = _head_ones(W_GROUP, N_SSD)
    lane_s = _iota((1, SMALL_W), 1)
    dt_mask = jnp.where((lane_s >= 16) & (lane_s < 24), 1.0, 0.0)
    lane = _iota((1, W_GROUP), 1)
    ii, jj = _iota((C, C), 0), _iota((C, C), 1)

    def gates(rows):
        dt = _softplus(ps_ref[rows, :] + dtb_ref[...]) * dt_mask
        la = dt * (-jnp.exp(alog_ref[...]))
        return dt, la

    def fwd(c, carry):
        r0 = pl.multiple_of(c * C, C)
        rows = pl.ds(r0, C)
        dt, la = gates(rows)
        cum_p = _dot01(ltri, la)
        cum_s = _dot01(utri, la)
        cum_pt, cum_st, dt_t = cum_p.T, cum_s.T, dt.T
        x_f = _dot01(exp_f, cum_p, m_left=False)
        dt_f = _dot01(exp_f, dt, m_left=False)
        xs, bm, cm = xs_s[rows, :], bm_s[rows, :], cm_s[rows, :]
        s_g = [_dot(cm[:, N_SSD * g:N_SSD * (g + 1)], bm[:, N_SSD * g:N_SSD * (g + 1)], NT) for g in range(G_SSD)]
        o = _dot(cm, sf[...]) * jnp.exp(x_f)
        for h in range(N_HEADS):
            f, b = 16 + h, 20 + h
            dec = (jnp.exp(jnp.where(jj <= ii, cum_p[:, f:f + 1] - cum_pt[f:f + 1, :], NEG_BIG)) * dt_t[f:f + 1, :]
                   + jnp.exp(jnp.where(jj >= ii, cum_s[:, b:b + 1] - cum_st[b:b + 1, :], NEG_BIG)) * dt_t[b:b + 1, :])
            mh = jnp.where(lane // D_HEAD == h, 1.0, 0.0)
            o = o + _dot(s_g[h // 2] * dec, xs * mh)
        oscr[rows, :] = o
        last = x_f[C - 1:C, :]
        sf[...] = sf[...] * jnp.exp(last) + gmask * _dot(bm, xs * dt_f * jnp.exp(last - x_f), TN)
        return carry

    lax.fori_loop(0, n_chunks, fwd, 0)

    def bwd(t, carry):
        c = n_chunks - 1 - t
        r0 = pl.multiple_of(c * C, C)
        rows = pl.ds(r0, C)
        dt, la = gates(rows)
        cum_s = _dot01(utri, la)
        x_b = _dot01(exp_b, cum_s, m_left=False)
        dt_b = _dot01(exp_b, dt, m_left=False)
        xs, bm, cm = xs_s[rows, :], bm_s[rows, :], cm_s[rows, :]
        o = oscr[rows, :] + _dot(cm, sb[...]) * jnp.exp(x_b)
        first = x_b[0:1, :]
        sb[...] = sb[...] * jnp.exp(first) + gmask * _dot(bm, xs * dt_b * jnp.exp(first - x_b), TN)
        y = (o + xs * dl_ref[...]) * _silu(p_ref[rows, 768:1024])
        halves = [y[:, 128 * g:128 * (g + 1)] for g in range(G_SSD)]
        halves = [yh * lax.rsqrt(jnp.mean(yh * yh, axis=-1, keepdims=True) + EPS) for yh in halves]
        o_ref[rows, :] = jnp.concatenate(halves, axis=1) * nw_ref[...]
        return carry

    lax.fori_loop(0, n_chunks, bwd, 0)

    for h in range(N_HEADS):
        g = h // 2
        st_ref[0, 0, h] = sf[N_SSD * g:N_SSD * (g + 1), D_HEAD * h:D_HEAD * (h + 1)]
        st_ref[0, 1, h] = sb[N_SSD * g:N_SSD * (g + 1), D_HEAD * h:D_HEAD * (h + 1)]


def _ssd(p_ssd, p_small, conv_w, conv_b, dtb, alog, d_lane, nw, bsz, seq_len, s0, layer):
    n_chunks = seq_len // SCAN_CHUNK
    has_s0 = s0 is not None
    const = lambda a, b: pl.BlockSpec((a, b), lambda i: (0, 0))
    in_specs = [pl.BlockSpec((seq_len, 1024), lambda i: (i, 0)), pl.BlockSpec((seq_len, SMALL_W), lambda i: (i, 0)),
                const(3, 768), const(1, 768), const(1, SMALL_W), const(1, SMALL_W), const(1, W_GROUP),
                const(1, W_GROUP)]
    args = [p_ssd, p_small, conv_w, conv_b, dtb, alog, d_lane, nw]
    if has_s0:
        in_specs.append(pl.BlockSpec((1, 1, 2, N_HEADS, N_SSD, D_HEAD), lambda i: (i, layer, 0, 0, 0, 0)))
        args.append(s0)
    return pl.pallas_call(
        functools.partial(_ssd_kernel, n_chunks, has_s0),
        out_shape=(jax.ShapeDtypeStruct((bsz * seq_len, W_GROUP), F32),
                   jax.ShapeDtypeStruct((bsz, 2, N_HEADS, N_SSD, D_HEAD), F32)),
        grid=(bsz,),
        in_specs=in_specs,
        out_specs=(pl.BlockSpec((seq_len, W_GROUP), lambda i: (i, 0)),
                   pl.BlockSpec((1, 2, N_HEADS, N_SSD, D_HEAD), lambda i: (i, 0, 0, 0, 0))),
        scratch_shapes=[pltpu.VMEM((W_GROUP, W_GROUP), F32), pltpu.VMEM((W_GROUP, W_GROUP), F32)]
        + [pltpu.VMEM((seq_len, W_GROUP), F32)] * 4,
        compiler_params=_params(("arbitrary",)),
        name="ssd",
    )(*args)


def _gdn_kernel(n_pre, n_seq, has_s0, *refs):
    C = GDN_CHUNK
    CP = SCAN_CHUNK
    R = N_HEADS * C
    refs = list(refs)
    p_ref, ps_ref, cw_ref, dtb_ref, alog_ref, nw_ref = refs[:6]
    refs = refs[6:]
    if has_s0:
        s0_ref = refs[0]
        refs = refs[1:]
    o_ref, st_ref, sf, sb, q_s, k_s, v_s, oscr_b = refs
    seq_len = n_pre * CP
    n_chunks = seq_len // C
    bd = _head_ones()

    def prep(c, carry):
        r0 = pl.multiple_of(c * CP, CP)
        cs = lax.rem(c, n_pre)
        y = _silu(_conv3_chunk(p_ref, r0, CP, n_seq * seq_len, 0, 768, cw_ref, cs == 0, cs == n_pre - 1))
        q, k, v = y[:, 0:256], y[:, 256:512], y[:, 512:768]
        q = q * lax.rsqrt(_dot01(bd, q * q, m_left=False) + EPS) * (D_HEAD ** -0.5)
        k = k * lax.rsqrt(_dot01(bd, k * k, m_left=False) + EPS)
        rows = pl.ds(r0, CP)
        q_s[rows, :] = q
        k_s[rows, :] = k
        v_s[rows, :] = v
        return carry

    lax.fori_loop(0, n_seq * n_pre, prep, 0)

    sf[...] = jnp.zeros_like(sf)
    sb[...] = jnp.zeros_like(sb)
    if has_s0:
        for j in range(n_seq):
            for h in range(N_HEADS):
                sl = slice(D_HEAD * h, D_HEAD * (h + 1))
                sf[j, sl, sl] = s0_ref[j, 0, 0, h]
                sb[j, sl, sl] = s0_ref[j, 0, 1, h]

    lane_s = _iota((1, SMALL_W), 1)
    la_mask = jnp.where(lane_s < 8, 1.0, 0.0)
    rr, cc = _iota((R, R), 0), _iota((R, R), 1)
    same = (rr // C) == (cc // C)
    head_lanes = jnp.where(_iota((R, W_GROUP), 0) // C == _iota((R, W_GROUP), 1) // D_HEAD, 1.0, 0.0)
    lane_head = _iota((1, W_GROUP), 1) // D_HEAD
    pad_rows = jnp.zeros((SMALL_W - C, SMALL_W), F32)

    def stack(x):
        return jnp.concatenate([x] * N_HEADS, axis=0)

    def per_row(x, lane0):
        return jnp.concatenate([jnp.broadcast_to(x[:, lane0 + h:lane0 + h + 1], (C, W_GROUP))
                                for h in range(N_HEADS)], axis=0)

    def each(f, *lists):
        return [f(*a) for a in zip(*lists)]

    def chunks(streams):
        seqs, cs, dirs, s_refs, o_refs = zip(*streams)
        fwd = [d == 0 for d in dirs]
        base = [4 * d for d in dirs]
        rows = [pl.ds(pl.multiple_of(j * seq_len + c * C, C), C) for j, c in zip(seqs, cs)]
        g = [ps_ref[r, :] for r in rows]
        la = [-jnp.exp(alog_ref[...]) * _softplus(x + dtb_ref[...]) * la_mask for x in g]
        beta = each(_sigmoid, g)
        cum = [_dot01(_tri(C, f), x) for f, x in zip(fwd, la)]
        last_row = [x[C - 1:C, :] if f else x[0:1, :] for f, x in zip(fwd, cum)]
        cum_t = [jnp.concatenate([x, pad_rows], axis=0).T for x in cum]
        cum_row = []
        for x, b in zip(cum_t, base):
            pc = [x[b + h:b + h + 1, :] for h in range(N_HEADS)]
            cum_row.append(jnp.concatenate([pc[0] + pltpu.roll(pc[1], C, 1), pc[2] + pltpu.roll(pc[3], C, 1)],
                                           axis=1))
        cum_cb = each(per_row, cum, base)
        beta_cb = [per_row(x, 8 + b) for x, b in zip(beta, base)]
        ecum_cb = [per_row(jnp.exp(x), b) for x, b in zip(cum, base)]
        kdec_cb = [per_row(jnp.exp(l - x), b) for l, x, b in zip(last_row, cum, base)]
        last_lane = []
        for l, b in zip(last_row, base):
            ll = jnp.zeros((1, W_GROUP), F32)
            for h in range(N_HEADS):
                ll = jnp.where(lane_head == h, l[:, b + h:b + h + 1], ll)
            last_lane.append(ll)
        incl = [same & ((cc <= rr) if f else (cc >= rr)) for f in fwd]
        strict = [same & ((cc < rr) if f else (cc > rr)) for f in fwd]
        dec = [jnp.exp(jnp.where(m, x - y, NEG_BIG)) for m, x, y in zip(incl, cum_cb, cum_row)]
        qst = [stack(q_s[r, :]) * head_lanes for r in rows]
        kst = [stack(k_s[r, :]) * head_lanes for r in rows]
        vst = [stack(v_s[r, :]) * head_lanes for r in rows]
        kk = [_dot(x, x, NT) for x in kst]
        qk = [_dot(x, y, NT) for x, y in zip(qst, kst)]
        a = [jnp.where(m, d * x, 0.0) * b for m, d, x, b in zip(strict, dec, kk, beta_cb)]
        n = [-jnp.where((rr // 2) == (cc // 2), x, 0.0) for x in a]
        s = 2
        while s < C:
            blk = ((rr // (2 * s)) == (cc // (2 * s))) & ((rr // s) != (cc // s))
            lo = [jnp.where(blk, x, 0.0) for x in a]
            y = [x + _dot_inv(x, m) for x, m in zip(lo, n)]
            n = [m - (x + _dot_inv(m, x)) for m, x in zip(n, y)]
            s *= 2
        kp = [k * (b * e) for k, b, e in zip(kst, beta_cb, ecum_cb)]
        vp = [v * b for v, b in zip(vst, beta_cb)]
        wm = [x + _dot_inv(m, x) for m, x in zip(n, kp)]
        um = [x + _dot_inv(m, x) for m, x in zip(n, vp)]
        st = [r[j] for r, j in zip(s_refs, seqs)]
        v_new = [u - _dot(w_, s_) for u, w_, s_ in zip(um, wm, st)]
        ost = [_dot(q * e, s_) + _dot(x * d, v) for q, e, s_, x, d, v in zip(qst, ecum_cb, st, qk, dec, v_new)]
        for r, o_r, x in zip(rows, o_refs, ost):
            o_r[r, :] = x[0:C] + x[C:2 * C] + x[2 * C:3 * C] + x[3 * C:4 * C]
        for s_r, j, s_, l, k, kd, v in zip(s_refs, seqs, st, last_lane, kst, kdec_cb, v_new):
            s_r[j] = s_ * jnp.exp(l) + _dot(k * kd, v, TN)

    def scan_body(t, carry):
        streams = []
        for j in range(n_seq):
            streams += [(j, t, 0, sf, o_ref), (j, n_chunks - 1 - t, 1, sb, oscr_b)]
        chunks(streams)
        return carry

    lax.fori_loop(0, n_chunks, scan_body, 0)

    def finish(c, carry):
        rows = pl.ds(pl.multiple_of(c * CP, CP), CP)
        o = o_ref[rows, :] + oscr_b[rows, :]
        ms = _dot01(bd, o * o, m_left=False) * (1.0 / D_HEAD)
        o_ref[rows, :] = o * lax.rsqrt(ms + EPS) * nw_ref[...] * _silu(p_ref[rows, 768:1024])
        return carry

    lax.fori_loop(0, n_seq * n_pre, finish, 0)

    for j in range(n_seq):
        for h in range(N_HEADS):
            sl = slice(D_HEAD * h, D_HEAD * (h + 1))
            st_ref[j, 0, h] = sf[j, sl, sl]
            st_ref[j, 1, h] = sb[j, sl, sl]


def _gdn(p_gdn, p_small, conv_w, dtb, alog, nw, bsz, seq_len, s0, layer):
    n_pre = seq_len // SCAN_CHUNK
    n_seq = GDN_SEQS
    rows = n_seq * seq_len
    has_s0 = s0 is not None
    big = dict(pipeline_mode=pl.Buffered(1)) if rows * 1024 * 4 > (8 << 20) else {}
    const = lambda a, b: pl.BlockSpec((a, b), lambda i: (0, 0))
    in_specs = [pl.BlockSpec((rows, 1024), lambda i: (i, 0), **big),
                pl.BlockSpec((rows, SMALL_W), lambda i: (i, 0), **big),
                const(3, 768), const(1, SMALL_W), const(1, SMALL_W), const(1, W_GROUP)]
    args = [p_gdn, p_small, conv_w, dtb, alog, nw]
    if has_s0:
        in_specs.append(pl.BlockSpec((n_seq, 1, 2, N_HEADS, D_HEAD, D_HEAD), lambda i: (i, layer, 0, 0, 0, 0)))
        args.append(s0)
    return pl.pallas_call(
        functools.partial(_gdn_kernel, n_pre, n_seq, has_s0),
        out_shape=(jax.ShapeDtypeStruct((bsz * seq_len, W_GROUP), F32),
                   jax.ShapeDtypeStruct((bsz, 2, N_HEADS, D_HEAD, D_HEAD), F32)),
        grid=(bsz // n_seq,),
        in_specs=in_specs,
        out_specs=(pl.BlockSpec((rows, W_GROUP), lambda i: (i, 0)),
                   pl.BlockSpec((n_seq, 2, N_HEADS, D_HEAD, D_HEAD), lambda i: (i, 0, 0, 0, 0))),
        scratch_shapes=[pltpu.VMEM((n_seq, W_GROUP, W_GROUP), F32), pltpu.VMEM((n_seq, W_GROUP, W_GROUP), F32)]
        + [pltpu.VMEM((rows, W_GROUP), F32)] * 4,
        compiler_params=_params(("arbitrary",)),
        name="gated_delta",
    )(*args)


def _dft_tables(seq_len):
    m = 2 * seq_len
    f = jnp.arange(seq_len, dtype=jnp.int32)
    k = ((2 * f[:, None] + 1) * (2 * f[None, :] + 1)) % (4 * m)
    ang = k.astype(F32) * (math.pi / (2 * m))
    c2, s2 = jnp.cos(ang), jnp.sin(ang)
    half = (2 * f + 1).astype(F32) * (math.pi / (2 * m))
    return _split(c2) + _split(s2) + (jnp.cos(half)[:, None], jnp.sin(half)[:, None])


def _filter_features(seq_len):
    t = jnp.linspace(0.0, 1.0, seq_len, dtype=F32)[:, None]
    bands = (HY_EMB - 1) // 2
    w = 2.0 * math.pi * jnp.arange(seq_len, dtype=F32)[:, None] / seq_len
    f = jnp.linspace(1e-4, bands - 1, bands, dtype=F32)[None, :]
    z = jnp.concatenate([t, jnp.cos(f * w), -jnp.sin(f * w)], axis=-1)
    z = jnp.pad(z, ((0, 0), (0, 128 - HY_EMB)))
    deltas = jnp.abs(jnp.linspace(math.log(HY_DECAY_TARGET) / HY_SLOW_PCT,
                                  math.log(HY_DECAY_TARGET) / HY_FAST_PCT, W_GROUP, dtype=F32))
    return z, jnp.exp(-t * deltas)


def _filter_kernel(z_ref, dec_ref, fr_ref, w1_ref, b1_ref, w2_ref, b2_ref, w3_ref,
                   chi, clo, shi, slo, cw_ref, sw_ref, hre_ref, him_ref, ps, pd):
    @pl.when(pl.program_id(0) == 0)
    def _():
        fr = fr_ref[0]
        h = jnp.sin(fr * (_dot_hp(z_ref[...], w1_ref[0]) + b1_ref[0]))
        h = jnp.sin(fr * (_dot_hp(h, w2_ref[0]) + b2_ref[0]))
        taps = _dot_hp(h, w3_ref[0])
        hf = taps[:, 0:W_GROUP] * dec_ref[...]
        hb = taps[:, W_GROUP:2 * W_GROUP] * dec_ref[...]
        hb = jnp.where(_iota(hb.shape, 0) == 0, 0.0, hb)
        s_hi, s_lo = _split(hf + hb)
        d_hi, d_lo = _split(hf - hb)
        ps[0], ps[1] = s_hi, s_lo
        pd[0], pd[1] = d_hi, d_lo

    c_sum = _dot3(chi[...], clo[...], ps[0], ps[1])
    s_sum = _dot3(shi[...], slo[...], ps[0], ps[1])
    c_dif = _dot3(chi[...], clo[...], pd[0], pd[1])
    s_dif = _dot3(shi[...], slo[...], pd[0], pd[1])
    cw, sw = cw_ref[...], sw_ref[...]
    hre_ref[...] = cw * c_sum + sw * s_sum
    him_ref[...] = sw * c_dif - cw * s_dif


def _hyena_spectrum(tabs, feats, freq, w1, b1, w2, b2, w3, seq_len, layer):
    chi, clo, shi, slo, cwv, swv = tabs
    z, dec = feats
    ft = min(HY_FTILE, seq_len)
    const = lambda a, b: pl.BlockSpec((a, b), lambda j: (0, 0))
    lay = lambda a, b: pl.BlockSpec((1, a, b), lambda j: (layer, 0, 0))
    rowt = lambda w: pl.BlockSpec((ft, w), lambda j: (j, 0))
    return pl.pallas_call(
        _filter_kernel,
        out_shape=(jax.ShapeDtypeStruct((seq_len, W_GROUP), F32),) * 2,
        grid=(seq_len // ft,),
        in_specs=[const(seq_len, 128), const(seq_len, W_GROUP), lay(1, HY_HIDDEN), lay(128, HY_HIDDEN),
                  lay(1, HY_HIDDEN), lay(HY_HIDDEN, HY_HIDDEN), lay(1, HY_HIDDEN), lay(HY_HIDDEN, 2 * W_GROUP),
                  rowt(seq_len), rowt(seq_len), rowt(seq_len), rowt(seq_len), rowt(1), rowt(1)],
        out_specs=(rowt(W_GROUP), rowt(W_GROUP)),
        scratch_shapes=[pltpu.VMEM((2, seq_len, W_GROUP), BF16), pltpu.VMEM((2, seq_len, W_GROUP), BF16)],
        compiler_params=_params(("arbitrary",)),
        name="hyena_filter",
    )(z, dec, freq, w1, b1, w2, b2, w3, chi, clo, shi, slo, cwv, swv)


def _hyena_kernel(n_chunks, n_ft, p_ref, cw_ref, cb_ref, bias_ref, nw_ref, hre_ref, him_ref,
                  rchi, rclo, rshi, rslo, cchi, cclo, cshi, cslo, o_ref, x0_s, u_s, us_s, yacc):
    C = SCAN_CHUNK
    seq_len = n_chunks * C
    j = pl.program_id(1)

    @pl.when(j == 0)
    def _():
        def prep(c, carry):
            r0 = pl.multiple_of(c * C, C)
            y = _conv3_chunk(p_ref, r0, C, seq_len, 0, 768, cw_ref, c == 0, c == n_chunks - 1) + cb_ref[...]
            rows = pl.ds(r0, C)
            u = y[:, 512:768] * y[:, 256:512]
            x0_s[rows, :] = y[:, 0:256]
            u_s[rows, :] = u
            u_hi, u_lo = _split(u)
            us_s[0, rows, :] = u_hi
            us_s[1, rows, :] = u_lo
            return carry

        lax.fori_loop(0, n_chunks, prep, 0)
        yacc[...] = jnp.zeros_like(yacc)

    u_re = _dot3(rchi[...], rclo[...], us_s[0], us_s[1])
    u_im = -_dot3(rshi[...], rslo[...], us_s[0], us_s[1])
    hre, him = hre_ref[...], him_ref[...]
    yre_hi, yre_lo = _split(u_re * hre - u_im * him)
    yim_hi, yim_lo = _split(u_re * him + u_im * hre)
    yacc[...] += (_dot3(cchi[...], cclo[...], yre_hi, yre_lo)
                  - _dot3(cshi[...], cslo[...], yim_hi, yim_lo))

    @pl.when(j == n_ft - 1)
    def _():
        def fin(c, carry):
            rows = pl.ds(pl.multiple_of(c * C, C), C)
            u = u_s[rows, :]
            y = x0_s[rows, :] * (yacc[rows, :] * (1.0 / seq_len) + u * bias_ref[...])
            o_ref[rows, :] = _rms(y, nw_ref[...])
            return carry

        lax.fori_loop(0, n_chunks, fin, 0)


def _hyena(p_hy, conv_w, conv_b, bias, nw, hre, him, tabs, bsz, seq_len):
    chi, clo, shi, slo, _, _ = tabs
    ft = min(HY_FTILE, seq_len)
    n_ft = seq_len // ft
    const = lambda a, b: pl.BlockSpec((a, b), lambda i, j: (0, 0))
    rowt = lambda w: pl.BlockSpec((ft, w), lambda i, j: (j, 0))
    colt = pl.BlockSpec((seq_len, ft), lambda i, j: (0, j))
    return pl.pallas_call(
        functools.partial(_hyena_kernel, seq_len // SCAN_CHUNK, n_ft),
        out_shape=jax.ShapeDtypeStruct((bsz * seq_len, W_GROUP), F32),
        grid=(bsz, n_ft),
        in_specs=[pl.BlockSpec((seq_len, 768), lambda i, j: (i, 0)), const(3, 768), const(1, 768),
                  const(1, W_GROUP), const(1, W_GROUP), rowt(W_GROUP), rowt(W_GROUP),
                  rowt(seq_len), rowt(seq_len), rowt(seq_len), rowt(seq_len), colt, colt, colt, colt],
        out_specs=pl.BlockSpec((seq_len, W_GROUP), lambda i, j: (i, 0)),
        scratch_shapes=[pltpu.VMEM((seq_len, W_GROUP), F32), pltpu.VMEM((seq_len, W_GROUP), F32),
                        pltpu.VMEM((2, seq_len, W_GROUP), BF16), pltpu.VMEM((seq_len, W_GROUP), F32)],
        compiler_params=_params(("arbitrary", "arbitrary")),
        name="hyena_longconv",
    )(p_hy, conv_w, conv_b, bias, nw, hre, him, chi, clo, shi, slo, chi, clo, shi, slo)


def _rope_tables(seq_len):
    rows = seq_len // GRID_W
    r = jnp.repeat(jnp.arange(rows), GRID_W).astype(F32)
    col = jnp.tile(jnp.arange(GRID_W), rows).astype(F32)
    quarter = D_HEAD // 4
    inv = ROPE_BASE ** (-jnp.arange(quarter, dtype=F32) / quarter)
    ang = jnp.concatenate([r[:, None] * inv, col[:, None] * inv], axis=-1)
    cos, sin = jnp.cos(ang), jnp.sin(ang)
    cos_full = jnp.tile(jnp.concatenate([cos, cos], axis=-1), (1, N_HEADS))
    sin_signed = jnp.tile(jnp.concatenate([-sin, sin], axis=-1), (1, N_HEADS))
    return cos_full, sin_signed


def _small_vec(first_lane, vals):
    flat = vals.reshape(-1)
    return jnp.zeros((1, SMALL_W), F32).at[0, first_lane:first_lane + flat.shape[0]].set(flat)


def _trunk(x, bsz, seq_len, row_fn, states, rope_tabs, hy_tabs, hy_feats, mod3, w):
    st_ret, st_gdn, st_ssd = states
    finals = []
    for l in range(DEPTH):
        p_ret, p_hy, p_gdn, p_ssd, p_small = _in_proj(x, mod3, w['norm1'], w['w_main'], w['w_small'], l, row_fn)
        o_ret, f_ret = _retention(p_ret, w['ret_la'][l], w['ret_norm'][l], bsz, seq_len, rope_tabs, st_ret, l)
        hre, him = _hyena_spectrum(hy_tabs, hy_feats, w['hy_freq'], w['hy_w1'], w['hy_b1'], w['hy_w2'],
                                   w['hy_b2'], w['hy_w3'], seq_len, l)
        o_hy = _hyena(p_hy, w['hy_conv_w'][l], w['hy_conv_b'][l], w['hy_bias'][l], w['hy_norm'][l],
                      hre, him, hy_tabs, bsz, seq_len)
        o_gdn, f_gdn = _gdn(p_gdn, p_small, w['gdn_conv_w'][l], w['gdn_dtb'][l], w['gdn_alog'][l],
                            w['gdn_norm'][l], bsz, seq_len, st_gdn, l)
        o_ssd, f_ssd = _ssd(p_ssd, p_small, w['ssd_conv_w'][l], w['ssd_conv_b'][l], w['ssd_dtb'][l],
                            w['ssd_alog'][l], w['ssd_dl'][l], w['ssd_norm'][l], bsz, seq_len, st_ssd, l)
        x = _out_proj((o_ret, o_hy, o_gdn, o_ssd), x, mod3, w['norm2'], w['w_out'], w['mlp_w1'], w['mlp_w2'],
                      w['final_norm'], l, row_fn, l == DEPTH - 1)
        finals.append((f_ret, f_gdn, f_ssd))
    return x, finals


def kernel(x_prompt, x_sample, state_ret, state_gdn, state_ssd, c, c_ctx, norm1_w, norm2_w, w_mod, b_mod, w_in, w_out, ret_log_decay, ret_norm_w, hy_conv_w, hy_conv_b, hy_freq, hy_w1, hy_b1, hy_w2, hy_b2, hy_w3, hy_bias, hy_norm_w, gdn_conv_w, gdn_A_log, gdn_dt_bias, gdn_norm_w, ssd_conv_w, ssd_conv_b, ssd_A_log, ssd_dt_bias, ssd_D, ssd_norm_w, mlp_w1, mlp_w2, final_norm_w):
    bsz, seq, _ = x_prompt.shape
    dbsz, dseq, _ = x_sample.shape

    cond8 = jnp.concatenate([c, c_ctx[None, :], jnp.zeros((MOD_ROWS - dbsz - 1, D_MODEL), F32)], axis=0)
    mod = _modulation(cond8, w_mod, b_mod)
    mod3 = mod.reshape(DEPTH * MOD_ROWS * 6, 1, D_MODEL)

    w_main = jnp.concatenate([w_in[:, :, 0:2816], w_in[:, :, 2832:3856]], axis=-1).astype(BF16)
    w_small = jnp.concatenate([w_in[:, :, 2816:2832], w_in[:, :, 3856:3864],
                               jnp.zeros((DEPTH, D_MODEL, SMALL_W - 24), F32)], axis=-1).astype(BF16)
    w = dict(
        norm1=norm1_w.reshape(DEPTH, 1, D_MODEL), norm2=norm2_w.reshape(DEPTH, 1, D_MODEL),
        w_main=w_main, w_small=w_small, w_out=w_out.astype(BF16),
        mlp_w1=mlp_w1.astype(BF16), mlp_w2=mlp_w2.astype(BF16), final_norm=final_norm_w,
        ret_la=[jnp.repeat(ret_log_decay[l], D_HEAD, axis=-1) for l in range(DEPTH)],
        ret_norm=[ret_norm_w[l][None, :] for l in range(DEPTH)],
        hy_freq=hy_freq.reshape(DEPTH, 1, HY_HIDDEN),
        hy_w1=jnp.pad(hy_w1, ((0, 0), (0, 128 - HY_EMB), (0, 0))),
        hy_b1=hy_b1.reshape(DEPTH, 1, HY_HIDDEN), hy_w2=hy_w2, hy_b2=hy_b2.reshape(DEPTH, 1, HY_HIDDEN),
        hy_w3=hy_w3,
        hy_conv_w=hy_conv_w, hy_conv_b=[hy_conv_b[l][None, :] for l in range(DEPTH)],
        hy_bias=[hy_bias[l][None, :] for l in range(DEPTH)], hy_norm=[hy_norm_w[l][None, :] for l in range(DEPTH)],
        gdn_conv_w=gdn_conv_w,
        gdn_dtb=[_small_vec(0, gdn_dt_bias[l]) for l in range(DEPTH)],
        gdn_alog=[_small_vec(0, gdn_A_log[l]) for l in range(DEPTH)],
        gdn_norm=[gdn_norm_w[l][None, :] for l in range(DEPTH)],
        ssd_conv_w=ssd_conv_w, ssd_conv_b=[ssd_conv_b[l][None, :] for l in range(DEPTH)],
        ssd_dtb=[_small_vec(16, ssd_dt_bias[l]) for l in range(DEPTH)],
        ssd_alog=[_small_vec(16, ssd_A_log[l]) for l in range(DEPTH)],
        ssd_dl=[jnp.repeat(ssd_D[l], D_HEAD)[None, :] for l in range(DEPTH)],
        ssd_norm=[ssd_norm_w[l][None, :] for l in range(DEPTH)],
    )

    y_p, fin = _trunk(x_prompt.reshape(bsz * seq, D_MODEL), bsz, seq, lambda i: CTX_MOD_ROW,
                      (None, None, None), None, _dft_tables(seq), _filter_features(seq), mod3, w)
    new_ret = jnp.stack([f[0] for f in fin], axis=1)
    new_gdn = jnp.stack([f[1] for f in fin], axis=1)
    new_ssd = jnp.stack([f[2] for f in fin], axis=1)

    tiles_per_seq = dseq // TOK_TILE
    y_s, _ = _trunk(x_sample.reshape(dbsz * dseq, D_MODEL), dbsz, dseq, lambda i: i // tiles_per_seq,
                    (state_ret, state_gdn, state_ssd), _rope_tables(dseq), _dft_tables(dseq),
                    _filter_features(dseq), mod3, w)

    return (y_p.reshape(bsz, seq, D_MODEL), y_s.reshape(dbsz, dseq, D_MODEL), new_ret, new_gdn, new_ssd)
```

```python
import functools
import math

import jax
import jax.numpy as jnp
from jax import lax
from jax.experimental import pallas as pl
from jax.experimental.pallas import tpu as pltpu

F32 = jnp.float32
BF16 = jnp.bfloat16

D_MODEL = 1024
DEPTH = 4
GRID_W = 64
W_GROUP = 256
N_HEADS = 4
D_HEAD = 64
N_SSD = 128
G_SSD = 2
HY_EMB = 33
HY_HIDDEN = 64
HY_DECAY_TARGET = 1e-2
HY_FAST_PCT = 0.3
HY_SLOW_PCT = 1.5
D_FF = 4 * D_MODEL
ROPE_BASE = 10000.0
EPS = 1e-6
IN_MAIN = 3840
SMALL_W = 128
MOD_ROWS = 8
CTX_MOD_ROW = 4

TOK_TILE = 256
SCAN_CHUNK = 256
GDN_CHUNK = 64
HY_FTILE = 256
NEG_BIG = -1e30
GDN_INV_PASSES = 1
GDN_SEQS = 2
VMEM_LIMIT = 56 * 1024 * 1024

NN = (((1,), (0,)), ((), ()))
NT = (((1,), (1,)), ((), ()))
TN = (((0,), (0,)), ((), ()))


def _dot(a, b, dims=NN):
    return lax.dot_general(a.astype(BF16), b.astype(BF16), dims, preferred_element_type=F32)


def _dot_hp(a, b, dims=NN):
    a_hi, a_lo = _split(a)
    b_hi, b_lo = _split(b)
    return _dot3(a_hi, a_lo, b_hi, b_lo, dims)


def _split(x):
    hi = x.astype(BF16)
    lo = (x - hi.astype(F32)).astype(BF16)
    return hi, lo


def _dot3(a_hi, a_lo, b_hi, b_lo, dims=NN):
    d = functools.partial(lax.dot_general, dimension_numbers=dims, preferred_element_type=F32)
    return d(a_hi, b_hi) + d(a_hi, b_lo) + d(a_lo, b_hi)


def _dot01(m, x, terms=3, m_left=True):
    mb = m.astype(BF16)
    acc, r = None, x.astype(F32)
    for _ in range(terms):
        t = r.astype(BF16)
        p = lax.dot_general(mb, t, NN, preferred_element_type=F32) if m_left else \
            lax.dot_general(t, mb, NN, preferred_element_type=F32)
        acc = p if acc is None else acc + p
        r = r - t.astype(F32)
    return acc


def _dot_inv(a, b):
    if GDN_INV_PASSES == 1:
        return _dot(a, b)
    a_hi, a_lo = _split(a)
    b_hi, b_lo = _split(b)
    return _dot3(a_hi, a_lo, b_hi, b_lo)


def _sigmoid(x):
    return 1.0 / (1.0 + jnp.exp(-x))


def _silu(x):
    return x * _sigmoid(x)


def _softplus(x):
    return jnp.maximum(x, 0.0) + jnp.log1p(jnp.exp(-jnp.abs(x)))


def _iota(shape, dim):
    return lax.broadcasted_iota(jnp.int32, shape, dim)


def _params(sem):
    return pltpu.CompilerParams(dimension_semantics=sem, vmem_limit_bytes=VMEM_LIMIT)


def _head_ones(n=W_GROUP, width=D_HEAD):
    return jnp.where(_iota((n, n), 0) // width == _iota((n, n), 1) // width, 1.0, 0.0).astype(F32)


def _mod_kernel(cond_ref, w_ref, b_ref, o_ref):
    s = _silu(cond_ref[...])
    o_ref[0] = _dot(s, w_ref[0]) + b_ref[0]


def _modulation(cond8, w_mod, b_mod):
    tn = 1536
    n_out = 6 * D_MODEL
    return pl.pallas_call(
        _mod_kernel,
        out_shape=jax.ShapeDtypeStruct((DEPTH, MOD_ROWS, n_out), F32),
        grid=(DEPTH, n_out // tn),
        in_specs=[
            pl.BlockSpec((MOD_ROWS, D_MODEL), lambda l, j: (0, 0)),
            pl.BlockSpec((1, D_MODEL, tn), lambda l, j: (l, 0, j)),
            pl.BlockSpec((1, 1, tn), lambda l, j: (l, 0, j)),
        ],
        out_specs=pl.BlockSpec((1, MOD_ROWS, tn), lambda l, j: (l, 0, j)),
        compiler_params=_params(("arbitrary", "arbitrary")),
        name="modulation",
    )(cond8, w_mod, b_mod.reshape(DEPTH, 1, n_out))


def _mod_spec(layer, k, row_fn):
    return pl.BlockSpec((1, 1, D_MODEL), lambda i: ((layer * MOD_ROWS + row_fn(i)) * 6 + k, 0, 0))


def _rms(x, w):
    return x * lax.rsqrt(jnp.mean(x * x, axis=-1, keepdims=True) + EPS) * w


def _in_kernel(x_ref, nw_ref, sh_ref, sc_ref, wm_ref, ws_ref, pret, phy, pgdn, pssd, psm):
    h = _rms(x_ref[...], nw_ref[0]) * (1.0 + sc_ref[0]) + sh_ref[0]
    hb = h.astype(BF16)
    d = functools.partial(jnp.dot, preferred_element_type=F32)
    pret[...] = d(hb, wm_ref[0, :, 0:1024])
    phy[...] = d(hb, wm_ref[0, :, 1024:1792])
    pgdn[...] = d(hb, wm_ref[0, :, 1792:2816])
    pssd[...] = d(hb, wm_ref[0, :, 2816:3840])
    psm[...] = d(hb, ws_ref[0])


def _in_proj(x, mod3, norm1_w3, w_main, w_small, layer, row_fn):
    n = x.shape[0]
    widths = (1024, 768, 1024, 1024, SMALL_W)
    return pl.pallas_call(
        _in_kernel,
        out_shape=tuple(jax.ShapeDtypeStruct((n, w), F32) for w in widths),
        grid=(n // TOK_TILE,),
        in_specs=[
            pl.BlockSpec((TOK_TILE, D_MODEL), lambda i: (i, 0)),
            pl.BlockSpec((1, 1, D_MODEL), lambda i: (layer, 0, 0)),
            _mod_spec(layer, 0, row_fn),
            _mod_spec(layer, 1, row_fn),
            pl.BlockSpec((1, D_MODEL, IN_MAIN), lambda i: (layer, 0, 0)),
            pl.BlockSpec((1, D_MODEL, SMALL_W), lambda i: (layer, 0, 0)),
        ],
        out_specs=tuple(pl.BlockSpec((TOK_TILE, w), lambda i: (i, 0)) for w in widths),
        compiler_params=_params(("arbitrary",)),
        name="in_proj",
    )(x, norm1_w3, mod3, mod3, w_main, w_small)


def _out_kernel(final, oret, ohy, ogdn, ossd, x_ref, g1, sh2, sc2, g2, nw2, wout, w1, w2, *rest):
    if final:
        fnw, y_ref = rest
    else:
        (y_ref,) = rest
    acc = _dot(oret[...], wout[0, 0:256, :])
    acc = acc + _dot(ohy[...], wout[0, 256:512, :])
    acc = acc + _dot(ogdn[...], wout[0, 512:768, :])
    acc = acc + _dot(ossd[...], wout[0, 768:1024, :])
    x1 = x_ref[...] + g1[0] * acc
    h2 = _rms(x1, nw2[0]) * (1.0 + sc2[0]) + sh2[0]
    a = _dot(h2, w1[0])
    a = jnp.square(jnp.maximum(a, 0.0))
    x2 = x1 + g2[0] * _dot(a, w2[0])
    if final:
        x2 = _rms(x2, fnw[...])
    y_ref[...] = x2


def _out_proj(mix, x, mod3, norm2_w3, w_out, w1, w2, final_norm_w, layer, row_fn, final):
    n = x.shape[0]
    tok = lambda w: pl.BlockSpec((TOK_TILE, w), lambda i: (i, 0))
    lay = lambda a, b: pl.BlockSpec((1, a, b), lambda i: (layer, 0, 0))
    in_specs = [tok(W_GROUP)] * 4 + [
        tok(D_MODEL),
        _mod_spec(layer, 2, row_fn), _mod_spec(layer, 3, row_fn),
        _mod_spec(layer, 4, row_fn), _mod_spec(layer, 5, row_fn),
        lay(1, D_MODEL), lay(D_MODEL, D_MODEL), lay(D_MODEL, D_FF), lay(D_FF, D_MODEL),
    ]
    args = list(mix) + [x, mod3, mod3, mod3, mod3, norm2_w3, w_out, w1, w2]
    if final:
        in_specs.append(pl.BlockSpec((1, D_MODEL), lambda i: (0, 0)))
        args.append(final_norm_w.reshape(1, D_MODEL))
    return pl.pallas_call(
        functools.partial(_out_kernel, final),
        out_shape=jax.ShapeDtypeStruct((n, D_MODEL), F32),
        grid=(n // TOK_TILE,),
        in_specs=in_specs,
        out_specs=tok(D_MODEL),
        compiler_params=_params(("arbitrary",)),
        name="out_proj_mlp",
    )(*args)


def _conv3_chunk(p_ref, r0, rows, seq_len, c0, width, w_ref, first, last):
    x = p_ref[pl.ds(r0, rows), c0:c0 + width]
    prev_i = jnp.maximum(r0 - 1, 0)
    next_i = jnp.minimum(r0 + rows, seq_len - 1)
    xp = p_ref[pl.ds(prev_i, 1), c0:c0 + width] * jnp.where(first, 0.0, 1.0)
    xn = p_ref[pl.ds(next_i, 1), c0:c0 + width] * jnp.where(last, 0.0, 1.0)
    row = _iota((rows, width), 0)
    x_prev = jnp.where(row == 0, xp, pltpu.roll(x, 1, 0))
    x_next = jnp.where(row == rows - 1, xn, pltpu.roll(x, rows - 1, 0))
    return x_prev * w_ref[0:1, :] + x * w_ref[1:2, :] + x_next * w_ref[2:3, :]


def _tri(n, lower):
    i, j = _iota((n, n), 0), _iota((n, n), 1)
    return jnp.where((j <= i) if lower else (j >= i), 1.0, 0.0).astype(F32)


def _lane_expand(first_row, width):
    r, c = _iota((SMALL_W, W_GROUP), 0), _iota((SMALL_W, W_GROUP), 1)
    return jnp.where(r == first_row + c // width, 1.0, 0.0).astype(F32)


def _ret_kernel(n_chunks, rope, has_s0, *refs):
    C = SCAN_CHUNK
    refs = list(refs)
    p_ref, la_ref, nw_ref = refs[:3]
    refs = refs[3:]
    if rope:
        cos_ref, sin_ref = refs[:2]
        refs = refs[2:]
    if has_s0:
        s0_ref = refs[0]
        refs = refs[1:]
    o_ref, st_ref, sf, sb, dmat, etab, qs, ks, oscr = refs

    la_f = la_ref[0:1, :]
    la_b = la_ref[1:2, :]

    @pl.when(pl.program_id(0) == 0)
    def _():
        ri = _iota((C, W_GROUP), 0).astype(F32)
        etab[0] = jnp.exp((ri + 1.0) * la_f)
        etab[1] = jnp.exp((C - ri) * la_b)
        etab[2] = jnp.exp((C - 1.0 - ri) * la_f)
        etab[3] = jnp.exp(ri * la_b)
        ii, jj = _iota((C, C), 0), _iota((C, C), 1)
        diff = (ii - jj).astype(F32)
        for h in range(N_HEADS):
            laf_h = la_ref[0:1, D_HEAD * h:D_HEAD * h + 1]
            lab_h = la_ref[1:2, D_HEAD * h:D_HEAD * h + 1]
            dmat[h] = (jnp.exp(jnp.where(jj <= ii, diff * laf_h, NEG_BIG))
                       + jnp.exp(jnp.where(jj >= ii, -diff * lab_h, NEG_BIG)))

    dc_f = jnp.exp(C * la_f)
    dc_b = jnp.exp(C * la_b)
    bd = _head_ones()
    lane = _iota((1, W_GROUP), 1)

    sf[...] = jnp.zeros_like(sf)
    sb[...] = jnp.zeros_like(sb)
    if has_s0:
        for h in range(N_HEADS):
            sl = slice(D_HEAD * h, D_HEAD * (h + 1))
            sf[sl, sl] = s0_ref[0, 0, 0, h]
            sb[sl, sl] = s0_ref[0, 0, 1, h]

    def rot(x, cs, sn):
        halves = []
        for a in range(2):
            xh = x[:, 128 * a:128 * (a + 1)]
            lh = _iota(xh.shape, 1)
            halves.append(jnp.where(lh % D_HEAD < D_HEAD // 2, pltpu.roll(xh, 96, 1), pltpu.roll(xh, 32, 1)))
        return x * cs + jnp.concatenate(halves, axis=1) * sn

    def fwd(c, carry):
        r0 = pl.multiple_of(c * C, C)
        rows = pl.ds(r0, C)
        q = p_ref[rows, 0:256]
        k = p_ref[rows, 256:512] * (D_HEAD ** -0.5)
        v = p_ref[rows, 512:768]
        if rope:
            cs, sn = cos_ref[rows, :], sin_ref[rows, :]
            q, k = rot(q, cs, sn), rot(k, cs, sn)
        qs[rows, :] = q
        ks[rows, :] = k
        o = _dot(q * etab[0], sf[...])
        for h in range(N_HEADS):
            mh = jnp.where(lane // D_HEAD == h, 1.0, 0.0)
            att = _dot(q * mh, k, NT) * dmat[h]
            o = o + _dot(att, v * mh)
        oscr[rows, :] = o
        sf[...] = sf[...] * dc_f + bd * _dot(k * etab[2], v, TN)
        return carry

    lax.fori_loop(0, n_chunks, fwd, 0)

    def bwd(t, carry):
        c = n_chunks - 1 - t
        r0 = pl.multiple_of(c * C, C)
        rows = pl.ds(r0, C)
        q, k = qs[rows, :], ks[rows, :]
        v = p_ref[rows, 512:768]
        g = p_ref[rows, 768:1024]
        o = oscr[rows, :] + _dot(q * etab[1], sb[...])
        sb[...] = sb[...] * dc_b + bd * _dot(k * etab[3], v, TN)
        ms = _dot01(bd, o * o, m_left=False) * (1.0 / D_HEAD)
        o_ref[rows, :] = o * lax.rsqrt(ms + EPS) * nw_ref[...] * _silu(g)
        return carry

    lax.fori_loop(0, n_chunks, bwd, 0)

    for h in range(N_HEADS):
        sl = slice(D_HEAD * h, D_HEAD * (h + 1))
        st_ref[0, 0, h] = sf[sl, sl]
        st_ref[0, 1, h] = sb[sl, sl]


def _retention(p_ret, la_lane, nw, bsz, seq_len, rope_tabs, s0, layer):
    n_chunks = seq_len // SCAN_CHUNK
    rope = rope_tabs is not None
    has_s0 = s0 is not None
    const = lambda a, b: pl.BlockSpec((a, b), lambda i: (0, 0))
    in_specs = [pl.BlockSpec((seq_len, 1024), lambda i: (i, 0)), const(2, W_GROUP), const(1, W_GROUP)]
    args = [p_ret, la_lane, nw]
    if rope:
        in_specs += [const(seq_len, W_GROUP)] * 2
        args += list(rope_tabs)
    if has_s0:
        in_specs.append(pl.BlockSpec((1, 1, 2, N_HEADS, D_HEAD, D_HEAD), lambda i: (i, layer, 0, 0, 0, 0)))
        args.append(s0)
    return pl.pallas_call(
        functools.partial(_ret_kernel, n_chunks, rope, has_s0),
        out_shape=(jax.ShapeDtypeStruct((bsz * seq_len, W_GROUP), F32),
                   jax.ShapeDtypeStruct((bsz, 2, N_HEADS, D_HEAD, D_HEAD), F32)),
        grid=(bsz,),
        in_specs=in_specs,
        out_specs=(pl.BlockSpec((seq_len, W_GROUP), lambda i: (i, 0)),
                   pl.BlockSpec((1, 2, N_HEADS, D_HEAD, D_HEAD), lambda i: (i, 0, 0, 0, 0))),
        scratch_shapes=[pltpu.VMEM((W_GROUP, W_GROUP), F32), pltpu.VMEM((W_GROUP, W_GROUP), F32),
                        pltpu.VMEM((N_HEADS, SCAN_CHUNK, SCAN_CHUNK), F32),
                        pltpu.VMEM((4, SCAN_CHUNK, W_GROUP), F32),
                        pltpu.VMEM((seq_len, W_GROUP), F32), pltpu.VMEM((seq_len, W_GROUP), F32),
                        pltpu.VMEM((seq_len, W_GROUP), F32)],
        compiler_params=_params(("arbitrary",)),
        name="retention",
    )(*args)


def _ssd_kernel(n_chunks, has_s0, *refs):
    C = SCAN_CHUNK
    refs = list(refs)
    p_ref, ps_ref, cw_ref, cb_ref, dtb_ref, alog_ref, dl_ref, nw_ref = refs[:8]
    refs = refs[8:]
    if has_s0:
        s0_ref = refs[0]
        refs = refs[1:]
    o_ref, st_ref, sf, sb, xs_s, bm_s, cm_s, oscr = refs
    seq_len = n_chunks * C

    def prep(c, carry):
        r0 = pl.multiple_of(c * C, C)
        y = _conv3_chunk(p_ref, r0, C, seq_len, 0, 768, cw_ref, c == 0, c == n_chunks - 1) + cb_ref[...]
        y = _silu(y)
        rows = pl.ds(r0, C)
        xs_s[rows, :] = y[:, 0:256]
        bm_s[rows, :] = y[:, 256:512]
        cm_s[rows, :] = y[:, 512:768]
        return carry

    lax.fori_loop(0, n_chunks, prep, 0)

    sf[...] = jnp.zeros_like(sf)
    sb[...] = jnp.zeros_like(sb)
    if has_s0:
        for h in range(N_HEADS):
            g = h // 2
            sf[N_SSD * g:N_SSD * (g + 1), D_HEAD * h:D_HEAD * (h + 1)] = s0_ref[0, 0, 0, h]
            sb[N_SSD * g:N_SSD * (g + 1), D_HEAD * h:D_HEAD * (h + 1)] = s0_ref[0, 0, 1, h]

    ltri, utri = _tri(C, True), _tri(C, False)
    exp_f, exp_b = _lane_expand(16, D_HEAD), _lane_expand(20, D_HEAD)
    gmask = _head_ones(W_GROUP, N_SSD)
    lane_s = _iota((1, SMALL_W), 1)
    dt_mask = jnp.where((lane_s >= 16) & (lane_s < 24), 1.0, 0.0)
    lane = _iota((1, W_GROUP), 1)
    ii, jj = _iota((C, C), 0), _iota((C, C), 1)

    def gates(rows):
        dt = _softplus(ps_ref[rows, :] + dtb_ref[...]) * dt_mask
        la = dt * (-jnp.exp(alog_ref[...]))
        return dt, la

    def fwd(c, carry):
        r0 = pl.multiple_of(c * C, C)
        rows = pl.ds(r0, C)
        dt, la = gates(rows)
        cum_p = _dot01(ltri, la)
        cum_s = _dot01(utri, la)
        cum_pt, cum_st, dt_t = cum_p.T, cum_s.T, dt.T
        x_f = _dot01(exp_f, cum_p, m_left=False)
        dt_f = _dot01(exp_f, dt, m_left=False)
        xs, bm, cm = xs_s[rows, :], bm_s[rows, :], cm_s[rows, :]
        s_g = [_dot(cm[:, N_SSD * g:N_SSD * (g + 1)], bm[:, N_SSD * g:N_SSD * (g + 1)], NT) for g in range(G_SSD)]
        o = _dot(cm, sf[...]) * jnp.exp(x_f)
        for h in range(N_HEADS):
            f, b = 16 + h, 20 + h
            dec = (jnp.exp(jnp.where(jj <= ii, cum_p[:, f:f + 1] - cum_pt[f:f + 1, :], NEG_BIG)) * dt_t[f:f + 1, :]
                   + jnp.exp(jnp.where(jj >= ii, cum_s[:, b:b + 1] - cum_st[b:b + 1, :], NEG_BIG)) * dt_t[b:b + 1, :])
            mh = jnp.where(lane // D_HEAD == h, 1.0, 0.0)
            o = o + _dot(s_g[h // 2] * dec, xs * mh)
        oscr[rows, :] = o
        last = x_f[C - 1:C, :]
        sf[...] = sf[...] * jnp.exp(last) + gmask * _dot(bm, xs * dt_f * jnp.exp(last - x_f), TN)
        return carry

    lax.fori_loop(0, n_chunks, fwd, 0)

    def bwd(t, carry):
        c = n_chunks - 1 - t
        r0 = pl.multiple_of(c * C, C)
        rows = pl.ds(r0, C)
        dt, la = gates(rows)
        cum_s = _dot01(utri, la)
        x_b = _dot01(exp_b, cum_s, m_left=False)
        dt_b = _dot01(exp_b, dt, m_left=False)
        xs, bm, cm = xs_s[rows, :], bm_s[rows, :], cm_s[rows, :]
        o = oscr[rows, :] + _dot(cm, sb[...]) * jnp.exp(x_b)
        first = x_b[0:1, :]
        sb[...] = sb[...] * jnp.exp(first) + gmask * _dot(bm, xs * dt_b * jnp.exp(first - x_b), TN)
        y = (o + xs * dl_ref[...]) * _silu(p_ref[rows, 768:1024])
        halves = [y[:, 128 * g:128 * (g + 1)] for g in range(G_SSD)]
        halves = [yh * lax.rsqrt(jnp.mean(yh * yh, axis=-1, keepdims=True) + EPS) for yh in halves]
        o_ref[rows, :] = jnp.concatenate(halves, axis=1) * nw_ref[...]
        return carry

    lax.fori_loop(0, n_chunks, bwd, 0)

    for h in range(N_HEADS):
        g = h // 2
        st_ref[0, 0, h] = sf[N_SSD * g:N_SSD * (g + 1), D_HEAD * h:D_HEAD * (h + 1)]
        st_ref[0, 1, h] = sb[N_SSD * g:N_SSD * (g + 1), D_HEAD * h:D_HEAD * (h + 1)]


def _ssd(p_ssd, p_small, conv_w, conv_b, dtb, alog, d_lane, nw, bsz, seq_len, s0, layer):
    n_chunks = seq_len // SCAN_CHUNK
    has_s0 = s0 is not None
    const = lambda a, b: pl.BlockSpec((a, b), lambda i: (0, 0))
    in_specs = [pl.BlockSpec((seq_len, 1024), lambda i: (i, 0)), pl.BlockSpec((seq_len, SMALL_W), lambda i: (i, 0)),
                const(3, 768), const(1, 768), const(1, SMALL_W), const(1, SMALL_W), const(1, W_GROUP),
                const(1, W_GROUP)]
    args = [p_ssd, p_small, conv_w, conv_b, dtb, alog, d_lane, nw]
    if has_s0:
        in_specs.append(pl.BlockSpec((1, 1, 2, N_HEADS, N_SSD, D_HEAD), lambda i: (i, layer, 0, 0, 0, 0)))
        args.append(s0)
    return pl.pallas_call(
        functools.partial(_ssd_kernel, n_chunks, has_s0),
        out_shape=(jax.ShapeDtypeStruct((bsz * seq_len, W_GROUP), F32),
                   jax.ShapeDtypeStruct((bsz, 2, N_HEADS, N_SSD, D_HEAD), F32)),
        grid=(bsz,),
        in_specs=in_specs,
        out_specs=(pl.BlockSpec((seq_len, W_GROUP), lambda i: (i, 0)),
                   pl.BlockSpec((1, 2, N_HEADS, N_SSD, D_HEAD), lambda i: (i, 0, 0, 0, 0))),
        scratch_shapes=[pltpu.VMEM((W_GROUP, W_GROUP), F32), pltpu.VMEM((W_GROUP, W_GROUP), F32)]
        + [pltpu.VMEM((seq_len, W_GROUP), F32)] * 4,
        compiler_params=_params(("arbitrary",)),
        name="ssd",
    )(*args)


def _gdn_kernel(n_pre, n_seq, has_s0, *refs):
    C = GDN_CHUNK
    CP = SCAN_CHUNK
    R = N_HEADS * C
    refs = list(refs)
    p_ref, ps_ref, cw_ref, dtb_ref, alog_ref, nw_ref = refs[:6]
    refs = refs[6:]
    if has_s0:
        s0_ref = refs[0]
        refs = refs[1:]
    o_ref, st_ref, sf, sb, q_s, k_s, v_s, oscr_b = refs
    seq_len = n_pre * CP
    n_chunks = seq_len // C
    bd = _head_ones()

    def prep(c, carry):
        r0 = pl.multiple_of(c * CP, CP)
        cs = lax.rem(c, n_pre)
        y = _silu(_conv3_chunk(p_ref, r0, CP, n_seq * seq_len, 0, 768, cw_ref, cs == 0, cs == n_pre - 1))
        q, k, v = y[:, 0:256], y[:, 256:512], y[:, 512:768]
        q = q * lax.rsqrt(_dot01(bd, q * q, m_left=False) + EPS) * (D_HEAD ** -0.5)
        k = k * lax.rsqrt(_dot01(bd, k * k, m_left=False) + EPS)
        rows = pl.ds(r0, CP)
        q_s[rows, :] = q
        k_s[rows, :] = k
        v_s[rows, :] = v
        return carry

    lax.fori_loop(0, n_seq * n_pre, prep, 0)

    sf[...] = jnp.zeros_like(sf)
    sb[...] = jnp.zeros_like(sb)
    if has_s0:
        for j in range(n_seq):
            for h in range(N_HEADS):
                sl = slice(D_HEAD * h, D_HEAD * (h + 1))
                sf[j, sl, sl] = s0_ref[j, 0, 0, h]
                sb[j, sl, sl] = s0_ref[j, 0, 1, h]

    lane_s = _iota((1, SMALL_W), 1)
    la_mask = jnp.where(lane_s < 8, 1.0, 0.0)
    rr, cc = _iota((R, R), 0), _iota((R, R), 1)
    same = (rr // C) == (cc // C)
    head_lanes = jnp.where(_iota((R, W_GROUP), 0) // C == _iota((R, W_GROUP), 1) // D_HEAD, 1.0, 0.0)
    lane_head = _iota((1, W_GROUP), 1) // D_HEAD
    pad_rows = jnp.zeros((SMALL_W - C, SMALL_W), F32)

    def stack(x):
        return jnp.concatenate([x] * N_HEADS, axis=0)

    def per_row(x, lane0):
        return jnp.concatenate([jnp.broadcast_to(x[:, lane0 + h:lane0 + h + 1], (C, W_GROUP))
                                for h in range(N_HEADS)], axis=0)

    def each(f, *lists):
        return [f(*a) for a in zip(*lists)]

    def chunks(streams):
        seqs, cs, dirs, s_refs, o_refs = zip(*streams)
        fwd = [d == 0 for d in dirs]
        base = [4 * d for d in dirs]
        rows = [pl.ds(pl.multiple_of(j * seq_len + c * C, C), C) for j, c in zip(seqs, cs)]
        g = [ps_ref[r, :] for r in rows]
        la = [-jnp.exp(alog_ref[...]) * _softplus(x + dtb_ref[...]) * la_mask for x in g]
        beta = each(_sigmoid, g)
        cum = [_dot01(_tri(C, f), x) for f, x in zip(fwd, la)]
        last_row = [x[C - 1:C, :] if f else x[0:1, :] for f, x in zip(fwd, cum)]
        cum_t = [jnp.concatenate([x, pad_rows], axis=0).T for x in cum]
        cum_row = []
        for x, b in zip(cum_t, base):
            pc = [x[b + h:b + h + 1, :] for h in range(N_HEADS)]
            cum_row.append(jnp.concatenate([pc[0] + pltpu.roll(pc[1], C, 1), pc[2] + pltpu.roll(pc[3], C, 1)],
                                           axis=1))
        cum_cb = each(per_row, cum, base)
        beta_cb = [per_row(x, 8 + b) for x, b in zip(beta, base)]
        ecum_cb = [per_row(jnp.exp(x), b) for x, b in zip(cum, base)]
        kdec_cb = [per_row(jnp.exp(l - x), b) for l, x, b in zip(last_row, cum, base)]
        last_lane = []
        for l, b in zip(last_row, base):
            ll = jnp.zeros((1, W_GROUP), F32)
            for h in range(N_HEADS):
                ll = jnp.where(lane_head == h, l[:, b + h:b + h + 1], ll)
            last_lane.append(ll)
        incl = [same & ((cc <= rr) if f else (cc >= rr)) for f in fwd]
        strict = [same & ((cc < rr) if f else (cc > rr)) for f in fwd]
        dec = [jnp.exp(jnp.where(m, x - y, NEG_BIG)) for m, x, y in zip(incl, cum_cb, cum_row)]
        qst = [stack(q_s[r, :]) * head_lanes for r in rows]
        kst = [stack(k_s[r, :]) * head_lanes for r in rows]
        vst = [stack(v_s[r, :]) * head_lanes for r in rows]
        kk = [_dot(x, x, NT) for x in kst]
        qk = [_dot(x, y, NT) for x, y in zip(qst, kst)]
        a = [jnp.where(m, d * x, 0.0) * b for m, d, x, b in zip(strict, dec, kk, beta_cb)]
        n = [-jnp.where((rr // 2) == (cc // 2), x, 0.0) for x in a]
        s = 2
        while s < C:
            blk = ((rr // (2 * s)) == (cc // (2 * s))) & ((rr // s) != (cc // s))
            lo = [jnp.where(blk, x, 0.0) for x in a]
            y = [x + _dot_inv(x, m) for x, m in zip(lo, n)]
            n = [m - (x + _dot_inv(m, x)) for m, x in zip(n, y)]
            s *= 2
        kp = [k * (b * e) for k, b, e in zip(kst, beta_cb, ecum_cb)]
        vp = [v * b for v, b in zip(vst, beta_cb)]
        wm = [x + _dot_inv(m, x) for m, x in zip(n, kp)]
        um = [x + _dot_inv(m, x) for m, x in zip(n, vp)]
        st = [r[j] for r, j in zip(s_refs, seqs)]
        v_new = [u - _dot(w_, s_) for u, w_, s_ in zip(um, wm, st)]
        ost = [_dot(q * e, s_) + _dot(x * d, v) for q, e, s_, x, d, v in zip(qst, ecum_cb, st, qk, dec, v_new)]
        for r, o_r, x in zip(rows, o_refs, ost):
            o_r[r, :] = x[0:C] + x[C:2 * C] + x[2 * C:3 * C] + x[3 * C:4 * C]
        for s_r, j, s_, l, k, kd, v in zip(s_refs, seqs, st, last_lane, kst, kdec_cb, v_new):
            s_r[j] = s_ * jnp.exp(l) + _dot(k * kd, v, TN)

    def scan_body(t, carry):
        streams = []
        for j in range(n_seq):
            streams += [(j, t, 0, sf, o_ref), (j, n_chunks - 1 - t, 1, sb, oscr_b)]
        chunks(streams)
        return carry

    lax.fori_loop(0, n_chunks, scan_body, 0)

    def finish(c, carry):
        rows = pl.ds(pl.multiple_of(c * CP, CP), CP)
        o = o_ref[rows, :] + oscr_b[rows, :]
        ms = _dot01(bd, o * o, m_left=False) * (1.0 / D_HEAD)
        o_ref[rows, :] = o * lax.rsqrt(ms + EPS) * nw_ref[...] * _silu(p_ref[rows, 768:1024])
        return carry

    lax.fori_loop(0, n_seq * n_pre, finish, 0)

    for j in range(n_seq):
        for h in range(N_HEADS):
            sl = slice(D_HEAD * h, D_HEAD * (h + 1))
            st_ref[j, 0, h] = sf[j, sl, sl]
            st_ref[j, 1, h] = sb[j, sl, sl]


def _gdn(p_gdn, p_small, conv_w, dtb, alog, nw, bsz, seq_len, s0, layer):
    n_pre = seq_len // SCAN_CHUNK
    n_seq = GDN_SEQS
    rows = n_seq * seq_len
    has_s0 = s0 is not None
    big = dict(pipeline_mode=pl.Buffered(1)) if rows * 1024 * 4 > (8 << 20) else {}
    const = lambda a, b: pl.BlockSpec((a, b), lambda i: (0, 0))
    in_specs = [pl.BlockSpec((rows, 1024), lambda i: (i, 0), **big),
                pl.BlockSpec((rows, SMALL_W), lambda i: (i, 0), **big),
                const(3, 768), const(1, SMALL_W), const(1, SMALL_W), const(1, W_GROUP)]
    args = [p_gdn, p_small, conv_w, dtb, alog, nw]
    if has_s0:
        in_specs.append(pl.BlockSpec((n_seq, 1, 2, N_HEADS, D_HEAD, D_HEAD), lambda i: (i, layer, 0, 0, 0, 0)))
        args.append(s0)
    return pl.pallas_call(
        functools.partial(_gdn_kernel, n_pre, n_seq, has_s0),
        out_shape=(jax.ShapeDtypeStruct((bsz * seq_len, W_GROUP), F32),
                   jax.ShapeDtypeStruct((bsz, 2, N_HEADS, D_HEAD, D_HEAD), F32)),
        grid=(bsz // n_seq,),
        in_specs=in_specs,
        out_specs=(pl.BlockSpec((rows, W_GROUP), lambda i: (i, 0)),
                   pl.BlockSpec((n_seq, 2, N_HEADS, D_HEAD, D_HEAD), lambda i: (i, 0, 0, 0, 0))),
        scratch_shapes=[pltpu.VMEM((n_seq, W_GROUP, W_GROUP), F32), pltpu.VMEM((n_seq, W_GROUP, W_GROUP), F32)]
        + [pltpu.VMEM((rows, W_GROUP), F32)] * 4,
        compiler_params=_params(("arbitrary",)),
        name="gated_delta",
    )(*args)


def _dft_tables(seq_len):
    period = 8 * seq_len
    theta = math.pi / (4 * seq_len)
    fine, coarse = 64, seq_len // 64
    odd = 2 * jnp.arange(seq_len, dtype=jnp.int32) + 1
    ka = ((2 * fine * jnp.arange(coarse, dtype=jnp.int32))[:, None] * odd[None, :]) % period
    kb = (odd[:fine, None] * odd[None, :]) % period
    ang_a = ka.astype(F32)[:, None, :] * theta
    ang_b = kb.astype(F32)[None, :, :] * theta
    ca, sa, cb, sb = jnp.cos(ang_a), jnp.sin(ang_a), jnp.cos(ang_b), jnp.sin(ang_b)
    c2 = (ca * cb - sa * sb).reshape(seq_len, seq_len).astype(BF16)
    s2 = (sa * cb + ca * sb).reshape(seq_len, seq_len).astype(BF16)
    half = odd.astype(F32) * theta
    return c2, s2, jnp.stack([jnp.cos(half), jnp.sin(half)], axis=1)


def _filter_features(seq_len):
    t = jnp.linspace(0.0, 1.0, seq_len, dtype=F32)[:, None]
    bands = (HY_EMB - 1) // 2
    w = 2.0 * math.pi * jnp.arange(seq_len, dtype=F32)[:, None] / seq_len
    f = jnp.linspace(1e-4, bands - 1, bands, dtype=F32)[None, :]
    z = jnp.concatenate([t, jnp.cos(f * w), -jnp.sin(f * w)], axis=-1)
    z = jnp.pad(z, ((0, 0), (0, 128 - HY_EMB)))
    deltas = jnp.abs(jnp.linspace(math.log(HY_DECAY_TARGET) / HY_SLOW_PCT,
                                  math.log(HY_DECAY_TARGET) / HY_FAST_PCT, W_GROUP, dtype=F32))
    return z, jnp.exp(-t * deltas)


def _hyena_kernel(n_chunks, p_ref, cw_ref, cb_ref, bias_ref, nw_ref, z_ref, dec_ref, fr_ref, w1_ref, b1_ref,
                  w2_ref, b2_ref, w3_ref, c2_ref, s2_ref, ph_ref, o_ref,
                  hre_s, him_s, x0_s, u_s, ub_s, a_s, b_s, yre_s, yim_s):
    C = SCAN_CHUNK
    seq_len = n_chunks * C

    @pl.when(pl.program_id(0) == 0)
    def _():
        fr = fr_ref[0]

        def taps_chunk(c, carry):
            r0 = pl.multiple_of(c * C, C)
            rows = pl.ds(r0, C)
            h = jnp.sin(fr * (_dot_hp(z_ref[rows, :], w1_ref[0]) + b1_ref[0]))
            h = jnp.sin(fr * (_dot_hp(h, w2_ref[0]) + b2_ref[0]))
            taps = _dot_hp(h, w3_ref[0])
            hf = taps[:, 0:W_GROUP] * dec_ref[rows, :]
            hb = taps[:, W_GROUP:2 * W_GROUP] * dec_ref[rows, :]
            hb = jnp.where(_iota(hb.shape, 0) + r0 == 0, 0.0, hb)
            yre_s[rows, :] = (hf + hb).astype(BF16)
            yim_s[rows, :] = (hf - hb).astype(BF16)
            return carry

        lax.fori_loop(0, n_chunks, taps_chunk, 0)
        a_s[...] = _dot(c2_ref[...], yre_s[...])
        b_s[...] = _dot(s2_ref[...], yre_s[...])

        def hre_chunk(c, carry):
            rows = pl.ds(pl.multiple_of(c * C, C), C)
            hre_s[rows, :] = ph_ref[rows, 0:1] * a_s[rows, :] + ph_ref[rows, 1:2] * b_s[rows, :]
            return carry

        lax.fori_loop(0, n_chunks, hre_chunk, 0)
        a_s[...] = _dot(c2_ref[...], yim_s[...])
        b_s[...] = _dot(s2_ref[...], yim_s[...])

        def him_chunk(c, carry):
            rows = pl.ds(pl.multiple_of(c * C, C), C)
            him_s[rows, :] = ph_ref[rows, 1:2] * a_s[rows, :] - ph_ref[rows, 0:1] * b_s[rows, :]
            return carry

        lax.fori_loop(0, n_chunks, him_chunk, 0)

    def prep(c, carry):
        r0 = pl.multiple_of(c * C, C)
        y = _conv3_chunk(p_ref, r0, C, seq_len, 0, 768, cw_ref, c == 0, c == n_chunks - 1) + cb_ref[...]
        rows = pl.ds(r0, C)
        u = y[:, 512:768] * y[:, 256:512]
        x0_s[rows, :] = y[:, 0:256]
        u_s[rows, :] = u
        ub_s[rows, :] = u.astype(BF16)
        return carry

    lax.fori_loop(0, n_chunks, prep, 0)

    a_s[...] = _dot(c2_ref[...], ub_s[...])
    b_s[...] = _dot(s2_ref[...], ub_s[...])

    def spectrum(c, carry):
        rows = pl.ds(pl.multiple_of(c * C, C), C)
        ure, su, hre, him = a_s[rows, :], b_s[rows, :], hre_s[rows, :], him_s[rows, :]
        yre_s[rows, :] = (ure * hre + su * him).astype(BF16)
        yim_s[rows, :] = (ure * him - su * hre).astype(BF16)
        return carry

    lax.fori_loop(0, n_chunks, spectrum, 0)
    a_s[...] = _dot(c2_ref[...], yre_s[...])
    b_s[...] = _dot(s2_ref[...], yim_s[...])

    def fin(c, carry):
        rows = pl.ds(pl.multiple_of(c * C, C), C)
        conv = (a_s[rows, :] - b_s[rows, :]) * (1.0 / seq_len)
        y = x0_s[rows, :] * (conv + u_s[rows, :] * bias_ref[...])
        o_ref[rows, :] = _rms(y, nw_ref[...])
        return carry

    lax.fori_loop(0, n_chunks, fin, 0)


def _hyena(p_hy, conv_w, conv_b, bias, nw, tabs, feats, freq, w1, b1, w2, b2, w3, bsz, seq_len, layer):
    c2, s2, phase = tabs
    z, dec = feats
    one = dict(pipeline_mode=pl.Buffered(1)) if seq_len > 1024 else {}
    const = lambda a, b: pl.BlockSpec((a, b), lambda i: (0, 0), **one)
    lay = lambda a, b: pl.BlockSpec((1, a, b), lambda i: (layer, 0, 0))
    acc = pltpu.VMEM((seq_len, W_GROUP), F32)
    half = pltpu.VMEM((seq_len, W_GROUP), BF16)
    return pl.pallas_call(
        functools.partial(_hyena_kernel, seq_len // SCAN_CHUNK),
        out_shape=jax.ShapeDtypeStruct((bsz * seq_len, W_GROUP), F32),
        grid=(bsz,),
        in_specs=[pl.BlockSpec((seq_len, 768), lambda i: (i, 0), **one), const(3, 768), const(1, 768),
                  const(1, W_GROUP), const(1, W_GROUP), const(seq_len, 128), const(seq_len, W_GROUP),
                  lay(1, HY_HIDDEN), lay(128, HY_HIDDEN), lay(1, HY_HIDDEN), lay(HY_HIDDEN, HY_HIDDEN),
                  lay(1, HY_HIDDEN), lay(HY_HIDDEN, 2 * W_GROUP),
                  const(seq_len, seq_len), const(seq_len, seq_len), const(seq_len, 2)],
        out_specs=pl.BlockSpec((seq_len, W_GROUP), lambda i: (i, 0)),
        scratch_shapes=[acc, acc, acc, acc, half, acc, acc, half, half],
        compiler_params=_params(("arbitrary",)),
        name="hyena",
    )(p_hy, conv_w, conv_b, bias, nw, z, dec, freq, w1, b1, w2, b2, w3, c2, s2, phase)


def _rope_tables(seq_len):
    rows = seq_len // GRID_W
    r = jnp.repeat(jnp.arange(rows), GRID_W).astype(F32)
    col = jnp.tile(jnp.arange(GRID_W), rows).astype(F32)
    quarter = D_HEAD // 4
    inv = ROPE_BASE ** (-jnp.arange(quarter, dtype=F32) / quarter)
    ang = jnp.concatenate([r[:, None] * inv, col[:, None] * inv], axis=-1)
    cos, sin = jnp.cos(ang), jnp.sin(ang)
    cos_full = jnp.tile(jnp.concatenate([cos, cos], axis=-1), (1, N_HEADS))
    sin_signed = jnp.tile(jnp.concatenate([-sin, sin], axis=-1), (1, N_HEADS))
    return cos_full, sin_signed


def _small_vec(first_lane, vals):
    flat = vals.reshape(-1)
    return jnp.zeros((1, SMALL_W), F32).at[0, first_lane:first_lane + flat.shape[0]].set(flat)


def _trunk(x, bsz, seq_len, row_fn, states, rope_tabs, hy_tabs, hy_feats, mod3, w):
    st_ret, st_gdn, st_ssd = states
    finals = []
    for l in range(DEPTH):
        p_ret, p_hy, p_gdn, p_ssd, p_small = _in_proj(x, mod3, w['norm1'], w['w_main'], w['w_small'], l, row_fn)
        o_ret, f_ret = _retention(p_ret, w['ret_la'][l], w['ret_norm'][l], bsz, seq_len, rope_tabs, st_ret, l)
        o_hy = _hyena(p_hy, w['hy_conv_w'][l], w['hy_conv_b'][l], w['hy_bias'][l], w['hy_norm'][l],
                      hy_tabs, hy_feats, w['hy_freq'], w['hy_w1'], w['hy_b1'], w['hy_w2'], w['hy_b2'],
                      w['hy_w3'], bsz, seq_len, l)
        o_gdn, f_gdn = _gdn(p_gdn, p_small, w['gdn_conv_w'][l], w['gdn_dtb'][l], w['gdn_alog'][l],
                            w['gdn_norm'][l], bsz, seq_len, st_gdn, l)
        o_ssd, f_ssd = _ssd(p_ssd, p_small, w['ssd_conv_w'][l], w['ssd_conv_b'][l], w['ssd_dtb'][l],
                            w['ssd_alog'][l], w['ssd_dl'][l], w['ssd_norm'][l], bsz, seq_len, st_ssd, l)
        x = _out_proj((o_ret, o_hy, o_gdn, o_ssd), x, mod3, w['norm2'], w['w_out'], w['mlp_w1'], w['mlp_w2'],
                      w['final_norm'], l, row_fn, l == DEPTH - 1)
        finals.append((f_ret, f_gdn, f_ssd))
    return x, finals


def kernel(x_prompt, x_sample, state_ret, state_gdn, state_ssd, c, c_ctx, norm1_w, norm2_w, w_mod, b_mod, w_in, w_out, ret_log_decay, ret_norm_w, hy_conv_w, hy_conv_b, hy_freq, hy_w1, hy_b1, hy_w2, hy_b2, hy_w3, hy_bias, hy_norm_w, gdn_conv_w, gdn_A_log, gdn_dt_bias, gdn_norm_w, ssd_conv_w, ssd_conv_b, ssd_A_log, ssd_dt_bias, ssd_D, ssd_norm_w, mlp_w1, mlp_w2, final_norm_w):
    bsz, seq, _ = x_prompt.shape
    dbsz, dseq, _ = x_sample.shape

    cond8 = jnp.concatenate([c, c_ctx[None, :], jnp.zeros((MOD_ROWS - dbsz - 1, D_MODEL), F32)], axis=0)
    mod = _modulation(cond8, w_mod, b_mod)
    mod3 = mod.reshape(DEPTH * MOD_ROWS * 6, 1, D_MODEL)

    w_main = jnp.concatenate([w_in[:, :, 0:2816], w_in[:, :, 2832:3856]], axis=-1).astype(BF16)
    w_small = jnp.concatenate([w_in[:, :, 2816:2832], w_in[:, :, 3856:3864],
                               jnp.zeros((DEPTH, D_MODEL, SMALL_W - 24), F32)], axis=-1).astype(BF16)
    w = dict(
        norm1=norm1_w.reshape(DEPTH, 1, D_MODEL), norm2=norm2_w.reshape(DEPTH, 1, D_MODEL),
        w_main=w_main, w_small=w_small, w_out=w_out.astype(BF16),
        mlp_w1=mlp_w1.astype(BF16), mlp_w2=mlp_w2.astype(BF16), final_norm=final_norm_w,
        ret_la=[jnp.repeat(ret_log_decay[l], D_HEAD, axis=-1) for l in range(DEPTH)],
        ret_norm=[ret_norm_w[l][None, :] for l in range(DEPTH)],
        hy_freq=hy_freq.reshape(DEPTH, 1, HY_HIDDEN),
        hy_w1=jnp.pad(hy_w1, ((0, 0), (0, 128 - HY_EMB), (0, 0))),
        hy_b1=hy_b1.reshape(DEPTH, 1, HY_HIDDEN), hy_w2=hy_w2, hy_b2=hy_b2.reshape(DEPTH, 1, HY_HIDDEN),
        hy_w3=hy_w3,
        hy_conv_w=hy_conv_w, hy_conv_b=[hy_conv_b[l][None, :] for l in range(DEPTH)],
        hy_bias=[hy_bias[l][None, :] for l in range(DEPTH)], hy_norm=[hy_norm_w[l][None, :] for l in range(DEPTH)],
        gdn_conv_w=gdn_conv_w,
        gdn_dtb=[_small_vec(0, gdn_dt_bias[l]) for l in range(DEPTH)],
        gdn_alog=[_small_vec(0, gdn_A_log[l]) for l in range(DEPTH)],
        gdn_norm=[gdn_norm_w[l][None, :] for l in range(DEPTH)],
        ssd_conv_w=ssd_conv_w, ssd_conv_b=[ssd_conv_b[l][None, :] for l in range(DEPTH)],
        ssd_dtb=[_small_vec(16, ssd_dt_bias[l]) for l in range(DEPTH)],
        ssd_alog=[_small_vec(16, ssd_A_log[l]) for l in range(DEPTH)],
        ssd_dl=[jnp.repeat(ssd_D[l], D_HEAD)[None, :] for l in range(DEPTH)],
        ssd_norm=[ssd_norm_w[l][None, :] for l in range(DEPTH)],
    )

    y_p, fin = _trunk(x_prompt.reshape(bsz * seq, D_MODEL), bsz, seq, lambda i: CTX_MOD_ROW,
                      (None, None, None), None, _dft_tables(seq), _filter_features(seq), mod3, w)
    new_ret = jnp.stack([f[0] for f in fin], axis=1)
    new_gdn = jnp.stack([f[1] for f in fin], axis=1)
    new_ssd = jnp.stack([f[2] for f in fin], axis=1)

    tiles_per_seq = dseq // TOK_TILE
    y_s, _ = _trunk(x_sample.reshape(dbsz * dseq, D_MODEL), dbsz, dseq, lambda i: i // tiles_per_seq,
                    (state_ret, state_gdn, state_ssd), _rope_tables(dseq), _dft_tables(dseq),
                    _filter_features(dseq), mod3, w)

    return (y_p.reshape(bsz, seq, D_MODEL), y_s.reshape(dbsz, dseq, D_MODEL), new_ret, new_gdn, new_ssd)
```

```python
import functools
import math

import jax
import jax.numpy as jnp
from jax import lax
from jax.experimental import pallas as pl
from jax.experimental.pallas import tpu as pltpu

F32 = jnp.float32
BF16 = jnp.bfloat16

D_MODEL = 1024
DEPTH = 4
GRID_W = 64
W_GROUP = 256
N_HEADS = 4
D_HEAD = 64
N_SSD = 128
G_SSD = 2
HY_EMB = 33
HY_HIDDEN = 64
HY_DECAY_TARGET = 1e-2
HY_FAST_PCT = 0.3
HY_SLOW_PCT = 1.5
D_FF = 4 * D_MODEL
ROPE_BASE = 10000.0
EPS = 1e-6
SMALL_W = 128
MOD_ROWS = 8
CTX_MOD_ROW = 4

TOK_TILE = 256
SCAN_CHUNK = 256
GDN_CHUNK = 64
NEG_BIG = -1e30
GDN_SEQS = 4
VMEM_LIMIT = 56 * 1024 * 1024

NN = (((1,), (0,)), ((), ()))
NT = (((1,), (1,)), ((), ()))
TN = (((0,), (0,)), ((), ()))


def _dot(a, b, dims=NN):
    return lax.dot_general(a.astype(BF16), b.astype(BF16), dims, preferred_element_type=F32)


def _dot_hp(a, b, dims=NN):
    a_hi, a_lo = _split(a)
    b_hi, b_lo = _split(b)
    return _dot3(a_hi, a_lo, b_hi, b_lo, dims)


def _split(x):
    hi = x.astype(BF16)
    lo = (x - hi.astype(F32)).astype(BF16)
    return hi, lo


def _dot3(a_hi, a_lo, b_hi, b_lo, dims=NN):
    d = functools.partial(lax.dot_general, dimension_numbers=dims, preferred_element_type=F32)
    return d(a_hi, b_hi) + d(a_hi, b_lo) + d(a_lo, b_hi)


def _dot01(m, x, terms=3, m_left=True):
    mb = m.astype(BF16)
    acc, r = None, x.astype(F32)
    for _ in range(terms):
        t = r.astype(BF16)
        p = lax.dot_general(mb, t, NN, preferred_element_type=F32) if m_left else \
            lax.dot_general(t, mb, NN, preferred_element_type=F32)
        acc = p if acc is None else acc + p
        r = r - t.astype(F32)
    return acc


def _sigmoid(x):
    return 1.0 / (1.0 + jnp.exp(-x))


def _silu(x):
    return x * _sigmoid(x)


def _softplus(x):
    return jnp.maximum(x, 0.0) + jnp.log1p(jnp.exp(-jnp.abs(x)))


def _iota(shape, dim):
    return lax.broadcasted_iota(jnp.int32, shape, dim)


def _params(sem):
    return pltpu.CompilerParams(dimension_semantics=sem, vmem_limit_bytes=VMEM_LIMIT)


def _head_ones(n=W_GROUP, width=D_HEAD):
    return jnp.where(_iota((n, n), 0) // width == _iota((n, n), 1) // width, 1.0, 0.0).astype(F32)


def _mod_kernel(cond_ref, w_ref, b_ref, o_ref):
    s = _silu(cond_ref[...])
    o_ref[0] = _dot(s, w_ref[0]) + b_ref[0]


def _modulation(cond8, w_mod, b_mod):
    tn = 1536
    n_out = 6 * D_MODEL
    return pl.pallas_call(
        _mod_kernel,
        out_shape=jax.ShapeDtypeStruct((DEPTH, MOD_ROWS, n_out), F32),
        grid=(DEPTH, n_out // tn),
        in_specs=[
            pl.BlockSpec((MOD_ROWS, D_MODEL), lambda l, j: (0, 0)),
            pl.BlockSpec((1, D_MODEL, tn), lambda l, j: (l, 0, j)),
            pl.BlockSpec((1, 1, tn), lambda l, j: (l, 0, j)),
        ],
        out_specs=pl.BlockSpec((1, MOD_ROWS, tn), lambda l, j: (l, 0, j)),
        compiler_params=_params(("arbitrary", "arbitrary")),
        name="modulation",
    )(cond8, w_mod, b_mod.reshape(DEPTH, 1, n_out))


def _mod_spec(layer, k, row_fn):
    return pl.BlockSpec((1, 1, D_MODEL), lambda i: ((layer * MOD_ROWS + row_fn(i)) * 6 + k, 0, 0))


def _rms(x, w):
    return x * lax.rsqrt(jnp.mean(x * x, axis=-1, keepdims=True) + EPS) * w


def _in_kernel(x_ref, nw_ref, sh_ref, sc_ref, wm_ref, wd_ref, ws_ref, pret, phy, pgdn, pssd, psm):
    h = _rms(x_ref[...], nw_ref[0]) * (1.0 + sc_ref[0]) + sh_ref[0]
    hb = h.astype(BF16)
    d = functools.partial(jnp.dot, preferred_element_type=F32)
    pret[...] = d(hb, wm_ref[0, :, 0:1024])
    phy[...] = d(hb, wm_ref[0, :, 1024:1792])
    pgdn[...] = d(hb, wm_ref[0, :, 1792:2816])
    pssd[...] = d(hb, wd_ref[0])
    psm[...] = d(hb, ws_ref[0])


def _in_proj(x, mod3, norm1_w3, w_main, w_ssd, w_small, layer, row_fn):
    n = x.shape[0]
    widths = (1024, 768, 1024, 1024, SMALL_W)
    return pl.pallas_call(
        _in_kernel,
        out_shape=tuple(jax.ShapeDtypeStruct((n, w), F32) for w in widths),
        grid=(n // TOK_TILE,),
        in_specs=[
            pl.BlockSpec((TOK_TILE, D_MODEL), lambda i: (i, 0)),
            pl.BlockSpec((1, 1, D_MODEL), lambda i: (layer, 0, 0)),
            _mod_spec(layer, 0, row_fn),
            _mod_spec(layer, 1, row_fn),
            pl.BlockSpec((1, D_MODEL, w_main.shape[-1]), lambda i: (layer, 0, 0)),
            pl.BlockSpec((1, D_MODEL, 1024), lambda i: (layer, 0, 0)),
            pl.BlockSpec((1, D_MODEL, SMALL_W), lambda i: (layer, 0, 0)),
        ],
        out_specs=tuple(pl.BlockSpec((TOK_TILE, w), lambda i: (i, 0)) for w in widths),
        compiler_params=_params(("arbitrary",)),
        name="in_proj",
    )(x, norm1_w3, mod3, mod3, w_main, w_ssd, w_small)


def _out_kernel(final, oret, ohy, ogdn, ossd, x_ref, g1, sh2, sc2, g2, nw2, wout, w1, w2, *rest):
    if final:
        fnw, y_ref = rest
    else:
        (y_ref,) = rest
    acc = _dot(oret[...], wout[0, 0:256, :])
    acc = acc + _dot(ohy[...], wout[0, 256:512, :])
    acc = acc + _dot(ogdn[...], wout[0, 512:768, :])
    acc = acc + _dot(ossd[...], wout[0, 768:1024, :])
    x1 = x_ref[...] + g1[0] * acc
    h2 = _rms(x1, nw2[0]) * (1.0 + sc2[0]) + sh2[0]
    a = _dot(h2, w1[0])
    a = jnp.square(jnp.maximum(a, 0.0))
    x2 = x1 + g2[0] * _dot(a, w2[0])
    if final:
        x2 = _rms(x2, fnw[...])
    y_ref[...] = x2


def _out_proj(mix, x, mod3, norm2_w3, w_out, w1, w2, final_norm_w, layer, row_fn, final):
    n = x.shape[0]
    tok = lambda w: pl.BlockSpec((TOK_TILE, w), lambda i: (i, 0))
    lay = lambda a, b: pl.BlockSpec((1, a, b), lambda i: (layer, 0, 0))
    in_specs = [tok(W_GROUP)] * 4 + [
        tok(D_MODEL),
        _mod_spec(layer, 2, row_fn), _mod_spec(layer, 3, row_fn),
        _mod_spec(layer, 4, row_fn), _mod_spec(layer, 5, row_fn),
        lay(1, D_MODEL), lay(D_MODEL, D_MODEL), lay(D_MODEL, D_FF), lay(D_FF, D_MODEL),
    ]
    args = list(mix) + [x, mod3, mod3, mod3, mod3, norm2_w3, w_out, w1, w2]
    if final:
        in_specs.append(pl.BlockSpec((1, D_MODEL), lambda i: (0, 0)))
        args.append(final_norm_w.reshape(1, D_MODEL))
    return pl.pallas_call(
        functools.partial(_out_kernel, final),
        out_shape=jax.ShapeDtypeStruct((n, D_MODEL), F32),
        grid=(n // TOK_TILE,),
        in_specs=in_specs,
        out_specs=tok(D_MODEL),
        compiler_params=_params(("arbitrary",)),
        name="out_proj_mlp",
    )(*args)


def _conv3_chunk(p_ref, r0, rows, seq_len, c0, width, w_ref, first, last):
    x = p_ref[pl.ds(r0, rows), c0:c0 + width]
    prev_i = jnp.maximum(r0 - 1, 0)
    next_i = jnp.minimum(r0 + rows, seq_len - 1)
    xp = p_ref[pl.ds(prev_i, 1), c0:c0 + width] * jnp.where(first, 0.0, 1.0)
    xn = p_ref[pl.ds(next_i, 1), c0:c0 + width] * jnp.where(last, 0.0, 1.0)
    row = _iota((rows, width), 0)
    x_prev = jnp.where(row == 0, xp, pltpu.roll(x, 1, 0))
    x_next = jnp.where(row == rows - 1, xn, pltpu.roll(x, rows - 1, 0))
    return x_prev * w_ref[0:1, :] + x * w_ref[1:2, :] + x_next * w_ref[2:3, :]


def _tri(n, lower):
    i, j = _iota((n, n), 0), _iota((n, n), 1)
    return jnp.where((j <= i) if lower else (j >= i), 1.0, 0.0).astype(F32)


def _lane_expand(first_row, width):
    r, c = _iota((SMALL_W, W_GROUP), 0), _iota((SMALL_W, W_GROUP), 1)
    return jnp.where(r == first_row + c // width, 1.0, 0.0).astype(F32)


def _ret_kernel(n_chunks, rope, has_s0, *refs):
    C = SCAN_CHUNK
    refs = list(refs)
    p_ref, la_ref, nw_ref = refs[:3]
    refs = refs[3:]
    if rope:
        cos_ref, sin_ref = refs[:2]
        refs = refs[2:]
    if has_s0:
        s0_ref = refs[0]
        refs = refs[1:]
    o_ref, st_ref, sf, sb, dmat, etab, qs, ks, oscr = refs

    la_f = la_ref[0:1, :]
    la_b = la_ref[1:2, :]

    @pl.when(pl.program_id(0) == 0)
    def _():
        ri = _iota((C, W_GROUP), 0).astype(F32)
        etab[0] = jnp.exp((ri + 1.0) * la_f)
        etab[1] = jnp.exp((C - ri) * la_b)
        etab[2] = jnp.exp((C - 1.0 - ri) * la_f)
        etab[3] = jnp.exp(ri * la_b)
        ii, jj = _iota((C, C), 0), _iota((C, C), 1)
        diff = (ii - jj).astype(F32)
        for h in range(N_HEADS):
            laf_h = la_ref[0:1, D_HEAD * h:D_HEAD * h + 1]
            lab_h = la_ref[1:2, D_HEAD * h:D_HEAD * h + 1]
            dmat[h] = (jnp.exp(jnp.where(jj <= ii, diff * laf_h, NEG_BIG))
                       + jnp.exp(jnp.where(jj >= ii, -diff * lab_h, NEG_BIG)))

    dc_f = jnp.exp(C * la_f)
    dc_b = jnp.exp(C * la_b)
    bd = _head_ones()
    lane = _iota((1, W_GROUP), 1)

    sf[...] = jnp.zeros_like(sf)
    sb[...] = jnp.zeros_like(sb)
    if has_s0:
        for h in range(N_HEADS):
            sl = slice(D_HEAD * h, D_HEAD * (h + 1))
            sf[sl, sl] = s0_ref[0, 0, 0, h]
            sb[sl, sl] = s0_ref[0, 0, 1, h]

    def rot(x, cs, sn):
        halves = []
        for a in range(2):
            xh = x[:, 128 * a:128 * (a + 1)]
            lh = _iota(xh.shape, 1)
            halves.append(jnp.where(lh % D_HEAD < D_HEAD // 2, pltpu.roll(xh, 96, 1), pltpu.roll(xh, 32, 1)))
        return x * cs + jnp.concatenate(halves, axis=1) * sn

    def fwd(c, carry):
        r0 = pl.multiple_of(c * C, C)
        rows = pl.ds(r0, C)
        q = p_ref[rows, 0:256]
        k = p_ref[rows, 256:512] * (D_HEAD ** -0.5)
        v = p_ref[rows, 512:768]
        if rope:
            cs, sn = cos_ref[rows, :], sin_ref[rows, :]
            q, k = rot(q, cs, sn), rot(k, cs, sn)
        qs[rows, :] = q
        ks[rows, :] = k
        o = _dot(q * etab[0], sf[...])
        for h in range(N_HEADS):
            mh = jnp.where(lane // D_HEAD == h, 1.0, 0.0)
            att = _dot(q * mh, k, NT) * dmat[h]
            o = o + _dot(att, v * mh)
        oscr[rows, :] = o
        sf[...] = sf[...] * dc_f + bd * _dot(k * etab[2], v, TN)
        return carry

    lax.fori_loop(0, n_chunks, fwd, 0)

    def bwd(t, carry):
        c = n_chunks - 1 - t
        r0 = pl.multiple_of(c * C, C)
        rows = pl.ds(r0, C)
        q, k = qs[rows, :], ks[rows, :]
        v = p_ref[rows, 512:768]
        g = p_ref[rows, 768:1024]
        o = oscr[rows, :] + _dot(q * etab[1], sb[...])
        sb[...] = sb[...] * dc_b + bd * _dot(k * etab[3], v, TN)
        ms = _dot01(bd, o * o, m_left=False) * (1.0 / D_HEAD)
        o_ref[rows, :] = o * lax.rsqrt(ms + EPS) * nw_ref[...] * _silu(g)
        return carry

    lax.fori_loop(0, n_chunks, bwd, 0)

    for h in range(N_HEADS):
        sl = slice(D_HEAD * h, D_HEAD * (h + 1))
        st_ref[0, 0, h] = sf[sl, sl]
        st_ref[0, 1, h] = sb[sl, sl]


def _retention(p_ret, la_lane, nw, bsz, seq_len, rope_tabs, s0, layer):
    n_chunks = seq_len // SCAN_CHUNK
    rope = rope_tabs is not None
    has_s0 = s0 is not None
    const = lambda a, b: pl.BlockSpec((a, b), lambda i: (0, 0))
    in_specs = [pl.BlockSpec((seq_len, 1024), lambda i: (i, 0)), const(2, W_GROUP), const(1, W_GROUP)]
    args = [p_ret, la_lane, nw]
    if rope:
        in_specs += [const(seq_len, W_GROUP)] * 2
        args += list(rope_tabs)
    if has_s0:
        in_specs.append(pl.BlockSpec((1, 1, 2, N_HEADS, D_HEAD, D_HEAD), lambda i: (i, layer, 0, 0, 0, 0)))
        args.append(s0)
    return pl.pallas_call(
        functools.partial(_ret_kernel, n_chunks, rope, has_s0),
        out_shape=(jax.ShapeDtypeStruct((bsz * seq_len, W_GROUP), F32),
                   jax.ShapeDtypeStruct((bsz, 2, N_HEADS, D_HEAD, D_HEAD), F32)),
        grid=(bsz,),
        in_specs=in_specs,
        out_specs=(pl.BlockSpec((seq_len, W_GROUP), lambda i: (i, 0)),
                   pl.BlockSpec((1, 2, N_HEADS, D_HEAD, D_HEAD), lambda i: (i, 0, 0, 0, 0))),
        scratch_shapes=[pltpu.VMEM((W_GROUP, W_GROUP), F32), pltpu.VMEM((W_GROUP, W_GROUP), F32),
                        pltpu.VMEM((N_HEADS, SCAN_CHUNK, SCAN_CHUNK), F32),
                        pltpu.VMEM((4, SCAN_CHUNK, W_GROUP), F32),
                        pltpu.VMEM((seq_len, W_GROUP), F32), pltpu.VMEM((seq_len, W_GROUP), F32),
                        pltpu.VMEM((seq_len, W_GROUP), F32)],
        compiler_params=_params(("arbitrary",)),
        name="retention",
    )(*args)


def _ssd_kernel(n_chunks, has_s0, *refs):
    C = SCAN_CHUNK
    refs = list(refs)
    p_ref, ps_ref, cw_ref, cb_ref, dtb_ref, alog_ref, dl_ref, nw_ref = refs[:8]
    refs = refs[8:]
    if has_s0:
        s0_ref = refs[0]
        refs = refs[1:]
    o_ref, st_ref, sf, sb, xs_s, bm_s, cm_s, oscr = refs
    seq_len = n_chunks * C

    def prep(c, carry):
        r0 = pl.multiple_of(c * C, C)
        y = _conv3_chunk(p_ref, r0, C, seq_len, 0, 768, cw_ref, c == 0, c == n_chunks - 1) + cb_ref[...]
        y = _silu(y)
        rows = pl.ds(r0, C)
        xs_s[rows, :] = y[:, 0:256]
        bm_s[rows, :] = y[:, 256:512]
        cm_s[rows, :] = y[:, 512:768]
        return carry

    lax.fori_loop(0, n_chunks, prep, 0)

    sf[...] = jnp.zeros_like(sf)
    sb[...] = jnp.zeros_like(sb)
    if has_s0:
        for h in range(N_HEADS):
            g = h // 2
            sf[N_SSD * g:N_SSD * (g + 1), D_HEAD * h:D_HEAD * (h + 1)] = s0_ref[0, 0, 0, h]
            sb[N_SSD * g:N_SSD * (g + 1), D_HEAD * h:D_HEAD * (h + 1)] = s0_ref[0, 0, 1, h]

    ltri, utri = _tri(C, True), _tri(C, False)
    exp_f, exp_b = _lane_expand(16, D_HEAD), _lane_expand(20, D_HEAD)
    gmask = _head_ones(W_GROUP, N_SSD)
    lane_s = _iota((1, SMALL_W), 1)
    dt_mask = jnp.where((lane_s >= 16) & (lane_s < 24), 1.0, 0.0)
    lane = _iota((1, W_GROUP), 1)
    ii, jj = _iota((C, C), 0), _iota((C, C), 1)

    def gates(rows):
        dt = _softplus(ps_ref[rows, :] + dtb_ref[...]) * dt_mask
        la = dt * (-jnp.exp(alog_ref[...]))
        return dt, la

    def fwd(c, carry):
        r0 = pl.multiple_of(c * C, C)
        rows = pl.ds(r0, C)
        dt, la = gates(rows)
        cum_p = _dot01(ltri, la)
        cum_s = _dot01(utri, la)
        cum_pt, cum_st, dt_t = cum_p.T, cum_s.T, dt.T
        x_f = _dot01(exp_f, cum_p, m_left=False)
        dt_f = _dot01(exp_f, dt, m_left=False)
        xs, bm, cm = xs_s[rows, :], bm_s[rows, :], cm_s[rows, :]
        s_g = [_dot(cm[:, N_SSD * g:N_SSD * (g + 1)], bm[:, N_SSD * g:N_SSD * (g + 1)], NT) for g in range(G_SSD)]
        o = _dot(cm, sf[...]) * jnp.exp(x_f)
        for h in range(N_HEADS):
            f, b = 16 + h, 20 + h
            dec = (jnp.exp(jnp.where(jj <= ii, cum_p[:, f:f + 1] - cum_pt[f:f + 1, :], NEG_BIG)) * dt_t[f:f + 1, :]
                   + jnp.exp(jnp.where(jj >= ii, cum_s[:, b:b + 1] - cum_st[b:b + 1, :], NEG_BIG)) * dt_t[b:b + 1, :])
            mh = jnp.where(lane // D_HEAD == h, 1.0, 0.0)
            o = o + _dot(s_g[h // 2] * dec, xs * mh)
        oscr[rows, :] = o
        last = x_f[C - 1:C, :]
        sf[...] = sf[...] * jnp.exp(last) + gmask * _dot(bm, xs * dt_f * jnp.exp(last - x_f), TN)
        return carry

    lax.fori_loop(0, n_chunks, fwd, 0)

    def bwd(t, carry):
        c = n_chunks - 1 - t
        r0 = pl.multiple_of(c * C, C)
        rows = pl.ds(r0, C)
        dt, la = gates(rows)
        cum_s = _dot01(utri, la)
        x_b = _dot01(exp_b, cum_s, m_left=False)
        dt_b = _dot01(exp_b, dt, m_left=False)
        xs, bm, cm = xs_s[rows, :], bm_s[rows, :], cm_s[rows, :]
        o = oscr[rows, :] + _dot(cm, sb[...]) * jnp.exp(x_b)
        first = x_b[0:1, :]
        sb[...] = sb[...] * jnp.exp(first) + gmask * _dot(bm, xs * dt_b * jnp.exp(first - x_b), TN)
        y = (o + xs * dl_ref[...]) * _silu(p_ref[rows, 768:1024])
        halves = [y[:, 128 * g:128 * (g + 1)] for g in range(G_SSD)]
        halves = [yh * lax.rsqrt(jnp.mean(yh * yh, axis=-1, keepdims=True) + EPS) for yh in halves]
        o_ref[rows, :] = jnp.concatenate(halves, axis=1) * nw_ref[...]
        return carry

    lax.fori_loop(0, n_chunks, bwd, 0)

    for h in range(N_HEADS):
        g = h // 2
        st_ref[0, 0, h] = sf[N_SSD * g:N_SSD * (g + 1), D_HEAD * h:D_HEAD * (h + 1)]
        st_ref[0, 1, h] = sb[N_SSD * g:N_SSD * (g + 1), D_HEAD * h:D_HEAD * (h + 1)]


def _ssd(p_ssd, p_small, conv_w, conv_b, dtb, alog, d_lane, nw, bsz, seq_len, s0, layer):
    n_chunks = seq_len // SCAN_CHUNK
    has_s0 = s0 is not None
    const = lambda a, b: pl.BlockSpec((a, b), lambda i: (0, 0))
    in_specs = [pl.BlockSpec((seq_len, 1024), lambda i: (i, 0)), pl.BlockSpec((seq_len, SMALL_W), lambda i: (i, 0)),
                const(3, 768), const(1, 768), const(1, SMALL_W), const(1, SMALL_W), const(1, W_GROUP),
                const(1, W_GROUP)]
    args = [p_ssd, p_small, conv_w, conv_b, dtb, alog, d_lane, nw]
    if has_s0:
        in_specs.append(pl.BlockSpec((1, 1, 2, N_HEADS, N_SSD, D_HEAD), lambda i: (i, layer, 0, 0, 0, 0)))
        args.append(s0)
    return pl.pallas_call(
        functools.partial(_ssd_kernel, n_chunks, has_s0),
        out_shape=(jax.ShapeDtypeStruct((bsz * seq_len, W_GROUP), F32),
                   jax.ShapeDtypeStruct((bsz, 2, N_HEADS, N_SSD, D_HEAD), F32)),
        grid=(bsz,),
        in_specs=in_specs,
        out_specs=(pl.BlockSpec((seq_len, W_GROUP), lambda i: (i, 0)),
                   pl.BlockSpec((1, 2, N_HEADS, N_SSD, D_HEAD), lambda i: (i, 0, 0, 0, 0))),
        scratch_shapes=[pltpu.VMEM((W_GROUP, W_GROUP), F32), pltpu.VMEM((W_GROUP, W_GROUP), F32)]
        + [pltpu.VMEM((seq_len, W_GROUP), F32)] * 4,
        compiler_params=_params(("arbitrary",)),
        name="ssd",
    )(*args)


def _gdn_prep_kernel(n_pre, p_ref, cw_ref, q_ref, k_ref, v_ref):
    CP = SCAN_CHUNK
    bd = _head_ones()

    def prep(c, carry):
        r0 = pl.multiple_of(c * CP, CP)
        y = _silu(_conv3_chunk(p_ref, r0, CP, n_pre * CP, 0, 768, cw_ref, c == 0, c == n_pre - 1))
        q, k, v = y[:, 0:256], y[:, 256:512], y[:, 512:768]
        rows = pl.ds(r0, CP)
        q_ref[rows, :] = q * lax.rsqrt(_dot01(bd, q * q, m_left=False) + EPS) * (D_HEAD ** -0.5)
        k_ref[rows, :] = k * lax.rsqrt(_dot01(bd, k * k, m_left=False) + EPS)
        v_ref[rows, :] = v
        return carry

    lax.fori_loop(0, n_pre, prep, 0)


def _gdn_finish_kernel(of_ref, ob_ref, z_ref, nw_ref, o_ref):
    o = of_ref[...] + ob_ref[...]
    ms = _dot01(_head_ones(), o * o, m_left=False) * (1.0 / D_HEAD)
    o_ref[...] = o * lax.rsqrt(ms + EPS) * nw_ref[...] * _silu(z_ref[...])


def _gdn_scan_kernel(n_seq, n_blk, has_s0, *refs):
    C = GDN_CHUNK
    R = N_HEADS * C
    SUB = SCAN_CHUNK // C
    refs = list(refs)
    qf, kf, vf, gf, qb, kb, vb, gb, dtb_ref, alog_ref = refs[:10]
    refs = refs[10:]
    if has_s0:
        s0_ref = refs[0]
        refs = refs[1:]
    of_ref, ob_ref, st_ref, sf, sb = refs
    t = pl.program_id(1)

    @pl.when(t == 0)
    def _():
        sf[...] = jnp.zeros_like(sf)
        sb[...] = jnp.zeros_like(sb)
        if has_s0:
            for j in range(n_seq):
                for h in range(N_HEADS):
                    sl = slice(D_HEAD * h, D_HEAD * (h + 1))
                    sf[j, sl, sl] = s0_ref[j, 0, 0, h]
                    sb[j, sl, sl] = s0_ref[j, 0, 1, h]

    lane_s = _iota((1, SMALL_W), 1)
    la_mask = jnp.where(lane_s < 8, 1.0, 0.0)
    same = (_iota((R, R), 0) // C) == (_iota((R, R), 1) // C)
    lane_head = _iota((1, W_GROUP), 1) // D_HEAD
    pad_rows = jnp.zeros((SMALL_W - C, SMALL_W), F32)
    ic = _iota((C, R), 0)
    jc = _iota((C, R), 1) % C
    zero_b = jnp.zeros((R, R), BF16)

    def expand(x):
        xb = x.astype(BF16)
        return jnp.where(same, jnp.concatenate([xb] * N_HEADS, axis=0), zero_b)

    def per_lane(x, lane0):
        cols = [jnp.broadcast_to(x[:, lane0 + h:lane0 + h + 1], (C, W_GROUP)) for h in range(N_HEADS)]
        return jnp.where(lane_head == 0, cols[0],
                         jnp.where(lane_head == 1, cols[1], jnp.where(lane_head == 2, cols[2], cols[3])))

    def chunks(streams):
        seqs, offs, dirs = zip(*streams)
        fwd = [d == 0 for d in dirs]
        base = [4 * d for d in dirs]
        rows = [pl.ds(pl.multiple_of(r0, C), C) for r0 in offs]
        pick = lambda a_f, a_b: [(a_f if f else a_b) for f in fwd]
        q_refs, k_refs, v_refs, g_refs = pick(qf, qb), pick(kf, kb), pick(vf, vb), pick(gf, gb)
        s_refs, o_refs = pick(sf, sb), pick(of_ref, ob_ref)
        g = [r[j, rw, :] for r, j, rw in zip(g_refs, seqs, rows)]
        la = [-jnp.exp(alog_ref[...]) * _softplus(x + dtb_ref[...]) * la_mask for x in g]
        beta = [_sigmoid(x) for x in g]
        cum = [_dot01(_tri(C, f), x) for f, x in zip(fwd, la)]
        last_row = [x[C - 1:C, :] if f else x[0:1, :] for f, x in zip(fwd, cum)]
        cum_t = [jnp.concatenate([x, pad_rows], axis=0).T for x in cum]
        cum_row = []
        for x, b in zip(cum_t, base):
            pc = [x[b + h:b + h + 1, :] for h in range(N_HEADS)]
            cum_row.append(jnp.concatenate([pc[0] + pltpu.roll(pc[1], C, 1), pc[2] + pltpu.roll(pc[3], C, 1)],
                                           axis=1))
        cum_n = [per_lane(x, b) for x, b in zip(cum, base)]
        beta_n = [per_lane(x, 8 + b) for x, b in zip(beta, base)]
        last_lane = []
        for l, b in zip(last_row, base):
            ll = jnp.zeros((1, W_GROUP), F32)
            for h in range(N_HEADS):
                ll = jnp.where(lane_head == h, l[:, b + h:b + h + 1], ll)
            last_lane.append(ll)
        ecum_n = [jnp.exp(x) for x in cum_n]
        kdec_n = [jnp.exp(l - x) for l, x in zip(last_lane, cum_n)]
        incl = [(jc <= ic) if f else (jc >= ic) for f in fwd]
        strict = [(jc < ic) if f else (jc > ic) for f in fwd]
        dec = [jnp.exp(jnp.where(m, x - y, NEG_BIG)) for m, x, y in zip(incl, cum_n, cum_row)]
        q = [r[j, rw, :] for r, j, rw in zip(q_refs, seqs, rows)]
        k = [r[j, rw, :] for r, j, rw in zip(k_refs, seqs, rows)]
        v = [r[j, rw, :] for r, j, rw in zip(v_refs, seqs, rows)]
        k_x = [expand(x) for x in k]
        kk = [_dot(x, y, NT) for x, y in zip(k, k_x)]
        qk = [_dot(x, y, NT) for x, y in zip(q, k_x)]
        a = [jnp.where(m, d * x, 0.0) * b for m, d, x, b in zip(strict, dec, kk, beta_n)]
        n = [-jnp.where((ic // 2) == (jc // 2), x, 0.0) for x in a]
        s = 2
        while s < C:
            blk = ((ic // (2 * s)) == (jc // (2 * s))) & ((ic // s) != (jc // s))
            lo = [jnp.where(blk, x, 0.0) for x in a]
            y = [x + _dot(x, expand(m)) for x, m in zip(lo, n)]
            n = [m - (x + _dot(m, expand(x))) for m, x in zip(n, y)]
            s *= 2
        kp = [x * (b * e) for x, b, e in zip(k, beta_n, ecum_n)]
        vp = [x * b for x, b in zip(v, beta_n)]
        wm = [x + _dot(m, expand(x)) for m, x in zip(n, kp)]
        um = [x + _dot(m, expand(x)) for m, x in zip(n, vp)]
        st = [r[j] for r, j in zip(s_refs, seqs)]
        v_new = [u - _dot(w_, s_) for u, w_, s_ in zip(um, wm, st)]
        o = [_dot(x * e, s_) + _dot(y * d, expand(vn))
             for x, e, s_, y, d, vn in zip(q, ecum_n, st, qk, dec, v_new)]
        for j, rw, o_r, x in zip(seqs, rows, o_refs, o):
            o_r[j, rw, :] = x
        for s_r, j, s_, l, x, kd, vn in zip(s_refs, seqs, st, last_lane, k, kdec_n, v_new):
            s_r[j] = s_ * jnp.exp(l) + jnp.where(same, _dot(x * kd, vn, TN), 0.0)

    def scan_body(u, carry):
        streams = []
        for j in range(n_seq):
            streams += [(j, u * C, 0), (j, (SUB - 1 - u) * C, 1)]
        chunks(streams)
        return carry

    lax.fori_loop(0, SUB, scan_body, 0)

    @pl.when(t == n_blk - 1)
    def _():
        for j in range(n_seq):
            for h in range(N_HEADS):
                sl = slice(D_HEAD * h, D_HEAD * (h + 1))
                st_ref[j, 0, h] = sf[j, sl, sl]
                st_ref[j, 1, h] = sb[j, sl, sl]


def _gdn(p_gdn, p_small, conv_w, dtb, alog, nw, bsz, seq_len, s0, layer):
    n_tok = bsz * seq_len
    n_blk = seq_len // SCAN_CHUNK
    n_seq = GDN_SEQS
    has_s0 = s0 is not None
    tok = jax.ShapeDtypeStruct((n_tok, W_GROUP), F32)

    q, k, v = pl.pallas_call(
        functools.partial(_gdn_prep_kernel, n_blk),
        out_shape=(tok, tok, tok),
        grid=(bsz,),
        in_specs=[pl.BlockSpec((seq_len, 768), lambda i: (i, 0)), pl.BlockSpec((3, 768), lambda i: (0, 0))],
        out_specs=(pl.BlockSpec((seq_len, W_GROUP), lambda i: (i, 0)),) * 3,
        compiler_params=_params(("arbitrary",)),
        name="gated_delta_prep",
    )(p_gdn, conv_w)

    seq3 = lambda a: a.reshape(bsz, seq_len, a.shape[-1])
    fwd_blk = lambda w: pl.BlockSpec((n_seq, SCAN_CHUNK, w), lambda i, t: (i, t, 0))
    bwd_blk = lambda w: pl.BlockSpec((n_seq, SCAN_CHUNK, w), lambda i, t: (i, n_blk - 1 - t, 0))
    const = lambda a, b: pl.BlockSpec((a, b), lambda i, t: (0, 0))
    q3, k3, v3, g3 = seq3(q), seq3(k), seq3(v), seq3(p_small)
    in_specs = [fwd_blk(W_GROUP)] * 3 + [fwd_blk(SMALL_W)] + [bwd_blk(W_GROUP)] * 3 + [bwd_blk(SMALL_W)] \
        + [const(1, SMALL_W), const(1, SMALL_W)]
    args = [q3, k3, v3, g3, q3, k3, v3, g3, dtb, alog]
    if has_s0:
        in_specs.append(pl.BlockSpec((n_seq, 1, 2, N_HEADS, D_HEAD, D_HEAD), lambda i, t: (i, layer, 0, 0, 0, 0)))
        args.append(s0)
    seq_out = jax.ShapeDtypeStruct((bsz, seq_len, W_GROUP), F32)
    o_f, o_b, state = pl.pallas_call(
        functools.partial(_gdn_scan_kernel, n_seq, n_blk, has_s0),
        out_shape=(seq_out, seq_out, jax.ShapeDtypeStruct((bsz, 2, N_HEADS, D_HEAD, D_HEAD), F32)),
        grid=(bsz // n_seq, n_blk),
        in_specs=in_specs,
        out_specs=(fwd_blk(W_GROUP), bwd_blk(W_GROUP),
                   pl.BlockSpec((n_seq, 2, N_HEADS, D_HEAD, D_HEAD), lambda i, t: (i, 0, 0, 0, 0))),
        scratch_shapes=[pltpu.VMEM((n_seq, W_GROUP, W_GROUP), F32), pltpu.VMEM((n_seq, W_GROUP, W_GROUP), F32)],
        compiler_params=_params(("arbitrary", "arbitrary")),
        name="gated_delta_scan",
    )(*args)

    tile = lambda c: pl.BlockSpec((TOK_TILE, W_GROUP), lambda i: (i, c))
    o = pl.pallas_call(
        _gdn_finish_kernel,
        out_shape=tok,
        grid=(n_tok // TOK_TILE,),
        in_specs=[tile(0), tile(0), tile(3), pl.BlockSpec((1, W_GROUP), lambda i: (0, 0))],
        out_specs=tile(0),
        compiler_params=_params(("arbitrary",)),
        name="gated_delta_finish",
    )(o_f.reshape(n_tok, W_GROUP), o_b.reshape(n_tok, W_GROUP), p_gdn, nw)
    return o, state


def _dft_tables(seq_len):
    period = 8 * seq_len
    theta = math.pi / (4 * seq_len)
    fine, coarse = 64, seq_len // 64
    odd = 2 * jnp.arange(seq_len, dtype=jnp.int32) + 1
    ka = ((2 * fine * jnp.arange(coarse, dtype=jnp.int32))[:, None] * odd[None, :]) % period
    kb = (odd[:fine, None] * odd[None, :]) % period
    ang_a = ka.astype(F32)[:, None, :] * theta
    ang_b = kb.astype(F32)[None, :, :] * theta
    ca, sa, cb, sb = jnp.cos(ang_a), jnp.sin(ang_a), jnp.cos(ang_b), jnp.sin(ang_b)
    c2 = (ca * cb - sa * sb).reshape(seq_len, seq_len).astype(BF16)
    s2 = (sa * cb + ca * sb).reshape(seq_len, seq_len).astype(BF16)
    half = odd.astype(F32) * theta
    return c2, s2, jnp.stack([jnp.cos(half), jnp.sin(half)], axis=1)


def _filter_features(seq_len):
    t = jnp.linspace(0.0, 1.0, seq_len, dtype=F32)[:, None]
    bands = (HY_EMB - 1) // 2
    w = 2.0 * math.pi * jnp.arange(seq_len, dtype=F32)[:, None] / seq_len
    f = jnp.linspace(1e-4, bands - 1, bands, dtype=F32)[None, :]
    z = jnp.concatenate([t, jnp.cos(f * w), -jnp.sin(f * w)], axis=-1)
    z = jnp.pad(z, ((0, 0), (0, 128 - HY_EMB)))
    deltas = jnp.abs(jnp.linspace(math.log(HY_DECAY_TARGET) / HY_SLOW_PCT,
                                  math.log(HY_DECAY_TARGET) / HY_FAST_PCT, W_GROUP, dtype=F32))
    return z, jnp.exp(-t * deltas)


def _hyena_kernel(n_chunks, p_ref, cw_ref, cb_ref, bias_ref, nw_ref, z_ref, dec_ref, fr_ref, w1_ref, b1_ref,
                  w2_ref, b2_ref, w3_ref, c2_ref, s2_ref, ph_ref, o_ref,
                  hre_s, him_s, x0_s, u_s, ub_s, a_s, b_s, yre_s, yim_s):
    C = SCAN_CHUNK
    seq_len = n_chunks * C

    @pl.when(pl.program_id(0) == 0)
    def _():
        fr = fr_ref[0]

        def taps_chunk(c, carry):
            r0 = pl.multiple_of(c * C, C)
            rows = pl.ds(r0, C)
            h = jnp.sin(fr * (_dot_hp(z_ref[rows, :], w1_ref[0]) + b1_ref[0]))
            h = jnp.sin(fr * (_dot_hp(h, w2_ref[0]) + b2_ref[0]))
            taps = _dot_hp(h, w3_ref[0])
            hf = taps[:, 0:W_GROUP] * dec_ref[rows, :]
            hb = taps[:, W_GROUP:2 * W_GROUP] * dec_ref[rows, :]
            hb = jnp.where(_iota(hb.shape, 0) + r0 == 0, 0.0, hb)
            yre_s[rows, :] = (hf + hb).astype(BF16)
            yim_s[rows, :] = (hf - hb).astype(BF16)
            return carry

        lax.fori_loop(0, n_chunks, taps_chunk, 0)
        a_s[...] = _dot(c2_ref[...], yre_s[...])
        b_s[...] = _dot(s2_ref[...], yre_s[...])

        def hre_chunk(c, carry):
            rows = pl.ds(pl.multiple_of(c * C, C), C)
            hre_s[rows, :] = ph_ref[rows, 0:1] * a_s[rows, :] + ph_ref[rows, 1:2] * b_s[rows, :]
            return carry

        lax.fori_loop(0, n_chunks, hre_chunk, 0)
        a_s[...] = _dot(c2_ref[...], yim_s[...])
        b_s[...] = _dot(s2_ref[...], yim_s[...])

        def him_chunk(c, carry):
            rows = pl.ds(pl.multiple_of(c * C, C), C)
            him_s[rows, :] = ph_ref[rows, 1:2] * a_s[rows, :] - ph_ref[rows, 0:1] * b_s[rows, :]
            return carry

        lax.fori_loop(0, n_chunks, him_chunk, 0)

    def prep(c, carry):
        r0 = pl.multiple_of(c * C, C)
        y = _conv3_chunk(p_ref, r0, C, seq_len, 0, 768, cw_ref, c == 0, c == n_chunks - 1) + cb_ref[...]
        rows = pl.ds(r0, C)
        u = y[:, 512:768] * y[:, 256:512]
        x0_s[rows, :] = y[:, 0:256]
        u_s[rows, :] = u
        ub_s[rows, :] = u.astype(BF16)
        return carry

    lax.fori_loop(0, n_chunks, prep, 0)

    a_s[...] = _dot(c2_ref[...], ub_s[...])
    b_s[...] = _dot(s2_ref[...], ub_s[...])

    def spectrum(c, carry):
        rows = pl.ds(pl.multiple_of(c * C, C), C)
        ure, su, hre, him = a_s[rows, :], b_s[rows, :], hre_s[rows, :], him_s[rows, :]
        yre_s[rows, :] = (ure * hre + su * him).astype(BF16)
        yim_s[rows, :] = (ure * him - su * hre).astype(BF16)
        return carry

    lax.fori_loop(0, n_chunks, spectrum, 0)
    a_s[...] = _dot(c2_ref[...], yre_s[...])
    b_s[...] = _dot(s2_ref[...], yim_s[...])

    def fin(c, carry):
        rows = pl.ds(pl.multiple_of(c * C, C), C)
        conv = (a_s[rows, :] - b_s[rows, :]) * (1.0 / seq_len)
        y = x0_s[rows, :] * (conv + u_s[rows, :] * bias_ref[...])
        o_ref[rows, :] = _rms(y, nw_ref[...])
        return carry

    lax.fori_loop(0, n_chunks, fin, 0)


def _hyena(p_hy, conv_w, conv_b, bias, nw, tabs, feats, freq, w1, b1, w2, b2, w3, bsz, seq_len, layer):
    c2, s2, phase = tabs
    z, dec = feats
    one = dict(pipeline_mode=pl.Buffered(1)) if seq_len > 1024 else {}
    const = lambda a, b: pl.BlockSpec((a, b), lambda i: (0, 0), **one)
    lay = lambda a, b: pl.BlockSpec((1, a, b), lambda i: (layer, 0, 0))
    acc = pltpu.VMEM((seq_len, W_GROUP), F32)
    half = pltpu.VMEM((seq_len, W_GROUP), BF16)
    return pl.pallas_call(
        functools.partial(_hyena_kernel, seq_len // SCAN_CHUNK),
        out_shape=jax.ShapeDtypeStruct((bsz * seq_len, W_GROUP), F32),
        grid=(bsz,),
        in_specs=[pl.BlockSpec((seq_len, 768), lambda i: (i, 0), **one), const(3, 768), const(1, 768),
                  const(1, W_GROUP), const(1, W_GROUP), const(seq_len, 128), const(seq_len, W_GROUP),
                  lay(1, HY_HIDDEN), lay(128, HY_HIDDEN), lay(1, HY_HIDDEN), lay(HY_HIDDEN, HY_HIDDEN),
                  lay(1, HY_HIDDEN), lay(HY_HIDDEN, 2 * W_GROUP),
                  const(seq_len, seq_len), const(seq_len, seq_len), const(seq_len, 2)],
        out_specs=pl.BlockSpec((seq_len, W_GROUP), lambda i: (i, 0)),
        scratch_shapes=[acc, acc, acc, acc, half, acc, acc, half, half],
        compiler_params=_params(("arbitrary",)),
        name="hyena",
    )(p_hy, conv_w, conv_b, bias, nw, z, dec, freq, w1, b1, w2, b2, w3, c2, s2, phase)


def _rope_tables(seq_len):
    rows = seq_len // GRID_W
    r = jnp.repeat(jnp.arange(rows), GRID_W).astype(F32)
    col = jnp.tile(jnp.arange(GRID_W), rows).astype(F32)
    quarter = D_HEAD // 4
    inv = ROPE_BASE ** (-jnp.arange(quarter, dtype=F32) / quarter)
    ang = jnp.concatenate([r[:, None] * inv, col[:, None] * inv], axis=-1)
    cos, sin = jnp.cos(ang), jnp.sin(ang)
    cos_full = jnp.tile(jnp.concatenate([cos, cos], axis=-1), (1, N_HEADS))
    sin_signed = jnp.tile(jnp.concatenate([-sin, sin], axis=-1), (1, N_HEADS))
    return cos_full, sin_signed


def _small_vec(first_lane, vals):
    flat = vals.reshape(-1)
    return jnp.zeros((1, SMALL_W), F32).at[0, first_lane:first_lane + flat.shape[0]].set(flat)


def _trunk(x, bsz, seq_len, row_fn, states, rope_tabs, hy_tabs, hy_feats, mod3, w):
    st_ret, st_gdn, st_ssd = states
    finals = []
    for l in range(DEPTH):
        p_ret, p_hy, p_gdn, p_ssd, p_small = _in_proj(x, mod3, w['norm1'], w['w_main'], w['w_ssd'],
                                                      w['w_small'], l, row_fn)
        o_ret, f_ret = _retention(p_ret, w['ret_la'][l], w['ret_norm'][l], bsz, seq_len, rope_tabs, st_ret, l)
        o_hy = _hyena(p_hy, w['hy_conv_w'][l], w['hy_conv_b'][l], w['hy_bias'][l], w['hy_norm'][l],
                      hy_tabs, hy_feats, w['hy_freq'], w['hy_w1'], w['hy_b1'], w['hy_w2'], w['hy_b2'],
                      w['hy_w3'], bsz, seq_len, l)
        o_gdn, f_gdn = _gdn(p_gdn, p_small, w['gdn_conv_w'][l], w['gdn_dtb'][l], w['gdn_alog'][l],
                            w['gdn_norm'][l], bsz, seq_len, st_gdn, l)
        o_ssd, f_ssd = _ssd(p_ssd, p_small, w['ssd_conv_w'][l], w['ssd_conv_b'][l], w['ssd_dtb'][l],
                            w['ssd_alog'][l], w['ssd_dl'][l], w['ssd_norm'][l], bsz, seq_len, st_ssd, l)
        x = _out_proj((o_ret, o_hy, o_gdn, o_ssd), x, mod3, w['norm2'], w['w_out'], w['mlp_w1'], w['mlp_w2'],
                      w['final_norm'], l, row_fn, l == DEPTH - 1)
        finals.append((f_ret, f_gdn, f_ssd))
    return x, finals


def kernel(x_prompt, x_sample, state_ret, state_gdn, state_ssd, c, c_ctx, norm1_w, norm2_w, w_mod, b_mod, w_in, w_out, ret_log_decay, ret_norm_w, hy_conv_w, hy_conv_b, hy_freq, hy_w1, hy_b1, hy_w2, hy_b2, hy_w3, hy_bias, hy_norm_w, gdn_conv_w, gdn_A_log, gdn_dt_bias, gdn_norm_w, ssd_conv_w, ssd_conv_b, ssd_A_log, ssd_dt_bias, ssd_D, ssd_norm_w, mlp_w1, mlp_w2, final_norm_w):
    bsz, seq, _ = x_prompt.shape
    dbsz, dseq, _ = x_sample.shape

    cond8 = jnp.concatenate([c, c_ctx[None, :], jnp.zeros((MOD_ROWS - dbsz - 1, D_MODEL), F32)], axis=0)
    mod = _modulation(cond8, w_mod, b_mod)
    mod3 = mod.reshape(DEPTH * MOD_ROWS * 6, 1, D_MODEL)

    w_main = w_in.astype(BF16)
    w_ssd = w_main[:, :, 2832:3856]
    w_small = jnp.concatenate([w_main[:, :, 2816:2832], w_main[:, :, 3856:3864],
                               jnp.zeros((DEPTH, D_MODEL, SMALL_W - 24), BF16)], axis=-1)
    w = dict(
        norm1=norm1_w.reshape(DEPTH, 1, D_MODEL), norm2=norm2_w.reshape(DEPTH, 1, D_MODEL),
        w_main=w_main, w_ssd=w_ssd, w_small=w_small, w_out=w_out.astype(BF16),
        mlp_w1=mlp_w1.astype(BF16), mlp_w2=mlp_w2.astype(BF16), final_norm=final_norm_w,
        ret_la=[jnp.repeat(ret_log_decay[l], D_HEAD, axis=-1) for l in range(DEPTH)],
        ret_norm=[ret_norm_w[l][None, :] for l in range(DEPTH)],
        hy_freq=hy_freq.reshape(DEPTH, 1, HY_HIDDEN),
        hy_w1=jnp.pad(hy_w1, ((0, 0), (0, 128 - HY_EMB), (0, 0))),
        hy_b1=hy_b1.reshape(DEPTH, 1, HY_HIDDEN), hy_w2=hy_w2, hy_b2=hy_b2.reshape(DEPTH, 1, HY_HIDDEN),
        hy_w3=hy_w3,
        hy_conv_w=hy_conv_w, hy_conv_b=[hy_conv_b[l][None, :] for l in range(DEPTH)],
        hy_bias=[hy_bias[l][None, :] for l in range(DEPTH)], hy_norm=[hy_norm_w[l][None, :] for l in range(DEPTH)],
        gdn_conv_w=gdn_conv_w,
        gdn_dtb=[_small_vec(0, gdn_dt_bias[l]) for l in range(DEPTH)],
        gdn_alog=[_small_vec(0, gdn_A_log[l]) for l in range(DEPTH)],
        gdn_norm=[gdn_norm_w[l][None, :] for l in range(DEPTH)],
        ssd_conv_w=ssd_conv_w, ssd_conv_b=[ssd_conv_b[l][None, :] for l in range(DEPTH)],
        ssd_dtb=[_small_vec(16, ssd_dt_bias[l]) for l in range(DEPTH)],
        ssd_alog=[_small_vec(16, ssd_A_log[l]) for l in range(DEPTH)],
        ssd_dl=[jnp.repeat(ssd_D[l], D_HEAD)[None, :] for l in range(DEPTH)],
        ssd_norm=[ssd_norm_w[l][None, :] for l in range(DEPTH)],
    )

    y_p, fin = _trunk(x_prompt.reshape(bsz * seq, D_MODEL), bsz, seq, lambda i: CTX_MOD_ROW,
                      (None, None, None), None, _dft_tables(seq), _filter_features(seq), mod3, w)
    new_ret = jnp.stack([f[0] for f in fin], axis=1)
    new_gdn = jnp.stack([f[1] for f in fin], axis=1)
    new_ssd = jnp.stack([f[2] for f in fin], axis=1)

    tiles_per_seq = dseq // TOK_TILE
    y_s, _ = _trunk(x_sample.reshape(dbsz * dseq, D_MODEL), dbsz, dseq, lambda i: i // tiles_per_seq,
                    (state_ret, state_gdn, state_ssd), _rope_tables(dseq), _dft_tables(dseq),
                    _filter_features(dseq), mod3, w)

    return (y_p.reshape(bsz, seq, D_MODEL), y_s.reshape(dbsz, dseq, D_MODEL), new_ret, new_gdn, new_ssd)
```

```python
import functools
import math

import jax
import jax.numpy as jnp
from jax import lax
from jax.experimental import pallas as pl
from jax.experimental.pallas import tpu as pltpu

F32 = jnp.float32
BF16 = jnp.bfloat16

D_MODEL = 1024
DEPTH = 4
GRID_W = 64
W_GROUP = 256
N_HEADS = 4
D_HEAD = 64
N_SSD = 128
G_SSD = 2
HY_EMB = 33
HY_HIDDEN = 64
HY_DECAY_TARGET = 1e-2
HY_FAST_PCT = 0.3
HY_SLOW_PCT = 1.5
D_FF = 4 * D_MODEL
ROPE_BASE = 10000.0
EPS = 1e-6
SMALL_W = 128
MOD_ROWS = 8
CTX_MOD_ROW = 4

TOK_TILE = 256
SCAN_CHUNK = 256
GDN_CHUNK = 64
NEG_BIG = -1e30
GDN_SEQS = 4
VMEM_LIMIT = 56 * 1024 * 1024

NN = (((1,), (0,)), ((), ()))
NT = (((1,), (1,)), ((), ()))
TN = (((0,), (0,)), ((), ()))


def _dot(a, b, dims=NN):
    return lax.dot_general(a.astype(BF16), b.astype(BF16), dims, preferred_element_type=F32)


def _dot_hp(a, b, dims=NN):
    a_hi, a_lo = _split(a)
    b_hi, b_lo = _split(b)
    return _dot3(a_hi, a_lo, b_hi, b_lo, dims)


def _split(x):
    hi = x.astype(BF16)
    lo = (x - hi.astype(F32)).astype(BF16)
    return hi, lo


def _dot3(a_hi, a_lo, b_hi, b_lo, dims=NN):
    d = functools.partial(lax.dot_general, dimension_numbers=dims, preferred_element_type=F32)
    return d(a_hi, b_hi) + d(a_hi, b_lo) + d(a_lo, b_hi)


def _dot01(m, x, terms=3, m_left=True):
    mb = m.astype(BF16)
    acc, r = None, x.astype(F32)
    for _ in range(terms):
        t = r.astype(BF16)
        p = lax.dot_general(mb, t, NN, preferred_element_type=F32) if m_left else \
            lax.dot_general(t, mb, NN, preferred_element_type=F32)
        acc = p if acc is None else acc + p
        r = r - t.astype(F32)
    return acc


def _sigmoid(x):
    return 1.0 / (1.0 + jnp.exp(-x))


def _silu(x):
    return x * _sigmoid(x)


def _softplus(x):
    return jnp.maximum(x, 0.0) + jnp.log1p(jnp.exp(-jnp.abs(x)))


def _iota(shape, dim):
    return lax.broadcasted_iota(jnp.int32, shape, dim)


def _params(sem):
    return pltpu.CompilerParams(dimension_semantics=sem, vmem_limit_bytes=VMEM_LIMIT)


def _head_ones(n=W_GROUP, width=D_HEAD):
    return jnp.where(_iota((n, n), 0) // width == _iota((n, n), 1) // width, 1.0, 0.0).astype(F32)


def _mod_kernel(cond_ref, w_ref, b_ref, o_ref):
    s = _silu(cond_ref[...])
    o_ref[0] = _dot(s, w_ref[0]) + b_ref[0]


def _modulation(cond8, w_mod, b_mod):
    tn = 1536
    n_out = 6 * D_MODEL
    return pl.pallas_call(
        _mod_kernel,
        out_shape=jax.ShapeDtypeStruct((DEPTH, MOD_ROWS, n_out), F32),
        grid=(DEPTH, n_out // tn),
        in_specs=[
            pl.BlockSpec((MOD_ROWS, D_MODEL), lambda l, j: (0, 0)),
            pl.BlockSpec((1, D_MODEL, tn), lambda l, j: (l, 0, j)),
            pl.BlockSpec((1, 1, tn), lambda l, j: (l, 0, j)),
        ],
        out_specs=pl.BlockSpec((1, MOD_ROWS, tn), lambda l, j: (l, 0, j)),
        compiler_params=_params(("arbitrary", "arbitrary")),
        name="modulation",
    )(cond8, w_mod, b_mod.reshape(DEPTH, 1, n_out))


def _mod_spec(layer, k, row_fn):
    return pl.BlockSpec((1, 1, D_MODEL), lambda i: ((layer * MOD_ROWS + row_fn(i)) * 6 + k, 0, 0))


def _rms(x, w):
    return x * lax.rsqrt(jnp.mean(x * x, axis=-1, keepdims=True) + EPS) * w


def _in_kernel(x_ref, nw_ref, sh_ref, sc_ref, wm_ref, wd_ref, ws_ref, pret, phy, pgdn, pssd, psm):
    h = _rms(x_ref[...], nw_ref[0]) * (1.0 + sc_ref[0]) + sh_ref[0]
    hb = h.astype(BF16)
    d = functools.partial(jnp.dot, preferred_element_type=F32)
    pret[...] = d(hb, wm_ref[0, :, 0:1024])
    phy[...] = d(hb, wm_ref[0, :, 1024:1792])
    pgdn[...] = d(hb, wm_ref[0, :, 1792:2816])
    pssd[...] = d(hb, wd_ref[0])
    psm[...] = d(hb, ws_ref[0])


def _in_proj(x, mod3, norm1_w3, w_main, w_ssd, w_small, layer, row_fn):
    n = x.shape[0]
    widths = (1024, 768, 1024, 1024, SMALL_W)
    return pl.pallas_call(
        _in_kernel,
        out_shape=tuple(jax.ShapeDtypeStruct((n, w), F32) for w in widths),
        grid=(n // TOK_TILE,),
        in_specs=[
            pl.BlockSpec((TOK_TILE, D_MODEL), lambda i: (i, 0)),
            pl.BlockSpec((1, 1, D_MODEL), lambda i: (layer, 0, 0)),
            _mod_spec(layer, 0, row_fn),
            _mod_spec(layer, 1, row_fn),
            pl.BlockSpec((1, D_MODEL, w_main.shape[-1]), lambda i: (layer, 0, 0)),
            pl.BlockSpec((1, D_MODEL, 1024), lambda i: (layer, 0, 0)),
            pl.BlockSpec((1, D_MODEL, SMALL_W), lambda i: (layer, 0, 0)),
        ],
        out_specs=tuple(pl.BlockSpec((TOK_TILE, w), lambda i: (i, 0)) for w in widths),
        compiler_params=_params(("arbitrary",)),
        name="in_proj",
    )(x, norm1_w3, mod3, mod3, w_main, w_ssd, w_small)


def _out_kernel(final, oret, ohy, ogf, ogb, zg, gnw, ossd, x_ref, g1, sh2, sc2, g2, nw2, wout, w1, w2, *rest):
    if final:
        fnw, y_ref = rest
    else:
        (y_ref,) = rest
    og = ogf[...] + ogb[...]
    ms = _dot01(_head_ones(), og * og, m_left=False) * (1.0 / D_HEAD)
    ogdn = og * lax.rsqrt(ms + EPS) * gnw[...] * _silu(zg[...])
    acc = _dot(oret[...], wout[0, 0:256, :])
    acc = acc + _dot(ohy[...], wout[0, 256:512, :])
    acc = acc + _dot(ogdn, wout[0, 512:768, :])
    acc = acc + _dot(ossd[...], wout[0, 768:1024, :])
    x1 = x_ref[...] + g1[0] * acc
    h2 = _rms(x1, nw2[0]) * (1.0 + sc2[0]) + sh2[0]
    a = _dot(h2, w1[0])
    a = jnp.square(jnp.maximum(a, 0.0))
    x2 = x1 + g2[0] * _dot(a, w2[0])
    if final:
        x2 = _rms(x2, fnw[...])
    y_ref[...] = x2


def _out_proj(mix, x, mod3, norm2_w3, w_out, w1, w2, final_norm_w, layer, row_fn, final):
    n = x.shape[0]
    tok = lambda w: pl.BlockSpec((TOK_TILE, w), lambda i: (i, 0))
    lay = lambda a, b: pl.BlockSpec((1, a, b), lambda i: (layer, 0, 0))
    in_specs = [tok(W_GROUP)] * 4 + [
        pl.BlockSpec((TOK_TILE, W_GROUP), lambda i: (i, 3)), pl.BlockSpec((1, W_GROUP), lambda i: (0, 0)),
        tok(W_GROUP),
        tok(D_MODEL),
        _mod_spec(layer, 2, row_fn), _mod_spec(layer, 3, row_fn),
        _mod_spec(layer, 4, row_fn), _mod_spec(layer, 5, row_fn),
        lay(1, D_MODEL), lay(D_MODEL, D_MODEL), lay(D_MODEL, D_FF), lay(D_FF, D_MODEL),
    ]
    args = list(mix) + [x, mod3, mod3, mod3, mod3, norm2_w3, w_out, w1, w2]
    if final:
        in_specs.append(pl.BlockSpec((1, D_MODEL), lambda i: (0, 0)))
        args.append(final_norm_w.reshape(1, D_MODEL))
    return pl.pallas_call(
        functools.partial(_out_kernel, final),
        out_shape=jax.ShapeDtypeStruct((n, D_MODEL), F32),
        grid=(n // TOK_TILE,),
        in_specs=in_specs,
        out_specs=tok(D_MODEL),
        compiler_params=_params(("arbitrary",)),
        name="out_proj_mlp",
    )(*args)


def _conv3_chunk(p_ref, r0, rows, seq_len, c0, width, w_ref, first, last):
    x = p_ref[pl.ds(r0, rows), c0:c0 + width]
    prev_i = jnp.maximum(r0 - 1, 0)
    next_i = jnp.minimum(r0 + rows, seq_len - 1)
    xp = p_ref[pl.ds(prev_i, 1), c0:c0 + width] * jnp.where(first, 0.0, 1.0)
    xn = p_ref[pl.ds(next_i, 1), c0:c0 + width] * jnp.where(last, 0.0, 1.0)
    row = _iota((rows, width), 0)
    x_prev = jnp.where(row == 0, xp, pltpu.roll(x, 1, 0))
    x_next = jnp.where(row == rows - 1, xn, pltpu.roll(x, rows - 1, 0))
    return x_prev * w_ref[0:1, :] + x * w_ref[1:2, :] + x_next * w_ref[2:3, :]


def _tri(n, lower):
    i, j = _iota((n, n), 0), _iota((n, n), 1)
    return jnp.where((j <= i) if lower else (j >= i), 1.0, 0.0).astype(F32)


def _lane_expand(first_row, width):
    r, c = _iota((SMALL_W, W_GROUP), 0), _iota((SMALL_W, W_GROUP), 1)
    return jnp.where(r == first_row + c // width, 1.0, 0.0).astype(F32)


def _ret_kernel(n_chunks, rope, has_s0, has_acc, *refs):
    C = SCAN_CHUNK
    refs = list(refs)
    p_ref, la_ref, nw_ref = refs[:3]
    refs = refs[3:]
    if rope:
        cos_ref, sin_ref = refs[:2]
        refs = refs[2:]
    if has_s0:
        s0_ref = refs[0]
        refs = refs[1:]
    if has_acc:
        refs = refs[1:]
    o_ref, st_ref, sf, sb, dmat, etab, qs, ks, oscr = refs

    la_f = la_ref[0:1, :]
    la_b = la_ref[1:2, :]

    @pl.when(pl.program_id(0) == 0)
    def _():
        ri = _iota((C, W_GROUP), 0).astype(F32)
        etab[0] = jnp.exp((ri + 1.0) * la_f)
        etab[1] = jnp.exp((C - ri) * la_b)
        etab[2] = jnp.exp((C - 1.0 - ri) * la_f)
        etab[3] = jnp.exp(ri * la_b)
        ii, jj = _iota((C, C), 0), _iota((C, C), 1)
        diff = (ii - jj).astype(F32)
        for h in range(N_HEADS):
            laf_h = la_ref[0:1, D_HEAD * h:D_HEAD * h + 1]
            lab_h = la_ref[1:2, D_HEAD * h:D_HEAD * h + 1]
            dmat[h] = (jnp.exp(jnp.where(jj <= ii, diff * laf_h, NEG_BIG))
                       + jnp.exp(jnp.where(jj >= ii, -diff * lab_h, NEG_BIG)))

    dc_f = jnp.exp(C * la_f)
    dc_b = jnp.exp(C * la_b)
    bd = _head_ones()
    lane = _iota((1, W_GROUP), 1)

    sf[...] = jnp.zeros_like(sf)
    sb[...] = jnp.zeros_like(sb)
    if has_s0:
        for h in range(N_HEADS):
            sl = slice(D_HEAD * h, D_HEAD * (h + 1))
            sf[sl, sl] = s0_ref[0, 0, 0, h]
            sb[sl, sl] = s0_ref[0, 0, 1, h]

    def rot(x, cs, sn):
        halves = []
        for a in range(2):
            xh = x[:, 128 * a:128 * (a + 1)]
            lh = _iota(xh.shape, 1)
            halves.append(jnp.where(lh % D_HEAD < D_HEAD // 2, pltpu.roll(xh, 96, 1), pltpu.roll(xh, 32, 1)))
        return x * cs + jnp.concatenate(halves, axis=1) * sn

    def fwd(c, carry):
        r0 = pl.multiple_of(c * C, C)
        rows = pl.ds(r0, C)
        q = p_ref[rows, 0:256]
        k = p_ref[rows, 256:512] * (D_HEAD ** -0.5)
        v = p_ref[rows, 512:768]
        if rope:
            cs, sn = cos_ref[rows, :], sin_ref[rows, :]
            q, k = rot(q, cs, sn), rot(k, cs, sn)
        qs[rows, :] = q
        ks[rows, :] = k
        o = _dot(q * etab[0], sf[...])
        for h in range(N_HEADS):
            mh = jnp.where(lane // D_HEAD == h, 1.0, 0.0)
            att = _dot(q * mh, k, NT) * dmat[h]
            o = o + _dot(att, v * mh)
        oscr[rows, :] = o
        sf[...] = sf[...] * dc_f + bd * _dot(k * etab[2], v, TN)
        return carry

    lax.fori_loop(0, n_chunks, fwd, 0)

    def bwd(t, carry):
        c = n_chunks - 1 - t
        r0 = pl.multiple_of(c * C, C)
        rows = pl.ds(r0, C)
        q, k = qs[rows, :], ks[rows, :]
        v = p_ref[rows, 512:768]
        g = p_ref[rows, 768:1024]
        o = oscr[rows, :] + _dot(q * etab[1], sb[...])
        sb[...] = sb[...] * dc_b + bd * _dot(k * etab[3], v, TN)
        ms = _dot01(bd, o * o, m_left=False) * (1.0 / D_HEAD)
        o_ref[rows, :] = o * lax.rsqrt(ms + EPS) * nw_ref[...] * _silu(g)
        return carry

    lax.fori_loop(0, n_chunks, bwd, 0)

    for h in range(N_HEADS):
        sl = slice(D_HEAD * h, D_HEAD * (h + 1))
        st_ref[0, 0, 0, h] = sf[sl, sl]
        st_ref[0, 0, 1, h] = sb[sl, sl]


def _state_target(acc, bsz, n_seq, layer, tail, args, in_specs, out_index):
    if acc is None:
        shape = jax.ShapeDtypeStruct((bsz, 1, 2) + tail, F32)
        return shape, pl.BlockSpec((n_seq, 1, 2) + tail, lambda i, *_: (i, 0, 0, 0, 0, 0)), {}
    alias = {len(args): out_index}
    args.append(acc)
    in_specs.append(pl.BlockSpec(memory_space=pl.ANY))
    shape = jax.ShapeDtypeStruct(acc.shape, F32)
    return shape, pl.BlockSpec((n_seq, 1, 2) + tail, lambda i, *_: (i, layer, 0, 0, 0, 0)), alias


def _retention(p_ret, la_lane, nw, bsz, seq_len, rope_tabs, s0, layer, acc):
    n_chunks = seq_len // SCAN_CHUNK
    rope = rope_tabs is not None
    has_s0 = s0 is not None
    const = lambda a, b: pl.BlockSpec((a, b), lambda i: (0, 0))
    in_specs = [pl.BlockSpec((seq_len, 1024), lambda i: (i, 0)), const(2, W_GROUP), const(1, W_GROUP)]
    args = [p_ret, la_lane, nw]
    if rope:
        in_specs += [const(seq_len, W_GROUP)] * 2
        args += list(rope_tabs)
    if has_s0:
        in_specs.append(pl.BlockSpec((1, 1, 2, N_HEADS, D_HEAD, D_HEAD), lambda i: (i, layer, 0, 0, 0, 0)))
        args.append(s0)
    st_shape, st_spec, alias = _state_target(acc, bsz, 1, layer, (N_HEADS, D_HEAD, D_HEAD), args, in_specs, 1)
    return pl.pallas_call(
        functools.partial(_ret_kernel, n_chunks, rope, has_s0, acc is not None),
        out_shape=(jax.ShapeDtypeStruct((bsz * seq_len, W_GROUP), F32), st_shape),
        grid=(bsz,),
        in_specs=in_specs,
        out_specs=(pl.BlockSpec((seq_len, W_GROUP), lambda i: (i, 0)), st_spec),
        input_output_aliases=alias,
        scratch_shapes=[pltpu.VMEM((W_GROUP, W_GROUP), F32), pltpu.VMEM((W_GROUP, W_GROUP), F32),
                        pltpu.VMEM((N_HEADS, SCAN_CHUNK, SCAN_CHUNK), F32),
                        pltpu.VMEM((4, SCAN_CHUNK, W_GROUP), F32),
                        pltpu.VMEM((seq_len, W_GROUP), F32), pltpu.VMEM((seq_len, W_GROUP), F32),
                        pltpu.VMEM((seq_len, W_GROUP), F32)],
        compiler_params=_params(("arbitrary",)),
        name="retention",
    )(*args)


def _ssd_kernel(n_chunks, has_s0, has_acc, *refs):
    C = SCAN_CHUNK
    refs = list(refs)
    p_ref, ps_ref, cw_ref, cb_ref, dtb_ref, alog_ref, dl_ref, nw_ref = refs[:8]
    refs = refs[8:]
    if has_s0:
        s0_ref = refs[0]
        refs = refs[1:]
    if has_acc:
        refs = refs[1:]
    o_ref, st_ref, sf, sb, xs_s, bm_s, cm_s, oscr = refs
    seq_len = n_chunks * C

    def prep(c, carry):
        r0 = pl.multiple_of(c * C, C)
        y = _conv3_chunk(p_ref, r0, C, seq_len, 0, 768, cw_ref, c == 0, c == n_chunks - 1) + cb_ref[...]
        y = _silu(y)
        rows = pl.ds(r0, C)
        xs_s[rows, :] = y[:, 0:256]
        bm_s[rows, :] = y[:, 256:512]
        cm_s[rows, :] = y[:, 512:768]
        return carry

    lax.fori_loop(0, n_chunks, prep, 0)

    sf[...] = jnp.zeros_like(sf)
    sb[...] = jnp.zeros_like(sb)
    if has_s0:
        for h in range(N_HEADS):
            g = h // 2
            sf[N_SSD * g:N_SSD * (g + 1), D_HEAD * h:D_HEAD * (h + 1)] = s0_ref[0, 0, 0, h]
            sb[N_SSD * g:N_SSD * (g + 1), D_HEAD * h:D_HEAD * (h + 1)] = s0_ref[0, 0, 1, h]

    ltri, utri = _tri(C, True), _tri(C, False)
    exp_f, exp_b = _lane_expand(16, D_HEAD), _lane_expand(20, D_HEAD)
    gmask = _head_ones(W_GROUP, N_SSD)
    lane_s = _iota((1, SMALL_W), 1)
    dt_mask = jnp.where((lane_s >= 16) & (lane_s < 24), 1.0, 0.0)
    lane = _iota((1, W_GROUP), 1)
    ii, jj = _iota((C, C), 0), _iota((C, C), 1)

    def gates(rows):
        dt = _softplus(ps_ref[rows, :] + dtb_ref[...]) * dt_mask
        la = dt * (-jnp.exp(alog_ref[...]))
        return dt, la

    def fwd(c, carry):
        r0 = pl.multiple_of(c * C, C)
        rows = pl.ds(r0, C)
        dt, la = gates(rows)
        cum_p = _dot01(ltri, la)
        cum_s = _dot01(utri, la)
        cum_pt, cum_st, dt_t = cum_p.T, cum_s.T, dt.T
        x_f = _dot01(exp_f, cum_p, m_left=False)
        dt_f = _dot01(exp_f, dt, m_left=False)
        xs, bm, cm = xs_s[rows, :], bm_s[rows, :], cm_s[rows, :]
        s_g = [_dot(cm[:, N_SSD * g:N_SSD * (g + 1)], bm[:, N_SSD * g:N_SSD * (g + 1)], NT) for g in range(G_SSD)]
        o = _dot(cm, sf[...]) * jnp.exp(x_f)
        for h in range(N_HEADS):
            f, b = 16 + h, 20 + h
            dec = (jnp.exp(jnp.where(jj <= ii, cum_p[:, f:f + 1] - cum_pt[f:f + 1, :], NEG_BIG)) * dt_t[f:f + 1, :]
                   + jnp.exp(jnp.where(jj >= ii, cum_s[:, b:b + 1] - cum_st[b:b + 1, :], NEG_BIG)) * dt_t[b:b + 1, :])
            mh = jnp.where(lane // D_HEAD == h, 1.0, 0.0)
            o = o + _dot(s_g[h // 2] * dec, xs * mh)
        oscr[rows, :] = o
        last = x_f[C - 1:C, :]
        sf[...] = sf[...] * jnp.exp(last) + gmask * _dot(bm, xs * dt_f * jnp.exp(last - x_f), TN)
        return carry

    lax.fori_loop(0, n_chunks, fwd, 0)

    def bwd(t, carry):
        c = n_chunks - 1 - t
        r0 = pl.multiple_of(c * C, C)
        rows = pl.ds(r0, C)
        dt, la = gates(rows)
        cum_s = _dot01(utri, la)
        x_b = _dot01(exp_b, cum_s, m_left=False)
        dt_b = _dot01(exp_b, dt, m_left=False)
        xs, bm, cm = xs_s[rows, :], bm_s[rows, :], cm_s[rows, :]
        o = oscr[rows, :] + _dot(cm, sb[...]) * jnp.exp(x_b)
        first = x_b[0:1, :]
        sb[...] = sb[...] * jnp.exp(first) + gmask * _dot(bm, xs * dt_b * jnp.exp(first - x_b), TN)
        y = (o + xs * dl_ref[...]) * _silu(p_ref[rows, 768:1024])
        halves = [y[:, 128 * g:128 * (g + 1)] for g in range(G_SSD)]
        halves = [yh * lax.rsqrt(jnp.mean(yh * yh, axis=-1, keepdims=True) + EPS) for yh in halves]
        o_ref[rows, :] = jnp.concatenate(halves, axis=1) * nw_ref[...]
        return carry

    lax.fori_loop(0, n_chunks, bwd, 0)

    for h in range(N_HEADS):
        g = h // 2
        st_ref[0, 0, 0, h] = sf[N_SSD * g:N_SSD * (g + 1), D_HEAD * h:D_HEAD * (h + 1)]
        st_ref[0, 0, 1, h] = sb[N_SSD * g:N_SSD * (g + 1), D_HEAD * h:D_HEAD * (h + 1)]


def _ssd(p_ssd, p_small, conv_w, conv_b, dtb, alog, d_lane, nw, bsz, seq_len, s0, layer, acc):
    n_chunks = seq_len // SCAN_CHUNK
    has_s0 = s0 is not None
    const = lambda a, b: pl.BlockSpec((a, b), lambda i: (0, 0))
    in_specs = [pl.BlockSpec((seq_len, 1024), lambda i: (i, 0)), pl.BlockSpec((seq_len, SMALL_W), lambda i: (i, 0)),
                const(3, 768), const(1, 768), const(1, SMALL_W), const(1, SMALL_W), const(1, W_GROUP),
                const(1, W_GROUP)]
    args = [p_ssd, p_small, conv_w, conv_b, dtb, alog, d_lane, nw]
    if has_s0:
        in_specs.append(pl.BlockSpec((1, 1, 2, N_HEADS, N_SSD, D_HEAD), lambda i: (i, layer, 0, 0, 0, 0)))
        args.append(s0)
    st_shape, st_spec, alias = _state_target(acc, bsz, 1, layer, (N_HEADS, N_SSD, D_HEAD), args, in_specs, 1)
    return pl.pallas_call(
        functools.partial(_ssd_kernel, n_chunks, has_s0, acc is not None),
        out_shape=(jax.ShapeDtypeStruct((bsz * seq_len, W_GROUP), F32), st_shape),
        grid=(bsz,),
        in_specs=in_specs,
        out_specs=(pl.BlockSpec((seq_len, W_GROUP), lambda i: (i, 0)), st_spec),
        input_output_aliases=alias,
        scratch_shapes=[pltpu.VMEM((W_GROUP, W_GROUP), F32), pltpu.VMEM((W_GROUP, W_GROUP), F32)]
        + [pltpu.VMEM((seq_len, W_GROUP), F32)] * 4,
        compiler_params=_params(("arbitrary",)),
        name="ssd",
    )(*args)


def _gdn_prep_kernel(n_pre, p_ref, cw_ref, q_ref, k_ref, v_ref):
    CP = SCAN_CHUNK
    bd = _head_ones()

    def prep(c, carry):
        r0 = pl.multiple_of(c * CP, CP)
        y = _silu(_conv3_chunk(p_ref, r0, CP, n_pre * CP, 0, 768, cw_ref, c == 0, c == n_pre - 1))
        q, k, v = y[:, 0:256], y[:, 256:512], y[:, 512:768]
        rows = pl.ds(r0, CP)
        q_ref[rows, :] = q * lax.rsqrt(_dot01(bd, q * q, m_left=False) + EPS) * (D_HEAD ** -0.5)
        k_ref[rows, :] = k * lax.rsqrt(_dot01(bd, k * k, m_left=False) + EPS)
        v_ref[rows, :] = v
        return carry

    lax.fori_loop(0, n_pre, prep, 0)


def _gdn_scan_kernel(n_seq, n_blk, has_s0, has_acc, *refs):
    C = GDN_CHUNK
    R = N_HEADS * C
    SUB = SCAN_CHUNK // C
    refs = list(refs)
    qf, kf, vf, gf, qb, kb, vb, gb, dtb_ref, alog_ref = refs[:10]
    refs = refs[10:]
    if has_s0:
        s0_ref = refs[0]
        refs = refs[1:]
    if has_acc:
        refs = refs[1:]
    of_ref, ob_ref, st_ref, sf, sb = refs
    t = pl.program_id(1)

    @pl.when(t == 0)
    def _():
        sf[...] = jnp.zeros_like(sf)
        sb[...] = jnp.zeros_like(sb)
        if has_s0:
            for j in range(n_seq):
                for h in range(N_HEADS):
                    sl = slice(D_HEAD * h, D_HEAD * (h + 1))
                    sf[j, sl, sl] = s0_ref[j, 0, 0, h]
                    sb[j, sl, sl] = s0_ref[j, 0, 1, h]

    lane_s = _iota((1, SMALL_W), 1)
    la_mask = jnp.where(lane_s < 8, 1.0, 0.0)
    same = (_iota((R, R), 0) // C) == (_iota((R, R), 1) // C)
    lane_head = _iota((1, W_GROUP), 1) // D_HEAD
    pad_rows = jnp.zeros((SMALL_W - C, SMALL_W), F32)
    ic = _iota((C, R), 0)
    jc = _iota((C, R), 1) % C
    zero_b = jnp.zeros((R, R), BF16)

    def expand(x):
        xb = x.astype(BF16)
        return jnp.where(same, jnp.concatenate([xb] * N_HEADS, axis=0), zero_b)

    def per_lane(x, lane0):
        cols = [jnp.broadcast_to(x[:, lane0 + h:lane0 + h + 1], (C, W_GROUP)) for h in range(N_HEADS)]
        return jnp.where(lane_head == 0, cols[0],
                         jnp.where(lane_head == 1, cols[1], jnp.where(lane_head == 2, cols[2], cols[3])))

    def chunks(streams):
        seqs, offs, dirs = zip(*streams)
        fwd = [d == 0 for d in dirs]
        base = [4 * d for d in dirs]
        rows = [pl.ds(pl.multiple_of(r0, C), C) for r0 in offs]
        pick = lambda a_f, a_b: [(a_f if f else a_b) for f in fwd]
        q_refs, k_refs, v_refs, g_refs = pick(qf, qb), pick(kf, kb), pick(vf, vb), pick(gf, gb)
        s_refs, o_refs = pick(sf, sb), pick(of_ref, ob_ref)
        g = [r[j, rw, :] for r, j, rw in zip(g_refs, seqs, rows)]
        la = [-jnp.exp(alog_ref[...]) * _softplus(x + dtb_ref[...]) * la_mask for x in g]
        beta = [_sigmoid(x) for x in g]
        cum = [_dot01(_tri(C, f), x) for f, x in zip(fwd, la)]
        last_row = [x[C - 1:C, :] if f else x[0:1, :] for f, x in zip(fwd, cum)]
        cum_t = [jnp.concatenate([x, pad_rows], axis=0).T for x in cum]
        cum_row = []
        for x, b in zip(cum_t, base):
            pc = [x[b + h:b + h + 1, :] for h in range(N_HEADS)]
            cum_row.append(jnp.concatenate([pc[0] + pltpu.roll(pc[1], C, 1), pc[2] + pltpu.roll(pc[3], C, 1)],
                                           axis=1))
        cum_n = [per_lane(x, b) for x, b in zip(cum, base)]
        beta_n = [per_lane(x, 8 + b) for x, b in zip(beta, base)]
        last_lane = []
        for l, b in zip(last_row, base):
            ll = jnp.zeros((1, W_GROUP), F32)
            for h in range(N_HEADS):
                ll = jnp.where(lane_head == h, l[:, b + h:b + h + 1], ll)
            last_lane.append(ll)
        ecum_n = [jnp.exp(x) for x in cum_n]
        kdec_n = [jnp.exp(l - x) for l, x in zip(last_lane, cum_n)]
        incl = [(jc <= ic) if f else (jc >= ic) for f in fwd]
        strict = [(jc < ic) if f else (jc > ic) for f in fwd]
        dec = [jnp.exp(jnp.where(m, x - y, NEG_BIG)) for m, x, y in zip(incl, cum_n, cum_row)]
        q = [r[j, rw, :] for r, j, rw in zip(q_refs, seqs, rows)]
        k = [r[j, rw, :] for r, j, rw in zip(k_refs, seqs, rows)]
        v = [r[j, rw, :] for r, j, rw in zip(v_refs, seqs, rows)]
        k_x = [expand(x) for x in k]
        kk = [_dot(x, y, NT) for x, y in zip(k, k_x)]
        qk = [_dot(x, y, NT) for x, y in zip(q, k_x)]
        a = [jnp.where(m, d * x, 0.0) * b for m, d, x, b in zip(strict, dec, kk, beta_n)]
        n = [-jnp.where((ic // 2) == (jc // 2), x, 0.0) for x in a]
        s = 2
        while s < C:
            blk = ((ic // (2 * s)) == (jc // (2 * s))) & ((ic // s) != (jc // s))
            lo = [jnp.where(blk, x, 0.0) for x in a]
            y = [x + _dot(x, expand(m)) for x, m in zip(lo, n)]
            n = [m - (x + _dot(m, expand(x))) for m, x in zip(n, y)]
            s *= 2
        kp = [x * (b * e) for x, b, e in zip(k, beta_n, ecum_n)]
        vp = [x * b for x, b in zip(v, beta_n)]
        wm = [x + _dot(m, expand(x)) for m, x in zip(n, kp)]
        um = [x + _dot(m, expand(x)) for m, x in zip(n, vp)]
        st = [r[j] for r, j in zip(s_refs, seqs)]
        v_new = [u - _dot(w_, s_) for u, w_, s_ in zip(um, wm, st)]
        o = [_dot(x * e, s_) + _dot(y * d, expand(vn))
             for x, e, s_, y, d, vn in zip(q, ecum_n, st, qk, dec, v_new)]
        for j, rw, o_r, x in zip(seqs, rows, o_refs, o):
            o_r[j, rw, :] = x
        for s_r, j, s_, l, x, kd, vn in zip(s_refs, seqs, st, last_lane, k, kdec_n, v_new):
            s_r[j] = s_ * jnp.exp(l) + jnp.where(same, _dot(x * kd, vn, TN), 0.0)

    def scan_body(u, carry):
        streams = []
        for j in range(n_seq):
            streams += [(j, u * C, 0), (j, (SUB - 1 - u) * C, 1)]
        chunks(streams)
        return carry

    lax.fori_loop(0, SUB, scan_body, 0)

    @pl.when(t == n_blk - 1)
    def _():
        for j in range(n_seq):
            for h in range(N_HEADS):
                sl = slice(D_HEAD * h, D_HEAD * (h + 1))
                st_ref[j, 0, 0, h] = sf[j, sl, sl]
                st_ref[j, 0, 1, h] = sb[j, sl, sl]


def _gdn(p_gdn, p_small, conv_w, dtb, alog, bsz, seq_len, s0, layer, acc):
    n_tok = bsz * seq_len
    n_blk = seq_len // SCAN_CHUNK
    n_seq = GDN_SEQS
    has_s0 = s0 is not None
    tok = jax.ShapeDtypeStruct((n_tok, W_GROUP), F32)

    q, k, v = pl.pallas_call(
        functools.partial(_gdn_prep_kernel, n_blk),
        out_shape=(tok, tok, tok),
        grid=(bsz,),
        in_specs=[pl.BlockSpec((seq_len, 768), lambda i: (i, 0)), pl.BlockSpec((3, 768), lambda i: (0, 0))],
        out_specs=(pl.BlockSpec((seq_len, W_GROUP), lambda i: (i, 0)),) * 3,
        compiler_params=_params(("arbitrary",)),
        name="gated_delta_prep",
    )(p_gdn, conv_w)

    seq3 = lambda a: a.reshape(bsz, seq_len, a.shape[-1])
    fwd_blk = lambda w: pl.BlockSpec((n_seq, SCAN_CHUNK, w), lambda i, t: (i, t, 0))
    bwd_blk = lambda w: pl.BlockSpec((n_seq, SCAN_CHUNK, w), lambda i, t: (i, n_blk - 1 - t, 0))
    const = lambda a, b: pl.BlockSpec((a, b), lambda i, t: (0, 0))
    q3, k3, v3, g3 = seq3(q), seq3(k), seq3(v), seq3(p_small)
    in_specs = [fwd_blk(W_GROUP)] * 3 + [fwd_blk(SMALL_W)] + [bwd_blk(W_GROUP)] * 3 + [bwd_blk(SMALL_W)] \
        + [const(1, SMALL_W), const(1, SMALL_W)]
    args = [q3, k3, v3, g3, q3, k3, v3, g3, dtb, alog]
    if has_s0:
        in_specs.append(pl.BlockSpec((n_seq, 1, 2, N_HEADS, D_HEAD, D_HEAD), lambda i, t: (i, layer, 0, 0, 0, 0)))
        args.append(s0)
    seq_out = jax.ShapeDtypeStruct((bsz, seq_len, W_GROUP), F32)
    st_shape, st_spec, alias = _state_target(acc, bsz, n_seq, layer, (N_HEADS, D_HEAD, D_HEAD), args, in_specs, 2)
    o_f, o_b, state = pl.pallas_call(
        functools.partial(_gdn_scan_kernel, n_seq, n_blk, has_s0, acc is not None),
        out_shape=(seq_out, seq_out, st_shape),
        grid=(bsz // n_seq, n_blk),
        in_specs=in_specs,
        out_specs=(fwd_blk(W_GROUP), bwd_blk(W_GROUP), st_spec),
        input_output_aliases=alias,
        scratch_shapes=[pltpu.VMEM((n_seq, W_GROUP, W_GROUP), F32), pltpu.VMEM((n_seq, W_GROUP, W_GROUP), F32)],
        compiler_params=_params(("arbitrary", "arbitrary")),
        name="gated_delta_scan",
    )(*args)
    return o_f.reshape(n_tok, W_GROUP), o_b.reshape(n_tok, W_GROUP), state


def _dft_tables(seq_len):
    period = 8 * seq_len
    theta = math.pi / (4 * seq_len)
    fine, coarse = 64, seq_len // 64
    odd = 2 * jnp.arange(seq_len, dtype=jnp.int32) + 1
    ka = ((2 * fine * jnp.arange(coarse, dtype=jnp.int32))[:, None] * odd[None, :]) % period
    kb = (odd[:fine, None] * odd[None, :]) % period
    ang_a = ka.astype(F32)[:, None, :] * theta
    ang_b = kb.astype(F32)[None, :, :] * theta
    ca, sa, cb, sb = jnp.cos(ang_a), jnp.sin(ang_a), jnp.cos(ang_b), jnp.sin(ang_b)
    c2 = (ca * cb - sa * sb).reshape(seq_len, seq_len).astype(BF16)
    s2 = (sa * cb + ca * sb).reshape(seq_len, seq_len).astype(BF16)
    half = odd.astype(F32) * theta
    return c2, s2, jnp.stack([jnp.cos(half), jnp.sin(half)], axis=1)


def _filter_features(seq_len):
    t = jnp.linspace(0.0, 1.0, seq_len, dtype=F32)[:, None]
    bands = (HY_EMB - 1) // 2
    w = 2.0 * math.pi * jnp.arange(seq_len, dtype=F32)[:, None] / seq_len
    f = jnp.linspace(1e-4, bands - 1, bands, dtype=F32)[None, :]
    z = jnp.concatenate([t, jnp.cos(f * w), -jnp.sin(f * w)], axis=-1)
    z = jnp.pad(z, ((0, 0), (0, 128 - HY_EMB)))
    deltas = jnp.abs(jnp.linspace(math.log(HY_DECAY_TARGET) / HY_SLOW_PCT,
                                  math.log(HY_DECAY_TARGET) / HY_FAST_PCT, W_GROUP, dtype=F32))
    return z, jnp.exp(-t * deltas)


def _hyena_kernel(n_chunks, p_ref, cw_ref, cb_ref, bias_ref, nw_ref, z_ref, dec_ref, fr_ref, w1_ref, b1_ref,
                  w2_ref, b2_ref, w3_ref, c2_ref, s2_ref, ph_ref, o_ref,
                  hre_s, him_s, x0_s, u_s, ub_s, a_s, b_s, yre_s, yim_s):
    C = SCAN_CHUNK
    seq_len = n_chunks * C

    @pl.when(pl.program_id(0) == 0)
    def _():
        fr = fr_ref[0]

        def taps_chunk(c, carry):
            r0 = pl.multiple_of(c * C, C)
            rows = pl.ds(r0, C)
            h = jnp.sin(fr * (_dot_hp(z_ref[rows, :], w1_ref[0]) + b1_ref[0]))
            h = jnp.sin(fr * (_dot_hp(h, w2_ref[0]) + b2_ref[0]))
            taps = _dot_hp(h, w3_ref[0])
            hf = taps[:, 0:W_GROUP] * dec_ref[rows, :]
            hb = taps[:, W_GROUP:2 * W_GROUP] * dec_ref[rows, :]
            hb = jnp.where(_iota(hb.shape, 0) + r0 == 0, 0.0, hb)
            yre_s[rows, :] = (hf + hb).astype(BF16)
            yim_s[rows, :] = (hf - hb).astype(BF16)
            return carry

        lax.fori_loop(0, n_chunks, taps_chunk, 0)
        a_s[...] = _dot(c2_ref[...], yre_s[...])
        b_s[...] = _dot(s2_ref[...], yre_s[...])

        def hre_chunk(c, carry):
            rows = pl.ds(pl.multiple_of(c * C, C), C)
            hre_s[rows, :] = ph_ref[rows, 0:1] * a_s[rows, :] + ph_ref[rows, 1:2] * b_s[rows, :]
            return carry

        lax.fori_loop(0, n_chunks, hre_chunk, 0)
        a_s[...] = _dot(c2_ref[...], yim_s[...])
        b_s[...] = _dot(s2_ref[...], yim_s[...])

        def him_chunk(c, carry):
            rows = pl.ds(pl.multiple_of(c * C, C), C)
            him_s[rows, :] = ph_ref[rows, 1:2] * a_s[rows, :] - ph_ref[rows, 0:1] * b_s[rows, :]
            return carry

        lax.fori_loop(0, n_chunks, him_chunk, 0)

    def prep(c, carry):
        r0 = pl.multiple_of(c * C, C)
        y = _conv3_chunk(p_ref, r0, C, seq_len, 0, 768, cw_ref, c == 0, c == n_chunks - 1) + cb_ref[...]
        rows = pl.ds(r0, C)
        u = y[:, 512:768] * y[:, 256:512]
        x0_s[rows, :] = y[:, 0:256]
        u_s[rows, :] = u
        ub_s[rows, :] = u.astype(BF16)
        return carry

    lax.fori_loop(0, n_chunks, prep, 0)

    a_s[...] = _dot(c2_ref[...], ub_s[...])
    b_s[...] = _dot(s2_ref[...], ub_s[...])

    def spectrum(c, carry):
        rows = pl.ds(pl.multiple_of(c * C, C), C)
        ure, su, hre, him = a_s[rows, :], b_s[rows, :], hre_s[rows, :], him_s[rows, :]
        yre_s[rows, :] = (ure * hre + su * him).astype(BF16)
        yim_s[rows, :] = (ure * him - su * hre).astype(BF16)
        return carry

    lax.fori_loop(0, n_chunks, spectrum, 0)
    a_s[...] = _dot(c2_ref[...], yre_s[...])
    b_s[...] = _dot(s2_ref[...], yim_s[...])

    def fin(c, carry):
        rows = pl.ds(pl.multiple_of(c * C, C), C)
        conv = (a_s[rows, :] - b_s[rows, :]) * (1.0 / seq_len)
        y = x0_s[rows, :] * (conv + u_s[rows, :] * bias_ref[...])
        o_ref[rows, :] = _rms(y, nw_ref[...])
        return carry

    lax.fori_loop(0, n_chunks, fin, 0)


def _hyena(p_hy, conv_w, conv_b, bias, nw, tabs, feats, freq, w1, b1, w2, b2, w3, bsz, seq_len, layer):
    c2, s2, phase = tabs
    z, dec = feats
    one = dict(pipeline_mode=pl.Buffered(1)) if seq_len > 1024 else {}
    const = lambda a, b: pl.BlockSpec((a, b), lambda i: (0, 0), **one)
    lay = lambda a, b: pl.BlockSpec((1, a, b), lambda i: (layer, 0, 0))
    acc = pltpu.VMEM((seq_len, W_GROUP), F32)
    half = pltpu.VMEM((seq_len, W_GROUP), BF16)
    return pl.pallas_call(
        functools.partial(_hyena_kernel, seq_len // SCAN_CHUNK),
        out_shape=jax.ShapeDtypeStruct((bsz * seq_len, W_GROUP), F32),
        grid=(bsz,),
        in_specs=[pl.BlockSpec((seq_len, 768), lambda i: (i, 0), **one), const(3, 768), const(1, 768),
                  const(1, W_GROUP), const(1, W_GROUP), const(seq_len, 128), const(seq_len, W_GROUP),
                  lay(1, HY_HIDDEN), lay(128, HY_HIDDEN), lay(1, HY_HIDDEN), lay(HY_HIDDEN, HY_HIDDEN),
                  lay(1, HY_HIDDEN), lay(HY_HIDDEN, 2 * W_GROUP),
                  const(seq_len, seq_len), const(seq_len, seq_len), const(seq_len, 2)],
        out_specs=pl.BlockSpec((seq_len, W_GROUP), lambda i: (i, 0)),
        scratch_shapes=[acc, acc, acc, acc, half, acc, acc, half, half],
        compiler_params=_params(("arbitrary",)),
        name="hyena",
    )(p_hy, conv_w, conv_b, bias, nw, z, dec, freq, w1, b1, w2, b2, w3, c2, s2, phase)


def _rope_tables(seq_len):
    rows = seq_len // GRID_W
    r = jnp.repeat(jnp.arange(rows), GRID_W).astype(F32)
    col = jnp.tile(jnp.arange(GRID_W), rows).astype(F32)
    quarter = D_HEAD // 4
    inv = ROPE_BASE ** (-jnp.arange(quarter, dtype=F32) / quarter)
    ang = jnp.concatenate([r[:, None] * inv, col[:, None] * inv], axis=-1)
    cos, sin = jnp.cos(ang), jnp.sin(ang)
    cos_full = jnp.tile(jnp.concatenate([cos, cos], axis=-1), (1, N_HEADS))
    sin_signed = jnp.tile(jnp.concatenate([-sin, sin], axis=-1), (1, N_HEADS))
    return cos_full, sin_signed


def _small_vec(first_lane, vals):
    flat = vals.reshape(-1)
    return jnp.zeros((1, SMALL_W), F32).at[0, first_lane:first_lane + flat.shape[0]].set(flat)


def _trunk(x, bsz, seq_len, row_fn, states, finals, rope_tabs, hy_tabs, hy_feats, mod3, w):
    st_ret, st_gdn, st_ssd = states
    f_ret, f_gdn, f_ssd = finals
    for l in range(DEPTH):
        p_ret, p_hy, p_gdn, p_ssd, p_small = _in_proj(x, mod3, w['norm1'], w['w_main'], w['w_ssd'],
                                                      w['w_small'], l, row_fn)
        o_ret, s_ret = _retention(p_ret, w['ret_la'][l], w['ret_norm'][l], bsz, seq_len, rope_tabs, st_ret, l,
                                  f_ret)
        o_hy = _hyena(p_hy, w['hy_conv_w'][l], w['hy_conv_b'][l], w['hy_bias'][l], w['hy_norm'][l],
                      hy_tabs, hy_feats, w['hy_freq'], w['hy_w1'], w['hy_b1'], w['hy_w2'], w['hy_b2'],
                      w['hy_w3'], bsz, seq_len, l)
        og_f, og_b, s_gdn = _gdn(p_gdn, p_small, w['gdn_conv_w'][l], w['gdn_dtb'][l], w['gdn_alog'][l],
                                 bsz, seq_len, st_gdn, l, f_gdn)
        o_ssd, s_ssd = _ssd(p_ssd, p_small, w['ssd_conv_w'][l], w['ssd_conv_b'][l], w['ssd_dtb'][l],
                            w['ssd_alog'][l], w['ssd_dl'][l], w['ssd_norm'][l], bsz, seq_len, st_ssd, l, f_ssd)
        x = _out_proj((o_ret, o_hy, og_f, og_b, p_gdn, w['gdn_norm'][l], o_ssd), x, mod3, w['norm2'],
                      w['w_out'], w['mlp_w1'], w['mlp_w2'], w['final_norm'], l, row_fn, l == DEPTH - 1)
        if f_ret is not None:
            f_ret, f_gdn, f_ssd = s_ret, s_gdn, s_ssd
    return x, (f_ret, f_gdn, f_ssd)


def kernel(x_prompt, x_sample, state_ret, state_gdn, state_ssd, c, c_ctx, norm1_w, norm2_w, w_mod, b_mod, w_in, w_out, ret_log_decay, ret_norm_w, hy_conv_w, hy_conv_b, hy_freq, hy_w1, hy_b1, hy_w2, hy_b2, hy_w3, hy_bias, hy_norm_w, gdn_conv_w, gdn_A_log, gdn_dt_bias, gdn_norm_w, ssd_conv_w, ssd_conv_b, ssd_A_log, ssd_dt_bias, ssd_D, ssd_norm_w, mlp_w1, mlp_w2, final_norm_w):
    bsz, seq, _ = x_prompt.shape
    dbsz, dseq, _ = x_sample.shape

    cond8 = jnp.concatenate([c, c_ctx[None, :], jnp.zeros((MOD_ROWS - dbsz - 1, D_MODEL), F32)], axis=0)
    mod = _modulation(cond8, w_mod, b_mod)
    mod3 = mod.reshape(DEPTH * MOD_ROWS * 6, 1, D_MODEL)

    w_main = w_in.astype(BF16)
    w_ssd = w_main[:, :, 2832:3856]
    w_small = jnp.concatenate([w_main[:, :, 2816:2832], w_main[:, :, 3856:3864],
                               jnp.zeros((DEPTH, D_MODEL, SMALL_W - 24), BF16)], axis=-1)
    w = dict(
        norm1=norm1_w.reshape(DEPTH, 1, D_MODEL), norm2=norm2_w.reshape(DEPTH, 1, D_MODEL),
        w_main=w_main, w_ssd=w_ssd, w_small=w_small, w_out=w_out.astype(BF16),
        mlp_w1=mlp_w1.astype(BF16), mlp_w2=mlp_w2.astype(BF16), final_norm=final_norm_w,
        ret_la=[jnp.repeat(ret_log_decay[l], D_HEAD, axis=-1) for l in range(DEPTH)],
        ret_norm=[ret_norm_w[l][None, :] for l in range(DEPTH)],
        hy_freq=hy_freq.reshape(DEPTH, 1, HY_HIDDEN),
        hy_w1=jnp.pad(hy_w1, ((0, 0), (0, 128 - HY_EMB), (0, 0))),
        hy_b1=hy_b1.reshape(DEPTH, 1, HY_HIDDEN), hy_w2=hy_w2, hy_b2=hy_b2.reshape(DEPTH, 1, HY_HIDDEN),
        hy_w3=hy_w3,
        hy_conv_w=hy_conv_w, hy_conv_b=[hy_conv_b[l][None, :] for l in range(DEPTH)],
        hy_bias=[hy_bias[l][None, :] for l in range(DEPTH)], hy_norm=[hy_norm_w[l][None, :] for l in range(DEPTH)],
        gdn_conv_w=gdn_conv_w,
        gdn_dtb=[_small_vec(0, gdn_dt_bias[l]) for l in range(DEPTH)],
        gdn_alog=[_small_vec(0, gdn_A_log[l]) for l in range(DEPTH)],
        gdn_norm=[gdn_norm_w[l][None, :] for l in range(DEPTH)],
        ssd_conv_w=ssd_conv_w, ssd_conv_b=[ssd_conv_b[l][None, :] for l in range(DEPTH)],
        ssd_dtb=[_small_vec(16, ssd_dt_bias[l]) for l in range(DEPTH)],
        ssd_alog=[_small_vec(16, ssd_A_log[l]) for l in range(DEPTH)],
        ssd_dl=[jnp.repeat(ssd_D[l], D_HEAD)[None, :] for l in range(DEPTH)],
        ssd_norm=[ssd_norm_w[l][None, :] for l in range(DEPTH)],
    )

    finals = (jnp.zeros((bsz, DEPTH, 2, N_HEADS, D_HEAD, D_HEAD), F32),
              jnp.zeros((bsz, DEPTH, 2, N_HEADS, D_HEAD, D_HEAD), F32),
              jnp.zeros((bsz, DEPTH, 2, N_HEADS, N_SSD, D_HEAD), F32))
    y_p, (new_ret, new_gdn, new_ssd) = _trunk(
        x_prompt.reshape(bsz * seq, D_MODEL), bsz, seq, lambda i: CTX_MOD_ROW, (None, None, None), finals,
        None, _dft_tables(seq), _filter_features(seq), mod3, w)

    tiles_per_seq = dseq // TOK_TILE
    y_s, _ = _trunk(x_sample.reshape(dbsz * dseq, D_MODEL), dbsz, dseq, lambda i: i // tiles_per_seq,
                    (state_ret, state_gdn, state_ssd), (None, None, None), _rope_tables(dseq),
                    _dft_tables(dseq), _filter_features(dseq), mod3, w)

    return (y_p.reshape(bsz, seq, D_MODEL), y_s.reshape(dbsz, dseq, D_MODEL), new_ret, new_gdn, new_ssd)
```

```python
import functools
import math

import jax
import jax.numpy as jnp
from jax import lax
from jax.experimental import pallas as pl
from jax.experimental.pallas import tpu as pltpu

F32 = jnp.float32
BF16 = jnp.bfloat16

D_MODEL = 1024
DEPTH = 4
GRID_W = 64
W_GROUP = 256
N_HEADS = 4
D_HEAD = 64
N_SSD = 128
G_SSD = 2
HY_EMB = 33
HY_HIDDEN = 64
HY_DECAY_TARGET = 1e-2
HY_FAST_PCT = 0.3
HY_SLOW_PCT = 1.5
D_FF = 4 * D_MODEL
ROPE_BASE = 10000.0
EPS = 1e-6
SMALL_W = 128
MOD_ROWS = 8
CTX_MOD_ROW = 4

TOK_TILE = 512
OUT_TILE = 512
SCAN_CHUNK = 256
GDN_CHUNK = 64
NEG_BIG = -1e30
GDN_SEQS_MAX = 4
VMEM_LIMIT = 56 * 1024 * 1024

NN = (((1,), (0,)), ((), ()))
NT = (((1,), (1,)), ((), ()))
TN = (((0,), (0,)), ((), ()))


def _dot(a, b, dims=NN):
    return lax.dot_general(a.astype(BF16), b.astype(BF16), dims, preferred_element_type=F32)


def _dot_hp(a, b, dims=NN):
    a_hi, a_lo = _split(a)
    b_hi, b_lo = _split(b)
    return _dot3(a_hi, a_lo, b_hi, b_lo, dims)


def _split(x):
    hi = x.astype(BF16)
    lo = (x - hi.astype(F32)).astype(BF16)
    return hi, lo


def _dot3(a_hi, a_lo, b_hi, b_lo, dims=NN):
    d = functools.partial(lax.dot_general, dimension_numbers=dims, preferred_element_type=F32)
    return d(a_hi, b_hi) + d(a_hi, b_lo) + d(a_lo, b_hi)


def _dot01(m, x, terms=3, m_left=True):
    mb = m.astype(BF16)
    acc, r = None, x.astype(F32)
    for _ in range(terms):
        t = r.astype(BF16)
        p = lax.dot_general(mb, t, NN, preferred_element_type=F32) if m_left else \
            lax.dot_general(t, mb, NN, preferred_element_type=F32)
        acc = p if acc is None else acc + p
        r = r - t.astype(F32)
    return acc


def _sigmoid(x):
    return 1.0 / (1.0 + jnp.exp(-x))


def _silu(x):
    return x * _sigmoid(x)


def _softplus(x):
    return jnp.maximum(x, 0.0) + jnp.log1p(jnp.exp(-jnp.abs(x)))


def _iota(shape, dim):
    return lax.broadcasted_iota(jnp.int32, shape, dim)


def _params(sem):
    return pltpu.CompilerParams(dimension_semantics=sem, vmem_limit_bytes=VMEM_LIMIT)


def _head_ones(n=W_GROUP, width=D_HEAD):
    return jnp.where(_iota((n, n), 0) // width == _iota((n, n), 1) // width, 1.0, 0.0).astype(F32)


def _mod_kernel(cond_ref, w_ref, b_ref, o_ref):
    s = _silu(cond_ref[...])
    o_ref[0] = _dot(s, w_ref[0]) + b_ref[0]


def _modulation(cond8, w_mod, b_mod):
    tn = 1536
    n_out = 6 * D_MODEL
    return pl.pallas_call(
        _mod_kernel,
        out_shape=jax.ShapeDtypeStruct((DEPTH, MOD_ROWS, n_out), F32),
        grid=(DEPTH, n_out // tn),
        in_specs=[
            pl.BlockSpec((MOD_ROWS, D_MODEL), lambda l, j: (0, 0)),
            pl.BlockSpec((1, D_MODEL, tn), lambda l, j: (l, 0, j)),
            pl.BlockSpec((1, 1, tn), lambda l, j: (l, 0, j)),
        ],
        out_specs=pl.BlockSpec((1, MOD_ROWS, tn), lambda l, j: (l, 0, j)),
        compiler_params=_params(("arbitrary", "arbitrary")),
        name="modulation",
    )(cond8, w_mod, b_mod.reshape(DEPTH, 1, n_out))


def _mod_spec(layer, k, row_fn, tile):
    return pl.BlockSpec((1, 1, D_MODEL), lambda i: ((layer * MOD_ROWS + row_fn(i * tile)) * 6 + k, 0, 0))


def _rms(x, w):
    return x * lax.rsqrt(jnp.mean(x * x, axis=-1, keepdims=True) + EPS) * w


def _in_kernel(x_ref, nw_ref, sh_ref, sc_ref, wm_ref, wd_ref, ws_ref, pret, phy, pgdn, pssd, psm):
    h = _rms(x_ref[...], nw_ref[0]) * (1.0 + sc_ref[0]) + sh_ref[0]
    hb = h.astype(BF16)
    d = functools.partial(jnp.dot, preferred_element_type=F32)
    pret[...] = d(hb, wm_ref[0, :, 0:1024])
    phy[...] = d(hb, wm_ref[0, :, 1024:1792])
    pgdn[...] = d(hb, wm_ref[0, :, 1792:2816])
    pssd[...] = d(hb, wd_ref[0])
    psm[...] = d(hb, ws_ref[0])


def _in_proj(x, mod3, norm1_w3, w_main, w_ssd, w_small, layer, row_fn):
    n = x.shape[0]
    widths = (1024, 768, 1024, 1024, SMALL_W)
    return pl.pallas_call(
        _in_kernel,
        out_shape=tuple(jax.ShapeDtypeStruct((n, w), F32) for w in widths),
        grid=(n // TOK_TILE,),
        in_specs=[
            pl.BlockSpec((TOK_TILE, D_MODEL), lambda i: (i, 0)),
            pl.BlockSpec((1, 1, D_MODEL), lambda i: (layer, 0, 0)),
            _mod_spec(layer, 0, row_fn, TOK_TILE),
            _mod_spec(layer, 1, row_fn, TOK_TILE),
            pl.BlockSpec((1, D_MODEL, w_main.shape[-1]), lambda i: (layer, 0, 0), pipeline_mode=pl.Buffered(1)),
            pl.BlockSpec((1, D_MODEL, 1024), lambda i: (layer, 0, 0), pipeline_mode=pl.Buffered(1)),
            pl.BlockSpec((1, D_MODEL, SMALL_W), lambda i: (layer, 0, 0), pipeline_mode=pl.Buffered(1)),
        ],
        out_specs=tuple(pl.BlockSpec((TOK_TILE, w), lambda i: (i, 0)) for w in widths),
        compiler_params=_params(("arbitrary",)),
        name="in_proj",
    )(x, norm1_w3, mod3, mod3, w_main, w_ssd, w_small)


def _out_kernel(final, oret, ohy, ogf, ogb, zg, gnw, ossd, x_ref, g1, sh2, sc2, g2, nw2, wout, w1, w2, *rest):
    if final:
        fnw, y_ref = rest
    else:
        (y_ref,) = rest
    acc = _dot(oret[...], wout[0, 0:256, :])
    acc = acc + _dot(ohy[...], wout[0, 256:512, :])
    acc = acc + _dot(ossd[...], wout[0, 768:1024, :])
    og = ogf[...] + ogb[...]
    ms = _dot01(_head_ones(), og * og, m_left=False) * (1.0 / D_HEAD)
    ogdn = og * lax.rsqrt(ms + EPS) * gnw[...] * _silu(zg[...])
    acc = acc + _dot(ogdn, wout[0, 512:768, :])
    x1 = x_ref[...] + g1[0] * acc
    h2 = _rms(x1, nw2[0]) * (1.0 + sc2[0]) + sh2[0]
    a = _dot(h2, w1[0])
    a = jnp.square(jnp.maximum(a, 0.0))
    x2 = x1 + g2[0] * _dot(a, w2[0])
    if final:
        x2 = _rms(x2, fnw[...])
    y_ref[...] = x2


def _out_proj(mix, x, mod3, norm2_w3, w_out, w1, w2, final_norm_w, layer, row_fn, final):
    n = x.shape[0]
    tok = lambda w: pl.BlockSpec((OUT_TILE, w), lambda i: (i, 0))
    lay = lambda a, b: pl.BlockSpec((1, a, b), lambda i: (layer, 0, 0), pipeline_mode=pl.Buffered(1))
    in_specs = [tok(W_GROUP)] * 4 + [
        pl.BlockSpec((OUT_TILE, W_GROUP), lambda i: (i, 3)), pl.BlockSpec((1, W_GROUP), lambda i: (0, 0)),
        tok(W_GROUP),
        tok(D_MODEL),
        _mod_spec(layer, 2, row_fn, OUT_TILE), _mod_spec(layer, 3, row_fn, OUT_TILE),
        _mod_spec(layer, 4, row_fn, OUT_TILE), _mod_spec(layer, 5, row_fn, OUT_TILE),
        lay(1, D_MODEL), lay(D_MODEL, D_MODEL), lay(D_MODEL, D_FF), lay(D_FF, D_MODEL),
    ]
    args = list(mix) + [x, mod3, mod3, mod3, mod3, norm2_w3, w_out, w1, w2]
    if final:
        in_specs.append(pl.BlockSpec((1, D_MODEL), lambda i: (0, 0)))
        args.append(final_norm_w.reshape(1, D_MODEL))
    return pl.pallas_call(
        functools.partial(_out_kernel, final),
        out_shape=jax.ShapeDtypeStruct((n, D_MODEL), F32),
        grid=(n // OUT_TILE,),
        in_specs=in_specs,
        out_specs=tok(D_MODEL),
        compiler_params=_params(("arbitrary",)),
        name="out_proj_mlp",
    )(*args)


def _conv3_chunk(p_ref, r0, rows, seq_len, c0, width, w_ref, first, last):
    x = p_ref[pl.ds(r0, rows), c0:c0 + width]
    prev_i = jnp.maximum(r0 - 1, 0)
    next_i = jnp.minimum(r0 + rows, seq_len - 1)
    xp = p_ref[pl.ds(prev_i, 1), c0:c0 + width] * jnp.where(first, 0.0, 1.0)
    xn = p_ref[pl.ds(next_i, 1), c0:c0 + width] * jnp.where(last, 0.0, 1.0)
    row = _iota((rows, width), 0)
    x_prev = jnp.where(row == 0, xp, pltpu.roll(x, 1, 0))
    x_next = jnp.where(row == rows - 1, xn, pltpu.roll(x, rows - 1, 0))
    return x_prev * w_ref[0:1, :] + x * w_ref[1:2, :] + x_next * w_ref[2:3, :]


def _tri(n, lower):
    i, j = _iota((n, n), 0), _iota((n, n), 1)
    return jnp.where((j <= i) if lower else (j >= i), 1.0, 0.0).astype(F32)


def _lane_expand(first_row, width):
    r, c = _iota((SMALL_W, W_GROUP), 0), _iota((SMALL_W, W_GROUP), 1)
    return jnp.where(r == first_row + c // width, 1.0, 0.0).astype(F32)


def _ret_kernel(n_chunks, rope, has_s0, has_acc, *refs):
    C = SCAN_CHUNK
    refs = list(refs)
    p_ref, la_ref, nw_ref = refs[:3]
    refs = refs[3:]
    if rope:
        cos_ref, sin_ref = refs[:2]
        refs = refs[2:]
    if has_s0:
        s0_ref = refs[0]
        refs = refs[1:]
    if has_acc:
        refs = refs[1:]
    o_ref, st_ref, sf, sb, dmat, etab, qs, ks, oscr = refs

    la_f = la_ref[0:1, :]
    la_b = la_ref[1:2, :]

    @pl.when(pl.program_id(0) == 0)
    def _():
        ri = _iota((C, W_GROUP), 0).astype(F32)
        etab[0] = jnp.exp((ri + 1.0) * la_f)
        etab[1] = jnp.exp((C - ri) * la_b)
        etab[2] = jnp.exp((C - 1.0 - ri) * la_f)
        etab[3] = jnp.exp(ri * la_b)
        ii, jj = _iota((C, C), 0), _iota((C, C), 1)
        diff = (ii - jj).astype(F32)
        for h in range(N_HEADS):
            laf_h = la_ref[0:1, D_HEAD * h:D_HEAD * h + 1]
            lab_h = la_ref[1:2, D_HEAD * h:D_HEAD * h + 1]
            dmat[h] = (jnp.exp(jnp.where(jj <= ii, diff * laf_h, NEG_BIG))
                       + jnp.exp(jnp.where(jj >= ii, -diff * lab_h, NEG_BIG)))

    dc_f = jnp.exp(C * la_f)
    dc_b = jnp.exp(C * la_b)
    bd = _head_ones()
    lane = _iota((1, W_GROUP), 1)

    sf[...] = jnp.zeros_like(sf)
    sb[...] = jnp.zeros_like(sb)
    if has_s0:
        for h in range(N_HEADS):
            sl = slice(D_HEAD * h, D_HEAD * (h + 1))
            sf[sl, sl] = s0_ref[0, 0, 0, h]
            sb[sl, sl] = s0_ref[0, 0, 1, h]

    def rot(x, cs, sn):
        halves = []
        for a in range(2):
            xh = x[:, 128 * a:128 * (a + 1)]
            lh = _iota(xh.shape, 1)
            halves.append(jnp.where(lh % D_HEAD < D_HEAD // 2, pltpu.roll(xh, 96, 1), pltpu.roll(xh, 32, 1)))
        return x * cs + jnp.concatenate(halves, axis=1) * sn

    def fwd(c, carry):
        r0 = pl.multiple_of(c * C, C)
        rows = pl.ds(r0, C)
        q = p_ref[rows, 0:256]
        k = p_ref[rows, 256:512] * (D_HEAD ** -0.5)
        v = p_ref[rows, 512:768]
        if rope:
            cs, sn = cos_ref[rows, :], sin_ref[rows, :]
            q, k = rot(q, cs, sn), rot(k, cs, sn)
        qs[rows, :] = q
        ks[rows, :] = k
        o = _dot(q * etab[0], sf[...])
        for h in range(N_HEADS):
            mh = jnp.where(lane // D_HEAD == h, 1.0, 0.0)
            att = _dot(q * mh, k, NT) * dmat[h]
            o = o + _dot(att, v * mh)
        oscr[rows, :] = o
        sf[...] = sf[...] * dc_f + bd * _dot(k * etab[2], v, TN)
        return carry

    lax.fori_loop(0, n_chunks, fwd, 0)

    def bwd(t, carry):
        c = n_chunks - 1 - t
        r0 = pl.multiple_of(c * C, C)
        rows = pl.ds(r0, C)
        q, k = qs[rows, :], ks[rows, :]
        v = p_ref[rows, 512:768]
        g = p_ref[rows, 768:1024]
        o = oscr[rows, :] + _dot(q * etab[1], sb[...])
        sb[...] = sb[...] * dc_b + bd * _dot(k * etab[3], v, TN)
        ms = _dot01(bd, o * o, m_left=False) * (1.0 / D_HEAD)
        o_ref[rows, :] = o * lax.rsqrt(ms + EPS) * nw_ref[...] * _silu(g)
        return carry

    lax.fori_loop(0, n_chunks, bwd, 0)

    for h in range(N_HEADS):
        sl = slice(D_HEAD * h, D_HEAD * (h + 1))
        st_ref[0, 0, 0, h] = sf[sl, sl]
        st_ref[0, 0, 1, h] = sb[sl, sl]


def _state_target(acc, bsz, n_seq, layer, tail, args, in_specs, out_index):
    if acc is None:
        shape = jax.ShapeDtypeStruct((bsz, 1, 2) + tail, F32)
        return shape, pl.BlockSpec((n_seq, 1, 2) + tail, lambda i, *_: (i, 0, 0, 0, 0, 0)), {}
    alias = {len(args): out_index}
    args.append(acc)
    in_specs.append(pl.BlockSpec(memory_space=pl.ANY))
    shape = jax.ShapeDtypeStruct(acc.shape, F32)
    return shape, pl.BlockSpec((n_seq, 1, 2) + tail, lambda i, *_: (i, layer, 0, 0, 0, 0)), alias


def _retention(p_ret, la_lane, nw, bsz, seq_len, rope_tabs, s0, layer, acc):
    n_chunks = seq_len // SCAN_CHUNK
    rope = rope_tabs is not None
    has_s0 = s0 is not None
    const = lambda a, b: pl.BlockSpec((a, b), lambda i: (0, 0))
    in_specs = [pl.BlockSpec((seq_len, 1024), lambda i: (i, 0)), const(2, W_GROUP), const(1, W_GROUP)]
    args = [p_ret, la_lane, nw]
    if rope:
        in_specs += [const(seq_len, W_GROUP)] * 2
        args += list(rope_tabs)
    if has_s0:
        in_specs.append(pl.BlockSpec((1, 1, 2, N_HEADS, D_HEAD, D_HEAD), lambda i: (i, layer, 0, 0, 0, 0)))
        args.append(s0)
    st_shape, st_spec, alias = _state_target(acc, bsz, 1, layer, (N_HEADS, D_HEAD, D_HEAD), args, in_specs, 1)
    return pl.pallas_call(
        functools.partial(_ret_kernel, n_chunks, rope, has_s0, acc is not None),
        out_shape=(jax.ShapeDtypeStruct((bsz * seq_len, W_GROUP), F32), st_shape),
        grid=(bsz,),
        in_specs=in_specs,
        out_specs=(pl.BlockSpec((seq_len, W_GROUP), lambda i: (i, 0)), st_spec),
        input_output_aliases=alias,
        scratch_shapes=[pltpu.VMEM((W_GROUP, W_GROUP), F32), pltpu.VMEM((W_GROUP, W_GROUP), F32),
                        pltpu.VMEM((N_HEADS, SCAN_CHUNK, SCAN_CHUNK), F32),
                        pltpu.VMEM((4, SCAN_CHUNK, W_GROUP), F32),
                        pltpu.VMEM((seq_len, W_GROUP), F32), pltpu.VMEM((seq_len, W_GROUP), F32),
                        pltpu.VMEM((seq_len, W_GROUP), F32)],
        compiler_params=_params(("arbitrary",)),
        name="retention",
    )(*args)


def _ssd_kernel(n_chunks, has_s0, has_acc, *refs):
    C = SCAN_CHUNK
    refs = list(refs)
    p_ref, ps_ref, cw_ref, cb_ref, dtb_ref, alog_ref, dl_ref, nw_ref = refs[:8]
    refs = refs[8:]
    if has_s0:
        s0_ref = refs[0]
        refs = refs[1:]
    if has_acc:
        refs = refs[1:]
    o_ref, st_ref, sf, sb, xs_s, bm_s, cm_s, oscr = refs
    seq_len = n_chunks * C

    def prep(c, carry):
        r0 = pl.multiple_of(c * C, C)
        y = _conv3_chunk(p_ref, r0, C, seq_len, 0, 768, cw_ref, c == 0, c == n_chunks - 1) + cb_ref[...]
        y = _silu(y)
        rows = pl.ds(r0, C)
        xs_s[rows, :] = y[:, 0:256]
        bm_s[rows, :] = y[:, 256:512]
        cm_s[rows, :] = y[:, 512:768]
        return carry

    lax.fori_loop(0, n_chunks, prep, 0)

    sf[...] = jnp.zeros_like(sf)
    sb[...] = jnp.zeros_like(sb)
    if has_s0:
        for h in range(N_HEADS):
            g = h // 2
            sf[N_SSD * g:N_SSD * (g + 1), D_HEAD * h:D_HEAD * (h + 1)] = s0_ref[0, 0, 0, h]
            sb[N_SSD * g:N_SSD * (g + 1), D_HEAD * h:D_HEAD * (h + 1)] = s0_ref[0, 0, 1, h]

    ltri, utri = _tri(C, True), _tri(C, False)
    exp_f, exp_b = _lane_expand(16, D_HEAD), _lane_expand(20, D_HEAD)
    gmask = _head_ones(W_GROUP, N_SSD)
    lane_s = _iota((1, SMALL_W), 1)
    dt_mask = jnp.where((lane_s >= 16) & (lane_s < 24), 1.0, 0.0)
    lane = _iota((1, W_GROUP), 1)
    ii, jj = _iota((C, C), 0), _iota((C, C), 1)

    def gates(rows):
        dt = _softplus(ps_ref[rows, :] + dtb_ref[...]) * dt_mask
        la = dt * (-jnp.exp(alog_ref[...]))
        return dt, la

    def fwd(c, carry):
        r0 = pl.multiple_of(c * C, C)
        rows = pl.ds(r0, C)
        dt, la = gates(rows)
        cum_p = _dot01(ltri, la)
        cum_s = _dot01(utri, la)
        cum_pt, cum_st, dt_t = cum_p.T, cum_s.T, dt.T
        x_f = _dot01(exp_f, cum_p, m_left=False)
        dt_f = _dot01(exp_f, dt, m_left=False)
        xs, bm, cm = xs_s[rows, :], bm_s[rows, :], cm_s[rows, :]
        s_g = [_dot(cm[:, N_SSD * g:N_SSD * (g + 1)], bm[:, N_SSD * g:N_SSD * (g + 1)], NT) for g in range(G_SSD)]
        o = _dot(cm, sf[...]) * jnp.exp(x_f)
        for h in range(N_HEADS):
            f, b = 16 + h, 20 + h
            dec = (jnp.exp(jnp.where(jj <= ii, cum_p[:, f:f + 1] - cum_pt[f:f + 1, :], NEG_BIG)) * dt_t[f:f + 1, :]
                   + jnp.exp(jnp.where(jj >= ii, cum_s[:, b:b + 1] - cum_st[b:b + 1, :], NEG_BIG)) * dt_t[b:b + 1, :])
            mh = jnp.where(lane // D_HEAD == h, 1.0, 0.0)
            o = o + _dot(s_g[h // 2] * dec, xs * mh)
        oscr[rows, :] = o
        last = x_f[C - 1:C, :]
        sf[...] = sf[...] * jnp.exp(last) + gmask * _dot(bm, xs * dt_f * jnp.exp(last - x_f), TN)
        return carry

    lax.fori_loop(0, n_chunks, fwd, 0)

    def bwd(t, carry):
        c = n_chunks - 1 - t
        r0 = pl.multiple_of(c * C, C)
        rows = pl.ds(r0, C)
        dt, la = gates(rows)
        cum_s = _dot01(utri, la)
        x_b = _dot01(exp_b, cum_s, m_left=False)
        dt_b = _dot01(exp_b, dt, m_left=False)
        xs, bm, cm = xs_s[rows, :], bm_s[rows, :], cm_s[rows, :]
        o = oscr[rows, :] + _dot(cm, sb[...]) * jnp.exp(x_b)
        first = x_b[0:1, :]
        sb[...] = sb[...] * jnp.exp(first) + gmask * _dot(bm, xs * dt_b * jnp.exp(first - x_b), TN)
        y = (o + xs * dl_ref[...]) * _silu(p_ref[rows, 768:1024])
        halves = [y[:, 128 * g:128 * (g + 1)] for g in range(G_SSD)]
        halves = [yh * lax.rsqrt(jnp.mean(yh * yh, axis=-1, keepdims=True) + EPS) for yh in halves]
        o_ref[rows, :] = jnp.concatenate(halves, axis=1) * nw_ref[...]
        return carry

    lax.fori_loop(0, n_chunks, bwd, 0)

    for h in range(N_HEADS):
        g = h // 2
        st_ref[0, 0, 0, h] = sf[N_SSD * g:N_SSD * (g + 1), D_HEAD * h:D_HEAD * (h + 1)]
        st_ref[0, 0, 1, h] = sb[N_SSD * g:N_SSD * (g + 1), D_HEAD * h:D_HEAD * (h + 1)]


def _ssd(p_ssd, p_small, conv_w, conv_b, dtb, alog, d_lane, nw, bsz, seq_len, s0, layer, acc):
    n_chunks = seq_len // SCAN_CHUNK
    has_s0 = s0 is not None
    const = lambda a, b: pl.BlockSpec((a, b), lambda i: (0, 0))
    in_specs = [pl.BlockSpec((seq_len, 1024), lambda i: (i, 0)), pl.BlockSpec((seq_len, SMALL_W), lambda i: (i, 0)),
                const(3, 768), const(1, 768), const(1, SMALL_W), const(1, SMALL_W), const(1, W_GROUP),
                const(1, W_GROUP)]
    args = [p_ssd, p_small, conv_w, conv_b, dtb, alog, d_lane, nw]
    if has_s0:
        in_specs.append(pl.BlockSpec((1, 1, 2, N_HEADS, N_SSD, D_HEAD), lambda i: (i, layer, 0, 0, 0, 0)))
        args.append(s0)
    st_shape, st_spec, alias = _state_target(acc, bsz, 1, layer, (N_HEADS, N_SSD, D_HEAD), args, in_specs, 1)
    return pl.pallas_call(
        functools.partial(_ssd_kernel, n_chunks, has_s0, acc is not None),
        out_shape=(jax.ShapeDtypeStruct((bsz * seq_len, W_GROUP), F32), st_shape),
        grid=(bsz,),
        in_specs=in_specs,
        out_specs=(pl.BlockSpec((seq_len, W_GROUP), lambda i: (i, 0)), st_spec),
        input_output_aliases=alias,
        scratch_shapes=[pltpu.VMEM((W_GROUP, W_GROUP), F32), pltpu.VMEM((W_GROUP, W_GROUP), F32)]
        + [pltpu.VMEM((seq_len, W_GROUP), F32)] * 4,
        compiler_params=_params(("arbitrary",)),
        name="ssd",
    )(*args)


def _gdn_prep_kernel(n_pre, p_ref, cw_ref, q_ref, k_ref, v_ref):
    CP = SCAN_CHUNK
    bd = _head_ones()

    def prep(c, carry):
        r0 = pl.multiple_of(c * CP, CP)
        y = _silu(_conv3_chunk(p_ref, r0, CP, n_pre * CP, 0, 768, cw_ref, c == 0, c == n_pre - 1))
        q, k, v = y[:, 0:256], y[:, 256:512], y[:, 512:768]
        rows = pl.ds(r0, CP)
        q_ref[rows, :] = q * lax.rsqrt(_dot01(bd, q * q, m_left=False) + EPS) * (D_HEAD ** -0.5)
        k_ref[rows, :] = k * lax.rsqrt(_dot01(bd, k * k, m_left=False) + EPS)
        v_ref[rows, :] = v
        return carry

    lax.fori_loop(0, n_pre, prep, 0)


def _gdn_scan_kernel(n_seq, n_blk, has_s0, has_acc, *refs):
    C = GDN_CHUNK
    R = N_HEADS * C
    SUB = SCAN_CHUNK // C
    refs = list(refs)
    qf, kf, vf, gf, qb, kb, vb, gb, dtb_ref, alog_ref = refs[:10]
    refs = refs[10:]
    if has_s0:
        s0_ref = refs[0]
        refs = refs[1:]
    if has_acc:
        refs = refs[1:]
    of_ref, ob_ref, st_ref, sf, sb = refs
    t = pl.program_id(1)

    @pl.when(t == 0)
    def _():
        sf[...] = jnp.zeros_like(sf)
        sb[...] = jnp.zeros_like(sb)
        if has_s0:
            for j in range(n_seq):
                for h in range(N_HEADS):
                    sl = slice(D_HEAD * h, D_HEAD * (h + 1))
                    sf[j, sl, sl] = s0_ref[j, 0, 0, h]
                    sb[j, sl, sl] = s0_ref[j, 0, 1, h]

    lane_s = _iota((1, SMALL_W), 1)
    la_mask = jnp.where(lane_s < 8, 1.0, 0.0)
    same = (_iota((R, R), 0) // C) == (_iota((R, R), 1) // C)
    lane_head = _iota((1, W_GROUP), 1) // D_HEAD
    pad_rows = jnp.zeros((SMALL_W - C, SMALL_W), F32)
    ic = _iota((C, R), 0)
    jc = _iota((C, R), 1) % C
    zero_b = jnp.zeros((R, R), BF16)

    def expand(x):
        xb = x.astype(BF16)
        return jnp.where(same, jnp.concatenate([xb] * N_HEADS, axis=0), zero_b)

    def per_lane(x, lane0):
        cols = [jnp.broadcast_to(x[:, lane0 + h:lane0 + h + 1], (C, W_GROUP)) for h in range(N_HEADS)]
        return jnp.where(lane_head == 0, cols[0],
                         jnp.where(lane_head == 1, cols[1], jnp.where(lane_head == 2, cols[2], cols[3])))

    def chunks(streams):
        seqs, offs, dirs = zip(*streams)
        fwd = [d == 0 for d in dirs]
        base = [4 * d for d in dirs]
        rows = [pl.ds(pl.multiple_of(r0, C), C) for r0 in offs]
        pick = lambda a_f, a_b: [(a_f if f else a_b) for f in fwd]
        q_refs, k_refs, v_refs, g_refs = pick(qf, qb), pick(kf, kb), pick(vf, vb), pick(gf, gb)
        s_refs, o_refs = pick(sf, sb), pick(of_ref, ob_ref)
        g = [r[j, rw, :] for r, j, rw in zip(g_refs, seqs, rows)]
        la = [-jnp.exp(alog_ref[...]) * _softplus(x + dtb_ref[...]) * la_mask for x in g]
        beta = [_sigmoid(x) for x in g]
        cum = [_dot01(_tri(C, f), x) for f, x in zip(fwd, la)]
        last_row = [x[C - 1:C, :] if f else x[0:1, :] for f, x in zip(fwd, cum)]
        cum_t = [jnp.concatenate([x, pad_rows], axis=0).T for x in cum]
        cum_row = []
        for x, b in zip(cum_t, base):
            pc = [x[b + h:b + h + 1, :] for h in range(N_HEADS)]
            cum_row.append(jnp.concatenate([pc[0] + pltpu.roll(pc[1], C, 1), pc[2] + pltpu.roll(pc[3], C, 1)],
                                           axis=1))
        cum_n = [per_lane(x, b) for x, b in zip(cum, base)]
        beta_n = [per_lane(x, 8 + b) for x, b in zip(beta, base)]
        last_lane = []
        for l, b in zip(last_row, base):
            ll = jnp.zeros((1, W_GROUP), F32)
            for h in range(N_HEADS):
                ll = jnp.where(lane_head == h, l[:, b + h:b + h + 1], ll)
            last_lane.append(ll)
        ecum_n = [jnp.exp(x) for x in cum_n]
        kdec_n = [jnp.exp(l - x) for l, x in zip(last_lane, cum_n)]
        incl = [(jc <= ic) if f else (jc >= ic) for f in fwd]
        strict = [(jc < ic) if f else (jc > ic) for f in fwd]
        dec = [jnp.exp(jnp.where(m, x - y, NEG_BIG)) for m, x, y in zip(incl, cum_n, cum_row)]
        q = [r[j, rw, :] for r, j, rw in zip(q_refs, seqs, rows)]
        k = [r[j, rw, :] for r, j, rw in zip(k_refs, seqs, rows)]
        v = [r[j, rw, :] for r, j, rw in zip(v_refs, seqs, rows)]
        k_x = [expand(x) for x in k]
        kk = [_dot(x, y, NT) for x, y in zip(k, k_x)]
        qk = [_dot(x, y, NT) for x, y in zip(q, k_x)]
        a = [jnp.where(m, d * x, 0.0) * b for m, d, x, b in zip(strict, dec, kk, beta_n)]
        n = [-jnp.where((ic // 2) == (jc // 2), x, 0.0) for x in a]
        s = 2
        while s < C:
            blk = ((ic // (2 * s)) == (jc // (2 * s))) & ((ic // s) != (jc // s))
            lo = [jnp.where(blk, x, 0.0) for x in a]
            y = [x + _dot(x, expand(m)) for x, m in zip(lo, n)]
            n = [m - (x + _dot(m, expand(x))) for m, x in zip(n, y)]
            s *= 2
        kp = [x * (b * e) for x, b, e in zip(k, beta_n, ecum_n)]
        vp = [x * b for x, b in zip(v, beta_n)]
        wm = [x + _dot(m, expand(x)) for m, x in zip(n, kp)]
        um = [x + _dot(m, expand(x)) for m, x in zip(n, vp)]
        st = [r[j] for r, j in zip(s_refs, seqs)]
        v_new = [u - _dot(w_, s_) for u, w_, s_ in zip(um, wm, st)]
        o = [_dot(x * e, s_) + _dot(y * d, expand(vn))
             for x, e, s_, y, d, vn in zip(q, ecum_n, st, qk, dec, v_new)]
        for j, rw, o_r, x in zip(seqs, rows, o_refs, o):
            o_r[j, rw, :] = x
        for s_r, j, s_, l, x, kd, vn in zip(s_refs, seqs, st, last_lane, k, kdec_n, v_new):
            s_r[j] = s_ * jnp.exp(l) + jnp.where(same, _dot(x * kd, vn, TN), 0.0)

    def scan_body(u, carry):
        streams = []
        for j in range(n_seq):
            streams += [(j, u * C, 0), (j, (SUB - 1 - u) * C, 1)]
        chunks(streams)
        return carry

    lax.fori_loop(0, SUB, scan_body, 0)

    @pl.when(t == n_blk - 1)
    def _():
        for j in range(n_seq):
            for h in range(N_HEADS):
                sl = slice(D_HEAD * h, D_HEAD * (h + 1))
                st_ref[j, 0, 0, h] = sf[j, sl, sl]
                st_ref[j, 0, 1, h] = sb[j, sl, sl]


def _gdn(p_gdn, p_small, conv_w, dtb, alog, bsz, seq_len, s0, layer, acc):
    n_tok = bsz * seq_len
    n_blk = seq_len // SCAN_CHUNK
    n_seq = min(GDN_SEQS_MAX, bsz)
    has_s0 = s0 is not None
    tok = jax.ShapeDtypeStruct((n_tok, W_GROUP), F32)

    q, k, v = pl.pallas_call(
        functools.partial(_gdn_prep_kernel, n_blk),
        out_shape=(tok, tok, tok),
        grid=(bsz,),
        in_specs=[pl.BlockSpec((seq_len, 768), lambda i: (i, 0)), pl.BlockSpec((3, 768), lambda i: (0, 0))],
        out_specs=(pl.BlockSpec((seq_len, W_GROUP), lambda i: (i, 0)),) * 3,
        compiler_params=_params(("arbitrary",)),
        name="gated_delta_prep",
    )(p_gdn, conv_w)

    seq3 = lambda a: a.reshape(bsz, seq_len, a.shape[-1])
    fwd_blk = lambda w: pl.BlockSpec((n_seq, SCAN_CHUNK, w), lambda i, t: (i, t, 0))
    bwd_blk = lambda w: pl.BlockSpec((n_seq, SCAN_CHUNK, w), lambda i, t: (i, n_blk - 1 - t, 0))
    const = lambda a, b: pl.BlockSpec((a, b), lambda i, t: (0, 0))
    q3, k3, v3, g3 = seq3(q), seq3(k), seq3(v), seq3(p_small)
    in_specs = [fwd_blk(W_GROUP)] * 3 + [fwd_blk(SMALL_W)] + [bwd_blk(W_GROUP)] * 3 + [bwd_blk(SMALL_W)] \
        + [const(1, SMALL_W), const(1, SMALL_W)]
    args = [q3, k3, v3, g3, q3, k3, v3, g3, dtb, alog]
    if has_s0:
        in_specs.append(pl.BlockSpec((n_seq, 1, 2, N_HEADS, D_HEAD, D_HEAD), lambda i, t: (i, layer, 0, 0, 0, 0)))
        args.append(s0)
    seq_out = jax.ShapeDtypeStruct((bsz, seq_len, W_GROUP), F32)
    st_shape, st_spec, alias = _state_target(acc, bsz, n_seq, layer, (N_HEADS, D_HEAD, D_HEAD), args, in_specs, 2)
    o_f, o_b, state = pl.pallas_call(
        functools.partial(_gdn_scan_kernel, n_seq, n_blk, has_s0, acc is not None),
        out_shape=(seq_out, seq_out, st_shape),
        grid=(bsz // n_seq, n_blk),
        in_specs=in_specs,
        out_specs=(fwd_blk(W_GROUP), bwd_blk(W_GROUP), st_spec),
        input_output_aliases=alias,
        scratch_shapes=[pltpu.VMEM((n_seq, W_GROUP, W_GROUP), F32), pltpu.VMEM((n_seq, W_GROUP, W_GROUP), F32)],
        compiler_params=_params(("arbitrary", "arbitrary")),
        name="gated_delta_scan",
    )(*args)
    return o_f.reshape(n_tok, W_GROUP), o_b.reshape(n_tok, W_GROUP), state


def _dft_tables(seq_len):
    period = 8 * seq_len
    theta = math.pi / (4 * seq_len)
    fine, coarse = 64, seq_len // 64
    odd = 2 * jnp.arange(seq_len, dtype=jnp.int32) + 1
    ka = ((2 * fine * jnp.arange(coarse, dtype=jnp.int32))[:, None] * odd[None, :]) % period
    kb = (odd[:fine, None] * odd[None, :]) % period
    ang_a = ka.astype(F32)[:, None, :] * theta
    ang_b = kb.astype(F32)[None, :, :] * theta
    ca, sa, cb, sb = jnp.cos(ang_a), jnp.sin(ang_a), jnp.cos(ang_b), jnp.sin(ang_b)
    c2 = (ca * cb - sa * sb).reshape(seq_len, seq_len).astype(BF16)
    s2 = (sa * cb + ca * sb).reshape(seq_len, seq_len).astype(BF16)
    half = odd.astype(F32) * theta
    return c2, s2, jnp.stack([jnp.cos(half), jnp.sin(half)], axis=1)


def _filter_features(seq_len):
    t = jnp.linspace(0.0, 1.0, seq_len, dtype=F32)[:, None]
    bands = (HY_EMB - 1) // 2
    w = 2.0 * math.pi * jnp.arange(seq_len, dtype=F32)[:, None] / seq_len
    f = jnp.linspace(1e-4, bands - 1, bands, dtype=F32)[None, :]
    z = jnp.concatenate([t, jnp.cos(f * w), -jnp.sin(f * w)], axis=-1)
    z = jnp.pad(z, ((0, 0), (0, 128 - HY_EMB)))
    deltas = jnp.abs(jnp.linspace(math.log(HY_DECAY_TARGET) / HY_SLOW_PCT,
                                  math.log(HY_DECAY_TARGET) / HY_FAST_PCT, W_GROUP, dtype=F32))
    return z, jnp.exp(-t * deltas)


def _hyena_kernel(n_chunks, p_ref, cw_ref, cb_ref, bias_ref, nw_ref, z_ref, dec_ref, fr_ref, w1_ref, b1_ref,
                  w2_ref, b2_ref, w3_ref, c2_ref, s2_ref, ph_ref, o_ref,
                  hre_s, him_s, x0_s, u_s, ub_s, a_s, b_s, yre_s, yim_s):
    C = SCAN_CHUNK
    seq_len = n_chunks * C

    @pl.when(pl.program_id(0) == 0)
    def _():
        fr = fr_ref[0]

        def taps_chunk(c, carry):
            r0 = pl.multiple_of(c * C, C)
            rows = pl.ds(r0, C)
            h = jnp.sin(fr * (_dot_hp(z_ref[rows, :], w1_ref[0]) + b1_ref[0]))
            h = jnp.sin(fr * (_dot_hp(h, w2_ref[0]) + b2_ref[0]))
            taps = _dot_hp(h, w3_ref[0])
            hf = taps[:, 0:W_GROUP] * dec_ref[rows, :]
            hb = taps[:, W_GROUP:2 * W_GROUP] * dec_ref[rows, :]
            hb = jnp.where(_iota(hb.shape, 0) + r0 == 0, 0.0, hb)
            yre_s[rows, :] = (hf + hb).astype(BF16)
            yim_s[rows, :] = (hf - hb).astype(BF16)
            return carry

        lax.fori_loop(0, n_chunks, taps_chunk, 0)
        a_s[...] = _dot(c2_ref[...], yre_s[...])
        b_s[...] = _dot(s2_ref[...], yre_s[...])

        def hre_chunk(c, carry):
            rows = pl.ds(pl.multiple_of(c * C, C), C)
            hre_s[rows, :] = ph_ref[rows, 0:1] * a_s[rows, :] + ph_ref[rows, 1:2] * b_s[rows, :]
            return carry

        lax.fori_loop(0, n_chunks, hre_chunk, 0)
        a_s[...] = _dot(c2_ref[...], yim_s[...])
        b_s[...] = _dot(s2_ref[...], yim_s[...])

        def him_chunk(c, carry):
            rows = pl.ds(pl.multiple_of(c * C, C), C)
            him_s[rows, :] = ph_ref[rows, 1:2] * a_s[rows, :] - ph_ref[rows, 0:1] * b_s[rows, :]
            return carry

        lax.fori_loop(0, n_chunks, him_chunk, 0)

    def prep(c, carry):
        r0 = pl.multiple_of(c * C, C)
        y = _conv3_chunk(p_ref, r0, C, seq_len, 0, 768, cw_ref, c == 0, c == n_chunks - 1) + cb_ref[...]
        rows = pl.ds(r0, C)
        u = y[:, 512:768] * y[:, 256:512]
        x0_s[rows, :] = y[:, 0:256]
        u_s[rows, :] = u
        ub_s[rows, :] = u.astype(BF16)
        return carry

    lax.fori_loop(0, n_chunks, prep, 0)

    a_s[...] = _dot(c2_ref[...], ub_s[...])
    b_s[...] = _dot(s2_ref[...], ub_s[...])

    def spectrum(c, carry):
        rows = pl.ds(pl.multiple_of(c * C, C), C)
        ure, su, hre, him = a_s[rows, :], b_s[rows, :], hre_s[rows, :], him_s[rows, :]
        yre_s[rows, :] = (ure * hre + su * him).astype(BF16)
        yim_s[rows, :] = (ure * him - su * hre).astype(BF16)
        return carry

    lax.fori_loop(0, n_chunks, spectrum, 0)
    a_s[...] = _dot(c2_ref[...], yre_s[...])
    b_s[...] = _dot(s2_ref[...], yim_s[...])

    def fin(c, carry):
        rows = pl.ds(pl.multiple_of(c * C, C), C)
        conv = (a_s[rows, :] - b_s[rows, :]) * (1.0 / seq_len)
        y = x0_s[rows, :] * (conv + u_s[rows, :] * bias_ref[...])
        o_ref[rows, :] = _rms(y, nw_ref[...])
        return carry

    lax.fori_loop(0, n_chunks, fin, 0)


def _hyena(p_hy, conv_w, conv_b, bias, nw, tabs, feats, freq, w1, b1, w2, b2, w3, bsz, seq_len, layer):
    c2, s2, phase = tabs
    z, dec = feats
    one = dict(pipeline_mode=pl.Buffered(1)) if seq_len > 1024 else {}
    const = lambda a, b: pl.BlockSpec((a, b), lambda i: (0, 0), **one)
    lay = lambda a, b: pl.BlockSpec((1, a, b), lambda i: (layer, 0, 0))
    acc = pltpu.VMEM((seq_len, W_GROUP), F32)
    half = pltpu.VMEM((seq_len, W_GROUP), BF16)
    return pl.pallas_call(
        functools.partial(_hyena_kernel, seq_len // SCAN_CHUNK),
        out_shape=jax.ShapeDtypeStruct((bsz * seq_len, W_GROUP), F32),
        grid=(bsz,),
        in_specs=[pl.BlockSpec((seq_len, 768), lambda i: (i, 0), **one), const(3, 768), const(1, 768),
                  const(1, W_GROUP), const(1, W_GROUP), const(seq_len, 128), const(seq_len, W_GROUP),
                  lay(1, HY_HIDDEN), lay(128, HY_HIDDEN), lay(1, HY_HIDDEN), lay(HY_HIDDEN, HY_HIDDEN),
                  lay(1, HY_HIDDEN), lay(HY_HIDDEN, 2 * W_GROUP),
                  const(seq_len, seq_len), const(seq_len, seq_len), const(seq_len, 2)],
        out_specs=pl.BlockSpec((seq_len, W_GROUP), lambda i: (i, 0)),
        scratch_shapes=[acc, acc, acc, acc, half, acc, acc, half, half],
        compiler_params=_params(("arbitrary",)),
        name="hyena",
    )(p_hy, conv_w, conv_b, bias, nw, z, dec, freq, w1, b1, w2, b2, w3, c2, s2, phase)


def _rope_tables(seq_len):
    rows = seq_len // GRID_W
    r = jnp.repeat(jnp.arange(rows), GRID_W).astype(F32)
    col = jnp.tile(jnp.arange(GRID_W), rows).astype(F32)
    quarter = D_HEAD // 4
    inv = ROPE_BASE ** (-jnp.arange(quarter, dtype=F32) / quarter)
    ang = jnp.concatenate([r[:, None] * inv, col[:, None] * inv], axis=-1)
    cos, sin = jnp.cos(ang), jnp.sin(ang)
    cos_full = jnp.tile(jnp.concatenate([cos, cos], axis=-1), (1, N_HEADS))
    sin_signed = jnp.tile(jnp.concatenate([-sin, sin], axis=-1), (1, N_HEADS))
    return cos_full, sin_signed


def _small_vec(first_lane, vals):
    flat = vals.reshape(-1)
    return jnp.zeros((1, SMALL_W), F32).at[0, first_lane:first_lane + flat.shape[0]].set(flat)


def _trunk(x, bsz, seq_len, row_fn, states, finals, rope_tabs, hy_tabs, hy_feats, mod3, w):
    st_ret, st_gdn, st_ssd = states
    f_ret, f_gdn, f_ssd = finals
    for l in range(DEPTH):
        p_ret, p_hy, p_gdn, p_ssd, p_small = _in_proj(x, mod3, w['norm1'], w['w_main'], w['w_ssd'],
                                                      w['w_small'], l, row_fn)
        o_ret, s_ret = _retention(p_ret, w['ret_la'][l], w['ret_norm'][l], bsz, seq_len, rope_tabs, st_ret, l,
                                  f_ret)
        o_hy = _hyena(p_hy, w['hy_conv_w'][l], w['hy_conv_b'][l], w['hy_bias'][l], w['hy_norm'][l],
                      hy_tabs, hy_feats, w['hy_freq'], w['hy_w1'], w['hy_b1'], w['hy_w2'], w['hy_b2'],
                      w['hy_w3'], bsz, seq_len, l)
        og_f, og_b, s_gdn = _gdn(p_gdn, p_small, w['gdn_conv_w'][l], w['gdn_dtb'][l], w['gdn_alog'][l],
                                 bsz, seq_len, st_gdn, l, f_gdn)
        o_ssd, s_ssd = _ssd(p_ssd, p_small, w['ssd_conv_w'][l], w['ssd_conv_b'][l], w['ssd_dtb'][l],
                            w['ssd_alog'][l], w['ssd_dl'][l], w['ssd_norm'][l], bsz, seq_len, st_ssd, l, f_ssd)
        x = _out_proj((o_ret, o_hy, og_f, og_b, p_gdn, w['gdn_norm'][l], o_ssd), x, mod3, w['norm2'],
                      w['w_out'], w['mlp_w1'], w['mlp_w2'], w['final_norm'], l, row_fn, l == DEPTH - 1)
        if f_ret is not None:
            f_ret, f_gdn, f_ssd = s_ret, s_gdn, s_ssd
    return x, (f_ret, f_gdn, f_ssd)


def kernel(x_prompt, x_sample, state_ret, state_gdn, state_ssd, c, c_ctx, norm1_w, norm2_w, w_mod, b_mod, w_in, w_out, ret_log_decay, ret_norm_w, hy_conv_w, hy_conv_b, hy_freq, hy_w1, hy_b1, hy_w2, hy_b2, hy_w3, hy_bias, hy_norm_w, gdn_conv_w, gdn_A_log, gdn_dt_bias, gdn_norm_w, ssd_conv_w, ssd_conv_b, ssd_A_log, ssd_dt_bias, ssd_D, ssd_norm_w, mlp_w1, mlp_w2, final_norm_w):
    bsz, seq, _ = x_prompt.shape
    dbsz, dseq, _ = x_sample.shape

    cond8 = jnp.concatenate([c, c_ctx[None, :], jnp.zeros((MOD_ROWS - dbsz - 1, D_MODEL), F32)], axis=0)
    mod = _modulation(cond8, w_mod, b_mod)
    mod3 = mod.reshape(DEPTH * MOD_ROWS * 6, 1, D_MODEL)

    w_main = w_in.astype(BF16)
    w_ssd = w_main[:, :, 2832:3856]
    w_small = jnp.concatenate([w_main[:, :, 2816:2832], w_main[:, :, 3856:3864],
                               jnp.zeros((DEPTH, D_MODEL, SMALL_W - 24), BF16)], axis=-1)
    w = dict(
        norm1=norm1_w.reshape(DEPTH, 1, D_MODEL), norm2=norm2_w.reshape(DEPTH, 1, D_MODEL),
        w_main=w_main, w_ssd=w_ssd, w_small=w_small, w_out=w_out.astype(BF16),
        mlp_w1=mlp_w1.astype(BF16), mlp_w2=mlp_w2.astype(BF16), final_norm=final_norm_w,
        ret_la=[jnp.repeat(ret_log_decay[l], D_HEAD, axis=-1) for l in range(DEPTH)],
        ret_norm=[ret_norm_w[l][None, :] for l in range(DEPTH)],
        hy_freq=hy_freq.reshape(DEPTH, 1, HY_HIDDEN),
        hy_w1=jnp.pad(hy_w1, ((0, 0), (0, 128 - HY_EMB), (0, 0))),
        hy_b1=hy_b1.reshape(DEPTH, 1, HY_HIDDEN), hy_w2=hy_w2, hy_b2=hy_b2.reshape(DEPTH, 1, HY_HIDDEN),
        hy_w3=hy_w3,
        hy_conv_w=hy_conv_w, hy_conv_b=[hy_conv_b[l][None, :] for l in range(DEPTH)],
        hy_bias=[hy_bias[l][None, :] for l in range(DEPTH)], hy_norm=[hy_norm_w[l][None, :] for l in range(DEPTH)],
        gdn_conv_w=gdn_conv_w,
        gdn_dtb=[_small_vec(0, gdn_dt_bias[l]) for l in range(DEPTH)],
        gdn_alog=[_small_vec(0, gdn_A_log[l]) for l in range(DEPTH)],
        gdn_norm=[gdn_norm_w[l][None, :] for l in range(DEPTH)],
        ssd_conv_w=ssd_conv_w, ssd_conv_b=[ssd_conv_b[l][None, :] for l in range(DEPTH)],
        ssd_dtb=[_small_vec(16, ssd_dt_bias[l]) for l in range(DEPTH)],
        ssd_alog=[_small_vec(16, ssd_A_log[l]) for l in range(DEPTH)],
        ssd_dl=[jnp.repeat(ssd_D[l], D_HEAD)[None, :] for l in range(DEPTH)],
        ssd_norm=[ssd_norm_w[l][None, :] for l in range(DEPTH)],
    )

    finals = (jnp.zeros((bsz, DEPTH, 2, N_HEADS, D_HEAD, D_HEAD), F32),
              jnp.zeros((bsz, DEPTH, 2, N_HEADS, D_HEAD, D_HEAD), F32),
              jnp.zeros((bsz, DEPTH, 2, N_HEADS, N_SSD, D_HEAD), F32))
    y_p, (new_ret, new_gdn, new_ssd) = _trunk(
        x_prompt.reshape(bsz * seq, D_MODEL), bsz, seq, lambda row: CTX_MOD_ROW, (None, None, None), finals,
        None, _dft_tables(seq), _filter_features(seq), mod3, w)

    y_s, _ = _trunk(x_sample.reshape(dbsz * dseq, D_MODEL), dbsz, dseq, lambda row: row // dseq,
                    (state_ret, state_gdn, state_ssd), (None, None, None), _rope_tables(dseq),
                    _dft_tables(dseq), _filter_features(dseq), mod3, w)

    return (y_p.reshape(bsz, seq, D_MODEL), y_s.reshape(dbsz, dseq, D_MODEL), new_ret, new_gdn, new_ssd)
```

```python
import functools
import math

import jax
import jax.numpy as jnp
from jax import lax
from jax.experimental import pallas as pl
from jax.experimental.pallas import tpu as pltpu

F32 = jnp.float32
BF16 = jnp.bfloat16

D_MODEL = 1024
DEPTH = 4
GRID_W = 64
W_GROUP = 256
N_HEADS = 4
D_HEAD = 64
N_SSD = 128
G_SSD = 2
HY_EMB = 33
HY_HIDDEN = 64
HY_DECAY_TARGET = 1e-2
HY_FAST_PCT = 0.3
HY_SLOW_PCT = 1.5
D_FF = 4 * D_MODEL
ROPE_BASE = 10000.0
EPS = 1e-6
SMALL_W = 128
MOD_ROWS = 8
CTX_MOD_ROW = 4

TOK_TILE = 512
OUT_TILE = 512
SCAN_CHUNK = 256
GDN_CHUNK = 64
NEG_BIG = -1e30
RET_SEQS = 2
SSD_SEQS = 2
GDN_SEQS_MAX = 4
VMEM_LIMIT = 56 * 1024 * 1024

NN = (((1,), (0,)), ((), ()))
NT = (((1,), (1,)), ((), ()))
TN = (((0,), (0,)), ((), ()))


def _dot(a, b, dims=NN):
    return lax.dot_general(a.astype(BF16), b.astype(BF16), dims, preferred_element_type=F32)


def _dot_hp(a, b, dims=NN):
    a_hi, a_lo = _split(a)
    b_hi, b_lo = _split(b)
    return _dot3(a_hi, a_lo, b_hi, b_lo, dims)


def _split(x):
    hi = x.astype(BF16)
    lo = (x - hi.astype(F32)).astype(BF16)
    return hi, lo


def _dot3(a_hi, a_lo, b_hi, b_lo, dims=NN):
    d = functools.partial(lax.dot_general, dimension_numbers=dims, preferred_element_type=F32)
    return d(a_hi, b_hi) + d(a_hi, b_lo) + d(a_lo, b_hi)


def _dot01(m, x, terms=3, m_left=True):
    mb = m.astype(BF16)
    acc, r = None, x.astype(F32)
    for _ in range(terms):
        t = r.astype(BF16)
        p = lax.dot_general(mb, t, NN, preferred_element_type=F32) if m_left else \
            lax.dot_general(t, mb, NN, preferred_element_type=F32)
        acc = p if acc is None else acc + p
        r = r - t.astype(F32)
    return acc


def _sigmoid(x):
    return 1.0 / (1.0 + jnp.exp(-x))


def _silu(x):
    return x * _sigmoid(x)


def _softplus(x):
    return jnp.maximum(x, 0.0) + jnp.log1p(jnp.exp(-jnp.abs(x)))


def _iota(shape, dim):
    return lax.broadcasted_iota(jnp.int32, shape, dim)


def _params(sem):
    return pltpu.CompilerParams(dimension_semantics=sem, vmem_limit_bytes=VMEM_LIMIT)


def _head_ones(n=W_GROUP, width=D_HEAD):
    return jnp.where(_iota((n, n), 0) // width == _iota((n, n), 1) // width, 1.0, 0.0).astype(F32)


def _mod_kernel(cond_ref, w_ref, b_ref, o_ref):
    s = _silu(cond_ref[...])
    o_ref[0] = _dot(s, w_ref[0]) + b_ref[0]


def _modulation(cond8, w_mod, b_mod):
    tn = 1536
    n_out = 6 * D_MODEL
    return pl.pallas_call(
        _mod_kernel,
        out_shape=jax.ShapeDtypeStruct((DEPTH, MOD_ROWS, n_out), F32),
        grid=(DEPTH, n_out // tn),
        in_specs=[
            pl.BlockSpec((MOD_ROWS, D_MODEL), lambda l, j: (0, 0)),
            pl.BlockSpec((1, D_MODEL, tn), lambda l, j: (l, 0, j)),
            pl.BlockSpec((1, 1, tn), lambda l, j: (l, 0, j)),
        ],
        out_specs=pl.BlockSpec((1, MOD_ROWS, tn), lambda l, j: (l, 0, j)),
        compiler_params=_params(("arbitrary", "arbitrary")),
        name="modulation",
    )(cond8, w_mod, b_mod.reshape(DEPTH, 1, n_out))


def _mod_spec(layer, k, row_fn, tile):
    return pl.BlockSpec((1, 1, D_MODEL), lambda i: ((layer * MOD_ROWS + row_fn(i * tile)) * 6 + k, 0, 0))


def _rms(x, w):
    return x * lax.rsqrt(jnp.mean(x * x, axis=-1, keepdims=True) + EPS) * w


def _in_kernel(x_ref, nw_ref, sh_ref, sc_ref, wm_ref, wd_ref, ws_ref, pret, phy, pgdn, pssd, psm):
    h = _rms(x_ref[...], nw_ref[0]) * (1.0 + sc_ref[0]) + sh_ref[0]
    hb = h.astype(BF16)
    d = functools.partial(jnp.dot, preferred_element_type=F32)
    pret[...] = d(hb, wm_ref[0, :, 0:1024])
    phy[...] = d(hb, wm_ref[0, :, 1024:1792])
    pgdn[...] = d(hb, wm_ref[0, :, 1792:2816])
    pssd[...] = d(hb, wd_ref[0])
    psm[...] = d(hb, ws_ref[0])


def _in_proj(x, mod3, norm1_w3, w_main, w_ssd, w_small, layer, row_fn):
    n = x.shape[0]
    widths = (1024, 768, 1024, 1024, SMALL_W)
    return pl.pallas_call(
        _in_kernel,
        out_shape=tuple(jax.ShapeDtypeStruct((n, w), F32) for w in widths),
        grid=(n // TOK_TILE,),
        in_specs=[
            pl.BlockSpec((TOK_TILE, D_MODEL), lambda i: (i, 0)),
            pl.BlockSpec((1, 1, D_MODEL), lambda i: (layer, 0, 0)),
            _mod_spec(layer, 0, row_fn, TOK_TILE),
            _mod_spec(layer, 1, row_fn, TOK_TILE),
            pl.BlockSpec((1, D_MODEL, w_main.shape[-1]), lambda i: (layer, 0, 0), pipeline_mode=pl.Buffered(1)),
            pl.BlockSpec((1, D_MODEL, 1024), lambda i: (layer, 0, 0), pipeline_mode=pl.Buffered(1)),
            pl.BlockSpec((1, D_MODEL, SMALL_W), lambda i: (layer, 0, 0), pipeline_mode=pl.Buffered(1)),
        ],
        out_specs=tuple(pl.BlockSpec((TOK_TILE, w), lambda i: (i, 0)) for w in widths),
        compiler_params=_params(("arbitrary",)),
        name="in_proj",
    )(x, norm1_w3, mod3, mod3, w_main, w_ssd, w_small)


def _out_kernel(final, oret, ohy, ogf, ogb, zg, gnw, ossd, x_ref, g1, sh2, sc2, g2, nw2, wout, w1, w2, *rest):
    if final:
        fnw, y_ref = rest
    else:
        (y_ref,) = rest
    acc = _dot(oret[...], wout[0, 0:256, :])
    acc = acc + _dot(ohy[...], wout[0, 256:512, :])
    acc = acc + _dot(ossd[...], wout[0, 768:1024, :])
    og = ogf[...] + ogb[...]
    ms = _dot01(_head_ones(), og * og, m_left=False) * (1.0 / D_HEAD)
    ogdn = og * lax.rsqrt(ms + EPS) * gnw[...] * _silu(zg[...])
    acc = acc + _dot(ogdn, wout[0, 512:768, :])
    x1 = x_ref[...] + g1[0] * acc
    h2 = _rms(x1, nw2[0]) * (1.0 + sc2[0]) + sh2[0]
    a = _dot(h2, w1[0])
    a = jnp.square(jnp.maximum(a, 0.0))
    x2 = x1 + g2[0] * _dot(a, w2[0])
    if final:
        x2 = _rms(x2, fnw[...])
    y_ref[...] = x2


def _out_proj(mix, x, mod3, norm2_w3, w_out, w1, w2, final_norm_w, layer, row_fn, final):
    n = x.shape[0]
    tok = lambda w: pl.BlockSpec((OUT_TILE, w), lambda i: (i, 0))
    lay = lambda a, b: pl.BlockSpec((1, a, b), lambda i: (layer, 0, 0), pipeline_mode=pl.Buffered(1))
    in_specs = [tok(W_GROUP)] * 4 + [
        pl.BlockSpec((OUT_TILE, W_GROUP), lambda i: (i, 3)), pl.BlockSpec((1, W_GROUP), lambda i: (0, 0)),
        tok(W_GROUP),
        tok(D_MODEL),
        _mod_spec(layer, 2, row_fn, OUT_TILE), _mod_spec(layer, 3, row_fn, OUT_TILE),
        _mod_spec(layer, 4, row_fn, OUT_TILE), _mod_spec(layer, 5, row_fn, OUT_TILE),
        lay(1, D_MODEL), lay(D_MODEL, D_MODEL), lay(D_MODEL, D_FF), lay(D_FF, D_MODEL),
    ]
    args = list(mix) + [x, mod3, mod3, mod3, mod3, norm2_w3, w_out, w1, w2]
    if final:
        in_specs.append(pl.BlockSpec((1, D_MODEL), lambda i: (0, 0)))
        args.append(final_norm_w.reshape(1, D_MODEL))
    return pl.pallas_call(
        functools.partial(_out_kernel, final),
        out_shape=jax.ShapeDtypeStruct((n, D_MODEL), F32),
        grid=(n // OUT_TILE,),
        in_specs=in_specs,
        out_specs=tok(D_MODEL),
        compiler_params=_params(("arbitrary",)),
        name="out_proj_mlp",
    )(*args)


def _conv3_chunk(p_ref, r0, rows, seq_len, c0, width, w_ref, first, last):
    x = p_ref[pl.ds(r0, rows), c0:c0 + width]
    prev_i = jnp.maximum(r0 - 1, 0)
    next_i = jnp.minimum(r0 + rows, seq_len - 1)
    xp = p_ref[pl.ds(prev_i, 1), c0:c0 + width] * jnp.where(first, 0.0, 1.0)
    xn = p_ref[pl.ds(next_i, 1), c0:c0 + width] * jnp.where(last, 0.0, 1.0)
    row = _iota((rows, width), 0)
    x_prev = jnp.where(row == 0, xp, pltpu.roll(x, 1, 0))
    x_next = jnp.where(row == rows - 1, xn, pltpu.roll(x, rows - 1, 0))
    return x_prev * w_ref[0:1, :] + x * w_ref[1:2, :] + x_next * w_ref[2:3, :]


def _tri(n, lower):
    i, j = _iota((n, n), 0), _iota((n, n), 1)
    return jnp.where((j <= i) if lower else (j >= i), 1.0, 0.0).astype(F32)


def _lane_expand(first_row, width):
    r, c = _iota((SMALL_W, W_GROUP), 0), _iota((SMALL_W, W_GROUP), 1)
    return jnp.where(r == first_row + c // width, 1.0, 0.0).astype(F32)


def _ret_kernel(n_chunks, n_seq, rope, has_s0, has_acc, *refs):
    C = SCAN_CHUNK
    refs = list(refs)
    p_ref, la_ref, nw_ref = refs[:3]
    refs = refs[3:]
    if rope:
        cos_ref, sin_ref = refs[:2]
        refs = refs[2:]
    if has_s0:
        s0_ref = refs[0]
        refs = refs[1:]
    if has_acc:
        refs = refs[1:]
    o_ref, st_ref, sf, sb, dmat, etab, qs, ks, oscr = refs

    la_f = la_ref[0:1, :]
    la_b = la_ref[1:2, :]

    @pl.when(pl.program_id(0) == 0)
    def _():
        ri = _iota((C, W_GROUP), 0).astype(F32)
        etab[0] = jnp.exp((ri + 1.0) * la_f)
        etab[1] = jnp.exp((C - ri) * la_b)
        etab[2] = jnp.exp((C - 1.0 - ri) * la_f)
        etab[3] = jnp.exp(ri * la_b)
        ii, jj = _iota((C, C), 0), _iota((C, C), 1)
        diff = (ii - jj).astype(F32)
        for h in range(N_HEADS):
            laf_h = la_ref[0:1, D_HEAD * h:D_HEAD * h + 1]
            lab_h = la_ref[1:2, D_HEAD * h:D_HEAD * h + 1]
            dmat[h] = (jnp.exp(jnp.where(jj <= ii, diff * laf_h, NEG_BIG))
                       + jnp.exp(jnp.where(jj >= ii, -diff * lab_h, NEG_BIG)))

    dc_f = jnp.exp(C * la_f)
    dc_b = jnp.exp(C * la_b)
    bd = _head_ones()
    lane = _iota((1, W_GROUP), 1)

    seq_len = n_chunks * C
    seqs = range(n_seq)
    sf[...] = jnp.zeros_like(sf)
    sb[...] = jnp.zeros_like(sb)
    if has_s0:
        for j in seqs:
            for h in range(N_HEADS):
                sl = slice(D_HEAD * h, D_HEAD * (h + 1))
                sf[j, sl, sl] = s0_ref[j, 0, 0, h]
                sb[j, sl, sl] = s0_ref[j, 0, 1, h]

    def rot(x, cs, sn):
        halves = []
        for a in range(2):
            xh = x[:, 128 * a:128 * (a + 1)]
            lh = _iota(xh.shape, 1)
            halves.append(jnp.where(lh % D_HEAD < D_HEAD // 2, pltpu.roll(xh, 96, 1), pltpu.roll(xh, 32, 1)))
        return x * cs + jnp.concatenate(halves, axis=1) * sn

    def chunk_rows(c):
        return [pl.ds(pl.multiple_of(j * seq_len + c * C, C), C) for j in seqs]

    def fwd(c, carry):
        rows = chunk_rows(c)
        q = [p_ref[r, 0:256] for r in rows]
        k = [p_ref[r, 256:512] * (D_HEAD ** -0.5) for r in rows]
        v = [p_ref[r, 512:768] for r in rows]
        if rope:
            pos = pl.ds(pl.multiple_of(c * C, C), C)
            cs, sn = cos_ref[pos, :], sin_ref[pos, :]
            q = [rot(x, cs, sn) for x in q]
            k = [rot(x, cs, sn) for x in k]
        for r, x, y in zip(rows, q, k):
            qs[r, :] = x
            ks[r, :] = y
        o = [_dot(x * etab[0], sf[j]) for j, x in zip(seqs, q)]
        for h in range(N_HEADS):
            mh = jnp.where(lane // D_HEAD == h, 1.0, 0.0)
            att = [_dot(x * mh, y, NT) * dmat[h] for x, y in zip(q, k)]
            o = [o_ + _dot(a, z * mh) for o_, a, z in zip(o, att, v)]
        for r, o_ in zip(rows, o):
            oscr[r, :] = o_
        for j, y, z in zip(seqs, k, v):
            sf[j] = sf[j] * dc_f + bd * _dot(y * etab[2], z, TN)
        return carry

    lax.fori_loop(0, n_chunks, fwd, 0)

    def bwd(t, carry):
        rows = chunk_rows(n_chunks - 1 - t)
        q, k = [qs[r, :] for r in rows], [ks[r, :] for r in rows]
        v = [p_ref[r, 512:768] for r in rows]
        o = [oscr[r, :] + _dot(x * etab[1], sb[j]) for r, j, x in zip(rows, seqs, q)]
        for j, y, z in zip(seqs, k, v):
            sb[j] = sb[j] * dc_b + bd * _dot(y * etab[3], z, TN)
        ms = [_dot01(bd, o_ * o_, m_left=False) * (1.0 / D_HEAD) for o_ in o]
        for r, o_, m in zip(rows, o, ms):
            o_ref[r, :] = o_ * lax.rsqrt(m + EPS) * nw_ref[...] * _silu(p_ref[r, 768:1024])
        return carry

    lax.fori_loop(0, n_chunks, bwd, 0)

    for j in seqs:
        for h in range(N_HEADS):
            sl = slice(D_HEAD * h, D_HEAD * (h + 1))
            st_ref[j, 0, 0, h] = sf[j, sl, sl]
            st_ref[j, 0, 1, h] = sb[j, sl, sl]


def _state_target(acc, bsz, n_seq, layer, tail, args, in_specs, out_index):
    if acc is None:
        shape = jax.ShapeDtypeStruct((bsz, 1, 2) + tail, F32)
        return shape, pl.BlockSpec((n_seq, 1, 2) + tail, lambda i, *_: (i, 0, 0, 0, 0, 0)), {}
    alias = {len(args): out_index}
    args.append(acc)
    in_specs.append(pl.BlockSpec(memory_space=pl.ANY))
    shape = jax.ShapeDtypeStruct(acc.shape, F32)
    return shape, pl.BlockSpec((n_seq, 1, 2) + tail, lambda i, *_: (i, layer, 0, 0, 0, 0)), alias


def _retention(p_ret, la_lane, nw, bsz, seq_len, rope_tabs, s0, layer, acc):
    n_chunks = seq_len // SCAN_CHUNK
    n_seq = RET_SEQS
    rows = n_seq * seq_len
    rope = rope_tabs is not None
    has_s0 = s0 is not None
    big = dict(pipeline_mode=pl.Buffered(1)) if rows * 1024 * 4 > (8 << 20) else {}
    const = lambda a, b: pl.BlockSpec((a, b), lambda i: (0, 0))
    in_specs = [pl.BlockSpec((rows, 1024), lambda i: (i, 0), **big), const(2, W_GROUP), const(1, W_GROUP)]
    args = [p_ret, la_lane, nw]
    if rope:
        in_specs += [pl.BlockSpec((seq_len, W_GROUP), lambda i: (0, 0), **big)] * 2
        args += list(rope_tabs)
    if has_s0:
        in_specs.append(pl.BlockSpec((n_seq, 1, 2, N_HEADS, D_HEAD, D_HEAD), lambda i: (i, layer, 0, 0, 0, 0)))
        args.append(s0)
    st_shape, st_spec, alias = _state_target(acc, bsz, n_seq, layer, (N_HEADS, D_HEAD, D_HEAD), args, in_specs, 1)
    return pl.pallas_call(
        functools.partial(_ret_kernel, n_chunks, n_seq, rope, has_s0, acc is not None),
        out_shape=(jax.ShapeDtypeStruct((bsz * seq_len, W_GROUP), F32), st_shape),
        grid=(bsz // n_seq,),
        in_specs=in_specs,
        out_specs=(pl.BlockSpec((rows, W_GROUP), lambda i: (i, 0)), st_spec),
        input_output_aliases=alias,
        scratch_shapes=[pltpu.VMEM((n_seq, W_GROUP, W_GROUP), F32), pltpu.VMEM((n_seq, W_GROUP, W_GROUP), F32),
                        pltpu.VMEM((N_HEADS, SCAN_CHUNK, SCAN_CHUNK), F32),
                        pltpu.VMEM((4, SCAN_CHUNK, W_GROUP), F32),
                        pltpu.VMEM((rows, W_GROUP), F32), pltpu.VMEM((rows, W_GROUP), F32),
                        pltpu.VMEM((rows, W_GROUP), F32)],
        compiler_params=_params(("arbitrary",)),
        name="retention",
    )(*args)


def _ssd_kernel(n_chunks, n_seq, has_s0, has_acc, *refs):
    C = SCAN_CHUNK
    refs = list(refs)
    p_ref, ps_ref, cw_ref, cb_ref, dtb_ref, alog_ref, dl_ref, nw_ref = refs[:8]
    refs = refs[8:]
    if has_s0:
        s0_ref = refs[0]
        refs = refs[1:]
    if has_acc:
        refs = refs[1:]
    o_ref, st_ref, sf, sb, xs_s, bm_s, cm_s, oscr = refs
    seq_len = n_chunks * C
    seqs = range(n_seq)

    def prep(c, carry):
        r0 = pl.multiple_of(c * C, C)
        cs = lax.rem(c, n_chunks)
        y = _conv3_chunk(p_ref, r0, C, n_seq * seq_len, 0, 768, cw_ref, cs == 0, cs == n_chunks - 1) + cb_ref[...]
        y = _silu(y)
        rows = pl.ds(r0, C)
        xs_s[rows, :] = y[:, 0:256]
        bm_s[rows, :] = y[:, 256:512]
        cm_s[rows, :] = y[:, 512:768]
        return carry

    lax.fori_loop(0, n_seq * n_chunks, prep, 0)

    sf[...] = jnp.zeros_like(sf)
    sb[...] = jnp.zeros_like(sb)
    if has_s0:
        for j in seqs:
            for h in range(N_HEADS):
                g = h // 2
                sf[j, N_SSD * g:N_SSD * (g + 1), D_HEAD * h:D_HEAD * (h + 1)] = s0_ref[j, 0, 0, h]
                sb[j, N_SSD * g:N_SSD * (g + 1), D_HEAD * h:D_HEAD * (h + 1)] = s0_ref[j, 0, 1, h]

    ltri, utri = _tri(C, True), _tri(C, False)
    exp_f, exp_b = _lane_expand(16, D_HEAD), _lane_expand(20, D_HEAD)
    gmask = _head_ones(W_GROUP, N_SSD)
    lane_s = _iota((1, SMALL_W), 1)
    dt_mask = jnp.where((lane_s >= 16) & (lane_s < 24), 1.0, 0.0)
    lane = _iota((1, W_GROUP), 1)
    ii, jj = _iota((C, C), 0), _iota((C, C), 1)

    def chunk_rows(c):
        return [pl.ds(pl.multiple_of(j * seq_len + c * C, C), C) for j in seqs]

    def gates(rows):
        dt = [_softplus(ps_ref[r, :] + dtb_ref[...]) * dt_mask for r in rows]
        la = [x * (-jnp.exp(alog_ref[...])) for x in dt]
        return dt, la

    def fwd(c, carry):
        rows = chunk_rows(c)
        dt, la = gates(rows)
        cum_p = [_dot01(ltri, x) for x in la]
        cum_s = [_dot01(utri, x) for x in la]
        cum_pt, cum_st, dt_t = [x.T for x in cum_p], [x.T for x in cum_s], [x.T for x in dt]
        x_f = [_dot01(exp_f, x, m_left=False) for x in cum_p]
        dt_f = [_dot01(exp_f, x, m_left=False) for x in dt]
        xs, bm, cm = [xs_s[r, :] for r in rows], [bm_s[r, :] for r in rows], [cm_s[r, :] for r in rows]
        s_g = [[_dot(c_[:, N_SSD * g:N_SSD * (g + 1)], b_[:, N_SSD * g:N_SSD * (g + 1)], NT) for g in range(G_SSD)]
               for c_, b_ in zip(cm, bm)]
        o = [_dot(c_, sf[j]) * jnp.exp(x) for j, c_, x in zip(seqs, cm, x_f)]
        for h in range(N_HEADS):
            f, b = 16 + h, 20 + h
            mh = jnp.where(lane // D_HEAD == h, 1.0, 0.0)
            dec = [(jnp.exp(jnp.where(jj <= ii, cp[:, f:f + 1] - cpt[f:f + 1, :], NEG_BIG)) * dtt[f:f + 1, :]
                    + jnp.exp(jnp.where(jj >= ii, cs_[:, b:b + 1] - cst[b:b + 1, :], NEG_BIG)) * dtt[b:b + 1, :])
                   for cp, cpt, cs_, cst, dtt in zip(cum_p, cum_pt, cum_s, cum_st, dt_t)]
            o = [o_ + _dot(sg[h // 2] * d, x * mh) for o_, sg, d, x in zip(o, s_g, dec, xs)]
        for r, o_ in zip(rows, o):
            oscr[r, :] = o_
        last = [x[C - 1:C, :] for x in x_f]
        for j, l, b_, x, d, xf in zip(seqs, last, bm, xs, dt_f, x_f):
            sf[j] = sf[j] * jnp.exp(l) + gmask * _dot(b_, x * d * jnp.exp(l - xf), TN)
        return carry

    lax.fori_loop(0, n_chunks, fwd, 0)

    def bwd(t, carry):
        rows = chunk_rows(n_chunks - 1 - t)
        dt, la = gates(rows)
        cum_s = [_dot01(utri, x) for x in la]
        x_b = [_dot01(exp_b, x, m_left=False) for x in cum_s]
        dt_b = [_dot01(exp_b, x, m_left=False) for x in dt]
        xs, bm, cm = [xs_s[r, :] for r in rows], [bm_s[r, :] for r in rows], [cm_s[r, :] for r in rows]
        o = [oscr[r, :] + _dot(c_, sb[j]) * jnp.exp(x) for r, j, c_, x in zip(rows, seqs, cm, x_b)]
        first = [x[0:1, :] for x in x_b]
        for j, l, b_, x, d, xb in zip(seqs, first, bm, xs, dt_b, x_b):
            sb[j] = sb[j] * jnp.exp(l) + gmask * _dot(b_, x * d * jnp.exp(l - xb), TN)
        for r, o_, x in zip(rows, o, xs):
            y = (o_ + x * dl_ref[...]) * _silu(p_ref[r, 768:1024])
            halves = [y[:, 128 * g:128 * (g + 1)] for g in range(G_SSD)]
            halves = [yh * lax.rsqrt(jnp.mean(yh * yh, axis=-1, keepdims=True) + EPS) for yh in halves]
            o_ref[r, :] = jnp.concatenate(halves, axis=1) * nw_ref[...]
        return carry

    lax.fori_loop(0, n_chunks, bwd, 0)

    for j in seqs:
        for h in range(N_HEADS):
            g = h // 2
            st_ref[j, 0, 0, h] = sf[j, N_SSD * g:N_SSD * (g + 1), D_HEAD * h:D_HEAD * (h + 1)]
            st_ref[j, 0, 1, h] = sb[j, N_SSD * g:N_SSD * (g + 1), D_HEAD * h:D_HEAD * (h + 1)]


def _ssd(p_ssd, p_small, conv_w, conv_b, dtb, alog, d_lane, nw, bsz, seq_len, s0, layer, acc):
    n_chunks = seq_len // SCAN_CHUNK
    n_seq = SSD_SEQS
    rows = n_seq * seq_len
    has_s0 = s0 is not None
    big = dict(pipeline_mode=pl.Buffered(1)) if rows * 1024 * 4 > (8 << 20) else {}
    const = lambda a, b: pl.BlockSpec((a, b), lambda i: (0, 0))
    in_specs = [pl.BlockSpec((rows, 1024), lambda i: (i, 0), **big),
                pl.BlockSpec((rows, SMALL_W), lambda i: (i, 0), **big),
                const(3, 768), const(1, 768), const(1, SMALL_W), const(1, SMALL_W), const(1, W_GROUP),
                const(1, W_GROUP)]
    args = [p_ssd, p_small, conv_w, conv_b, dtb, alog, d_lane, nw]
    if has_s0:
        in_specs.append(pl.BlockSpec((n_seq, 1, 2, N_HEADS, N_SSD, D_HEAD), lambda i: (i, layer, 0, 0, 0, 0)))
        args.append(s0)
    st_shape, st_spec, alias = _state_target(acc, bsz, n_seq, layer, (N_HEADS, N_SSD, D_HEAD), args, in_specs, 1)
    return pl.pallas_call(
        functools.partial(_ssd_kernel, n_chunks, n_seq, has_s0, acc is not None),
        out_shape=(jax.ShapeDtypeStruct((bsz * seq_len, W_GROUP), F32), st_shape),
        grid=(bsz // n_seq,),
        in_specs=in_specs,
        out_specs=(pl.BlockSpec((rows, W_GROUP), lambda i: (i, 0)), st_spec),
        input_output_aliases=alias,
        scratch_shapes=[pltpu.VMEM((n_seq, W_GROUP, W_GROUP), F32), pltpu.VMEM((n_seq, W_GROUP, W_GROUP), F32)]
        + [pltpu.VMEM((rows, W_GROUP), F32)] * 4,
        compiler_params=_params(("arbitrary",)),
        name="ssd",
    )(*args)


def _gdn_prep_kernel(n_pre, p_ref, cw_ref, q_ref, k_ref, v_ref):
    CP = SCAN_CHUNK
    bd = _head_ones()

    def prep(c, carry):
        r0 = pl.multiple_of(c * CP, CP)
        y = _silu(_conv3_chunk(p_ref, r0, CP, n_pre * CP, 0, 768, cw_ref, c == 0, c == n_pre - 1))
        q, k, v = y[:, 0:256], y[:, 256:512], y[:, 512:768]
        rows = pl.ds(r0, CP)
        q_ref[rows, :] = q * lax.rsqrt(_dot01(bd, q * q, m_left=False) + EPS) * (D_HEAD ** -0.5)
        k_ref[rows, :] = k * lax.rsqrt(_dot01(bd, k * k, m_left=False) + EPS)
        v_ref[rows, :] = v
        return carry

    lax.fori_loop(0, n_pre, prep, 0)


def _gdn_scan_kernel(n_seq, n_blk, has_s0, has_acc, *refs):
    C = GDN_CHUNK
    R = N_HEADS * C
    SUB = SCAN_CHUNK // C
    refs = list(refs)
    qf, kf, vf, gf, qb, kb, vb, gb, dtb_ref, alog_ref = refs[:10]
    refs = refs[10:]
    if has_s0:
        s0_ref = refs[0]
        refs = refs[1:]
    if has_acc:
        refs = refs[1:]
    of_ref, ob_ref, st_ref, sf, sb = refs
    t = pl.program_id(1)

    @pl.when(t == 0)
    def _():
        sf[...] = jnp.zeros_like(sf)
        sb[...] = jnp.zeros_like(sb)
        if has_s0:
            for j in range(n_seq):
                for h in range(N_HEADS):
                    sl = slice(D_HEAD * h, D_HEAD * (h + 1))
                    sf[j, sl, sl] = s0_ref[j, 0, 0, h]
                    sb[j, sl, sl] = s0_ref[j, 0, 1, h]

    lane_s = _iota((1, SMALL_W), 1)
    la_mask = jnp.where(lane_s < 8, 1.0, 0.0)
    same = (_iota((R, R), 0) // C) == (_iota((R, R), 1) // C)
    lane_head = _iota((1, W_GROUP), 1) // D_HEAD
    pad_rows = jnp.zeros((SMALL_W - C, SMALL_W), F32)
    ic = _iota((C, R), 0)
    jc = _iota((C, R), 1) % C
    zero_b = jnp.zeros((R, R), BF16)

    def expand(x):
        xb = x.astype(BF16)
        return jnp.where(same, jnp.concatenate([xb] * N_HEADS, axis=0), zero_b)

    def per_lane(x, lane0):
        cols = [jnp.broadcast_to(x[:, lane0 + h:lane0 + h + 1], (C, W_GROUP)) for h in range(N_HEADS)]
        return jnp.where(lane_head == 0, cols[0],
                         jnp.where(lane_head == 1, cols[1], jnp.where(lane_head == 2, cols[2], cols[3])))

    def chunks(streams):
        seqs, offs, dirs = zip(*streams)
        fwd = [d == 0 for d in dirs]
        base = [4 * d for d in dirs]
        rows = [pl.ds(pl.multiple_of(r0, C), C) for r0 in offs]
        pick = lambda a_f, a_b: [(a_f if f else a_b) for f in fwd]
        q_refs, k_refs, v_refs, g_refs = pick(qf, qb), pick(kf, kb), pick(vf, vb), pick(gf, gb)
        s_refs, o_refs = pick(sf, sb), pick(of_ref, ob_ref)
        g = [r[j, rw, :] for r, j, rw in zip(g_refs, seqs, rows)]
        la = [-jnp.exp(alog_ref[...]) * _softplus(x + dtb_ref[...]) * la_mask for x in g]
        beta = [_sigmoid(x) for x in g]
        cum = [_dot01(_tri(C, f), x) for f, x in zip(fwd, la)]
        last_row = [x[C - 1:C, :] if f else x[0:1, :] for f, x in zip(fwd, cum)]
        cum_t = [jnp.concatenate([x, pad_rows], axis=0).T for x in cum]
        cum_row = []
        for x, b in zip(cum_t, base):
            pc = [x[b + h:b + h + 1, :] for h in range(N_HEADS)]
            cum_row.append(jnp.concatenate([pc[0] + pltpu.roll(pc[1], C, 1), pc[2] + pltpu.roll(pc[3], C, 1)],
                                           axis=1))
        cum_n = [per_lane(x, b) for x, b in zip(cum, base)]
        beta_n = [per_lane(x, 8 + b) for x, b in zip(beta, base)]
        last_lane = []
        for l, b in zip(last_row, base):
            ll = jnp.zeros((1, W_GROUP), F32)
            for h in range(N_HEADS):
                ll = jnp.where(lane_head == h, l[:, b + h:b + h + 1], ll)
            last_lane.append(ll)
        ecum_n = [jnp.exp(x) for x in cum_n]
        kdec_n = [jnp.exp(l - x) for l, x in zip(last_lane, cum_n)]
        incl = [(jc <= ic) if f else (jc >= ic) for f in fwd]
        strict = [(jc < ic) if f else (jc > ic) for f in fwd]
        dec = [jnp.exp(jnp.where(m, x - y, NEG_BIG)) for m, x, y in zip(incl, cum_n, cum_row)]
        q = [r[j, rw, :] for r, j, rw in zip(q_refs, seqs, rows)]
        k = [r[j, rw, :] for r, j, rw in zip(k_refs, seqs, rows)]
        v = [r[j, rw, :] for r, j, rw in zip(v_refs, seqs, rows)]
        k_x = [expand(x) for x in k]
        kk = [_dot(x, y, NT) for x, y in zip(k, k_x)]
        qk = [_dot(x, y, NT) for x, y in zip(q, k_x)]
        a = [jnp.where(m, d * x, 0.0) * b for m, d, x, b in zip(strict, dec, kk, beta_n)]
        n = [-jnp.where((ic // 2) == (jc // 2), x, 0.0) for x in a]
        s = 2
        while s < C:
            blk = ((ic // (2 * s)) == (jc // (2 * s))) & ((ic // s) != (jc // s))
            lo = [jnp.where(blk, x, 0.0) for x in a]
            y = [x + _dot(x, expand(m)) for x, m in zip(lo, n)]
            n = [m - (x + _dot(m, expand(x))) for m, x in zip(n, y)]
            s *= 2
        kp = [x * (b * e) for x, b, e in zip(k, beta_n, ecum_n)]
        vp = [x * b for x, b in zip(v, beta_n)]
        wm = [x + _dot(m, expand(x)) for m, x in zip(n, kp)]
        um = [x + _dot(m, expand(x)) for m, x in zip(n, vp)]
        st = [r[j] for r, j in zip(s_refs, seqs)]
        v_new = [u - _dot(w_, s_) for u, w_, s_ in zip(um, wm, st)]
        o = [_dot(x * e, s_) + _dot(y * d, expand(vn))
             for x, e, s_, y, d, vn in zip(q, ecum_n, st, qk, dec, v_new)]
        for j, rw, o_r, x in zip(seqs, rows, o_refs, o):
            o_r[j, rw, :] = x
        for s_r, j, s_, l, x, kd, vn in zip(s_refs, seqs, st, last_lane, k, kdec_n, v_new):
            s_r[j] = s_ * jnp.exp(l) + jnp.where(same, _dot(x * kd, vn, TN), 0.0)

    def scan_body(u, carry):
        streams = []
        for j in range(n_seq):
            streams += [(j, u * C, 0), (j, (SUB - 1 - u) * C, 1)]
        chunks(streams)
        return carry

    lax.fori_loop(0, SUB, scan_body, 0)

    @pl.when(t == n_blk - 1)
    def _():
        for j in range(n_seq):
            for h in range(N_HEADS):
                sl = slice(D_HEAD * h, D_HEAD * (h + 1))
                st_ref[j, 0, 0, h] = sf[j, sl, sl]
                st_ref[j, 0, 1, h] = sb[j, sl, sl]


def _gdn(p_gdn, p_small, conv_w, dtb, alog, bsz, seq_len, s0, layer, acc):
    n_tok = bsz * seq_len
    n_blk = seq_len // SCAN_CHUNK
    n_seq = min(GDN_SEQS_MAX, bsz)
    has_s0 = s0 is not None
    tok = jax.ShapeDtypeStruct((n_tok, W_GROUP), F32)

    q, k, v = pl.pallas_call(
        functools.partial(_gdn_prep_kernel, n_blk),
        out_shape=(tok, tok, tok),
        grid=(bsz,),
        in_specs=[pl.BlockSpec((seq_len, 768), lambda i: (i, 0)), pl.BlockSpec((3, 768), lambda i: (0, 0))],
        out_specs=(pl.BlockSpec((seq_len, W_GROUP), lambda i: (i, 0)),) * 3,
        compiler_params=_params(("arbitrary",)),
        name="gated_delta_prep",
    )(p_gdn, conv_w)

    seq3 = lambda a: a.reshape(bsz, seq_len, a.shape[-1])
    fwd_blk = lambda w: pl.BlockSpec((n_seq, SCAN_CHUNK, w), lambda i, t: (i, t, 0))
    bwd_blk = lambda w: pl.BlockSpec((n_seq, SCAN_CHUNK, w), lambda i, t: (i, n_blk - 1 - t, 0))
    const = lambda a, b: pl.BlockSpec((a, b), lambda i, t: (0, 0))
    q3, k3, v3, g3 = seq3(q), seq3(k), seq3(v), seq3(p_small)
    in_specs = [fwd_blk(W_GROUP)] * 3 + [fwd_blk(SMALL_W)] + [bwd_blk(W_GROUP)] * 3 + [bwd_blk(SMALL_W)] \
        + [const(1, SMALL_W), const(1, SMALL_W)]
    args = [q3, k3, v3, g3, q3, k3, v3, g3, dtb, alog]
    if has_s0:
        in_specs.append(pl.BlockSpec((n_seq, 1, 2, N_HEADS, D_HEAD, D_HEAD), lambda i, t: (i, layer, 0, 0, 0, 0)))
        args.append(s0)
    seq_out = jax.ShapeDtypeStruct((bsz, seq_len, W_GROUP), F32)
    st_shape, st_spec, alias = _state_target(acc, bsz, n_seq, layer, (N_HEADS, D_HEAD, D_HEAD), args, in_specs, 2)
    o_f, o_b, state = pl.pallas_call(
        functools.partial(_gdn_scan_kernel, n_seq, n_blk, has_s0, acc is not None),
        out_shape=(seq_out, seq_out, st_shape),
        grid=(bsz // n_seq, n_blk),
        in_specs=in_specs,
        out_specs=(fwd_blk(W_GROUP), bwd_blk(W_GROUP), st_spec),
        input_output_aliases=alias,
        scratch_shapes=[pltpu.VMEM((n_seq, W_GROUP, W_GROUP), F32), pltpu.VMEM((n_seq, W_GROUP, W_GROUP), F32)],
        compiler_params=_params(("arbitrary", "arbitrary")),
        name="gated_delta_scan",
    )(*args)
    return o_f.reshape(n_tok, W_GROUP), o_b.reshape(n_tok, W_GROUP), state


def _dft_tables(seq_len):
    period = 8 * seq_len
    theta = math.pi / (4 * seq_len)
    fine, coarse = 64, seq_len // 64
    odd = 2 * jnp.arange(seq_len, dtype=jnp.int32) + 1
    ka = ((2 * fine * jnp.arange(coarse, dtype=jnp.int32))[:, None] * odd[None, :]) % period
    kb = (odd[:fine, None] * odd[None, :]) % period
    ang_a = ka.astype(F32)[:, None, :] * theta
    ang_b = kb.astype(F32)[None, :, :] * theta
    ca, sa, cb, sb = jnp.cos(ang_a), jnp.sin(ang_a), jnp.cos(ang_b), jnp.sin(ang_b)
    c2 = (ca * cb - sa * sb).reshape(seq_len, seq_len).astype(BF16)
    s2 = (sa * cb + ca * sb).reshape(seq_len, seq_len).astype(BF16)
    half = odd.astype(F32) * theta
    return c2, s2, jnp.stack([jnp.cos(half), jnp.sin(half)], axis=1)


def _filter_features(seq_len):
    t = jnp.linspace(0.0, 1.0, seq_len, dtype=F32)[:, None]
    bands = (HY_EMB - 1) // 2
    w = 2.0 * math.pi * jnp.arange(seq_len, dtype=F32)[:, None] / seq_len
    f = jnp.linspace(1e-4, bands - 1, bands, dtype=F32)[None, :]
    z = jnp.concatenate([t, jnp.cos(f * w), -jnp.sin(f * w)], axis=-1)
    z = jnp.pad(z, ((0, 0), (0, 128 - HY_EMB)))
    deltas = jnp.abs(jnp.linspace(math.log(HY_DECAY_TARGET) / HY_SLOW_PCT,
                                  math.log(HY_DECAY_TARGET) / HY_FAST_PCT, W_GROUP, dtype=F32))
    return z, jnp.exp(-t * deltas)


def _hyena_kernel(n_chunks, p_ref, cw_ref, cb_ref, bias_ref, nw_ref, z_ref, dec_ref, fr_ref, w1_ref, b1_ref,
                  w2_ref, b2_ref, w3_ref, c2_ref, s2_ref, ph_ref, o_ref,
                  hre_s, him_s, x0_s, u_s, ub_s, a_s, b_s, yre_s, yim_s):
    C = SCAN_CHUNK
    seq_len = n_chunks * C

    @pl.when(pl.program_id(0) == 0)
    def _():
        fr = fr_ref[0]

        def taps_chunk(c, carry):
            r0 = pl.multiple_of(c * C, C)
            rows = pl.ds(r0, C)
            h = jnp.sin(fr * (_dot_hp(z_ref[rows, :], w1_ref[0]) + b1_ref[0]))
            h = jnp.sin(fr * (_dot_hp(h, w2_ref[0]) + b2_ref[0]))
            taps = _dot_hp(h, w3_ref[0])
            hf = taps[:, 0:W_GROUP] * dec_ref[rows, :]
            hb = taps[:, W_GROUP:2 * W_GROUP] * dec_ref[rows, :]
            hb = jnp.where(_iota(hb.shape, 0) + r0 == 0, 0.0, hb)
            yre_s[rows, :] = (hf + hb).astype(BF16)
            yim_s[rows, :] = (hf - hb).astype(BF16)
            return carry

        lax.fori_loop(0, n_chunks, taps_chunk, 0)
        a_s[...] = _dot(c2_ref[...], yre_s[...])
        b_s[...] = _dot(s2_ref[...], yre_s[...])

        def hre_chunk(c, carry):
            rows = pl.ds(pl.multiple_of(c * C, C), C)
            hre_s[rows, :] = ph_ref[rows, 0:1] * a_s[rows, :] + ph_ref[rows, 1:2] * b_s[rows, :]
            return carry

        lax.fori_loop(0, n_chunks, hre_chunk, 0)
        a_s[...] = _dot(c2_ref[...], yim_s[...])
        b_s[...] = _dot(s2_ref[...], yim_s[...])

        def him_chunk(c, carry):
            rows = pl.ds(pl.multiple_of(c * C, C), C)
            him_s[rows, :] = ph_ref[rows, 1:2] * a_s[rows, :] - ph_ref[rows, 0:1] * b_s[rows, :]
            return carry

        lax.fori_loop(0, n_chunks, him_chunk, 0)

    def prep(c, carry):
        r0 = pl.multiple_of(c * C, C)
        y = _conv3_chunk(p_ref, r0, C, seq_len, 0, 768, cw_ref, c == 0, c == n_chunks - 1) + cb_ref[...]
        rows = pl.ds(r0, C)
        u = y[:, 512:768] * y[:, 256:512]
        x0_s[rows, :] = y[:, 0:256]
        u_s[rows, :] = u
        ub_s[rows, :] = u.astype(BF16)
        return carry

    lax.fori_loop(0, n_chunks, prep, 0)

    a_s[...] = _dot(c2_ref[...], ub_s[...])
    b_s[...] = _dot(s2_ref[...], ub_s[...])

    def spectrum(c, carry):
        rows = pl.ds(pl.multiple_of(c * C, C), C)
        ure, su, hre, him = a_s[rows, :], b_s[rows, :], hre_s[rows, :], him_s[rows, :]
        yre_s[rows, :] = (ure * hre + su * him).astype(BF16)
        yim_s[rows, :] = (ure * him - su * hre).astype(BF16)
        return carry

    lax.fori_loop(0, n_chunks, spectrum, 0)
    a_s[...] = _dot(c2_ref[...], yre_s[...])
    b_s[...] = _dot(s2_ref[...], yim_s[...])

    def fin(c, carry):
        rows = pl.ds(pl.multiple_of(c * C, C), C)
        conv = (a_s[rows, :] - b_s[rows, :]) * (1.0 / seq_len)
        y = x0_s[rows, :] * (conv + u_s[rows, :] * bias_ref[...])
        o_ref[rows, :] = _rms(y, nw_ref[...])
        return carry

    lax.fori_loop(0, n_chunks, fin, 0)


def _hyena(p_hy, conv_w, conv_b, bias, nw, tabs, feats, freq, w1, b1, w2, b2, w3, bsz, seq_len, layer):
    c2, s2, phase = tabs
    z, dec = feats
    one = dict(pipeline_mode=pl.Buffered(1)) if seq_len > 1024 else {}
    const = lambda a, b: pl.BlockSpec((a, b), lambda i: (0, 0), **one)
    lay = lambda a, b: pl.BlockSpec((1, a, b), lambda i: (layer, 0, 0))
    acc = pltpu.VMEM((seq_len, W_GROUP), F32)
    half = pltpu.VMEM((seq_len, W_GROUP), BF16)
    return pl.pallas_call(
        functools.partial(_hyena_kernel, seq_len // SCAN_CHUNK),
        out_shape=jax.ShapeDtypeStruct((bsz * seq_len, W_GROUP), F32),
        grid=(bsz,),
        in_specs=[pl.BlockSpec((seq_len, 768), lambda i: (i, 0), **one), const(3, 768), const(1, 768),
                  const(1, W_GROUP), const(1, W_GROUP), const(seq_len, 128), const(seq_len, W_GROUP),
                  lay(1, HY_HIDDEN), lay(128, HY_HIDDEN), lay(1, HY_HIDDEN), lay(HY_HIDDEN, HY_HIDDEN),
                  lay(1, HY_HIDDEN), lay(HY_HIDDEN, 2 * W_GROUP),
                  const(seq_len, seq_len), const(seq_len, seq_len), const(seq_len, 2)],
        out_specs=pl.BlockSpec((seq_len, W_GROUP), lambda i: (i, 0)),
        scratch_shapes=[acc, acc, acc, acc, half, acc, acc, half, half],
        compiler_params=_params(("arbitrary",)),
        name="hyena",
    )(p_hy, conv_w, conv_b, bias, nw, z, dec, freq, w1, b1, w2, b2, w3, c2, s2, phase)


def _rope_tables(seq_len):
    rows = seq_len // GRID_W
    r = jnp.repeat(jnp.arange(rows), GRID_W).astype(F32)
    col = jnp.tile(jnp.arange(GRID_W), rows).astype(F32)
    quarter = D_HEAD // 4
    inv = ROPE_BASE ** (-jnp.arange(quarter, dtype=F32) / quarter)
    ang = jnp.concatenate([r[:, None] * inv, col[:, None] * inv], axis=-1)
    cos, sin = jnp.cos(ang), jnp.sin(ang)
    cos_full = jnp.tile(jnp.concatenate([cos, cos], axis=-1), (1, N_HEADS))
    sin_signed = jnp.tile(jnp.concatenate([-sin, sin], axis=-1), (1, N_HEADS))
    return cos_full, sin_signed


def _small_vec(first_lane, vals):
    flat = vals.reshape(-1)
    return jnp.zeros((1, SMALL_W), F32).at[0, first_lane:first_lane + flat.shape[0]].set(flat)


def _trunk(x, bsz, seq_len, row_fn, states, finals, rope_tabs, hy_tabs, hy_feats, mod3, w):
    st_ret, st_gdn, st_ssd = states
    f_ret, f_gdn, f_ssd = finals
    for l in range(DEPTH):
        p_ret, p_hy, p_gdn, p_ssd, p_small = _in_proj(x, mod3, w['norm1'], w['w_main'], w['w_ssd'],
                                                      w['w_small'], l, row_fn)
        o_ret, s_ret = _retention(p_ret, w['ret_la'][l], w['ret_norm'][l], bsz, seq_len, rope_tabs, st_ret, l,
                                  f_ret)
        o_hy = _hyena(p_hy, w['hy_conv_w'][l], w['hy_conv_b'][l], w['hy_bias'][l], w['hy_norm'][l],
                      hy_tabs, hy_feats, w['hy_freq'], w['hy_w1'], w['hy_b1'], w['hy_w2'], w['hy_b2'],
                      w['hy_w3'], bsz, seq_len, l)
        og_f, og_b, s_gdn = _gdn(p_gdn, p_small, w['gdn_conv_w'][l], w['gdn_dtb'][l], w['gdn_alog'][l],
                                 bsz, seq_len, st_gdn, l, f_gdn)
        o_ssd, s_ssd = _ssd(p_ssd, p_small, w['ssd_conv_w'][l], w['ssd_conv_b'][l], w['ssd_dtb'][l],
                            w['ssd_alog'][l], w['ssd_dl'][l], w['ssd_norm'][l], bsz, seq_len, st_ssd, l, f_ssd)
        x = _out_proj((o_ret, o_hy, og_f, og_b, p_gdn, w['gdn_norm'][l], o_ssd), x, mod3, w['norm2'],
                      w['w_out'], w['mlp_w1'], w['mlp_w2'], w['final_norm'], l, row_fn, l == DEPTH - 1)
        if f_ret is not None:
            f_ret, f_gdn, f_ssd = s_ret, s_gdn, s_ssd
    return x, (f_ret, f_gdn, f_ssd)


def kernel(x_prompt, x_sample, state_ret, state_gdn, state_ssd, c, c_ctx, norm1_w, norm2_w, w_mod, b_mod, w_in, w_out, ret_log_decay, ret_norm_w, hy_conv_w, hy_conv_b, hy_freq, hy_w1, hy_b1, hy_w2, hy_b2, hy_w3, hy_bias, hy_norm_w, gdn_conv_w, gdn_A_log, gdn_dt_bias, gdn_norm_w, ssd_conv_w, ssd_conv_b, ssd_A_log, ssd_dt_bias, ssd_D, ssd_norm_w, mlp_w1, mlp_w2, final_norm_w):
    bsz, seq, _ = x_prompt.shape
    dbsz, dseq, _ = x_sample.shape

    cond8 = jnp.concatenate([c, c_ctx[None, :], jnp.zeros((MOD_ROWS - dbsz - 1, D_MODEL), F32)], axis=0)
    mod = _modulation(cond8, w_mod, b_mod)
    mod3 = mod.reshape(DEPTH * MOD_ROWS * 6, 1, D_MODEL)

    w_main = w_in.astype(BF16)
    w_ssd = w_main[:, :, 2832:3856]
    w_small = jnp.concatenate([w_main[:, :, 2816:2832], w_main[:, :, 3856:3864],
                               jnp.zeros((DEPTH, D_MODEL, SMALL_W - 24), BF16)], axis=-1)
    w = dict(
        norm1=norm1_w.reshape(DEPTH, 1, D_MODEL), norm2=norm2_w.reshape(DEPTH, 1, D_MODEL),
        w_main=w_main, w_ssd=w_ssd, w_small=w_small, w_out=w_out.astype(BF16),
        mlp_w1=mlp_w1.astype(BF16), mlp_w2=mlp_w2.astype(BF16), final_norm=final_norm_w,
        ret_la=[jnp.repeat(ret_log_decay[l], D_HEAD, axis=-1) for l in range(DEPTH)],
        ret_norm=[ret_norm_w[l][None, :] for l in range(DEPTH)],
        hy_freq=hy_freq.reshape(DEPTH, 1, HY_HIDDEN),
        hy_w1=jnp.pad(hy_w1, ((0, 0), (0, 128 - HY_EMB), (0, 0))),
        hy_b1=hy_b1.reshape(DEPTH, 1, HY_HIDDEN), hy_w2=hy_w2, hy_b2=hy_b2.reshape(DEPTH, 1, HY_HIDDEN),
        hy_w3=hy_w3,
        hy_conv_w=hy_conv_w, hy_conv_b=[hy_conv_b[l][None, :] for l in range(DEPTH)],
        hy_bias=[hy_bias[l][None, :] for l in range(DEPTH)], hy_norm=[hy_norm_w[l][None, :] for l in range(DEPTH)],
        gdn_conv_w=gdn_conv_w,
        gdn_dtb=[_small_vec(0, gdn_dt_bias[l]) for l in range(DEPTH)],
        gdn_alog=[_small_vec(0, gdn_A_log[l]) for l in range(DEPTH)],
        gdn_norm=[gdn_norm_w[l][None, :] for l in range(DEPTH)],
        ssd_conv_w=ssd_conv_w, ssd_conv_b=[ssd_conv_b[l][None, :] for l in range(DEPTH)],
        ssd_dtb=[_small_vec(16, ssd_dt_bias[l]) for l in range(DEPTH)],
        ssd_alog=[_small_vec(16, ssd_A_log[l]) for l in range(DEPTH)],
        ssd_dl=[jnp.repeat(ssd_D[l], D_HEAD)[None, :] for l in range(DEPTH)],
        ssd_norm=[ssd_norm_w[l][None, :] for l in range(DEPTH)],
    )

    finals = (jnp.zeros((bsz, DEPTH, 2, N_HEADS, D_HEAD, D_HEAD), F32),
              jnp.zeros((bsz, DEPTH, 2, N_HEADS, D_HEAD, D_HEAD), F32),
              jnp.zeros((bsz, DEPTH, 2, N_HEADS, N_SSD, D_HEAD), F32))
    y_p, (new_ret, new_gdn, new_ssd) = _trunk(
        x_prompt.reshape(bsz * seq, D_MODEL), bsz, seq, lambda row: CTX_MOD_ROW, (None, None, None), finals,
        None, _dft_tables(seq), _filter_features(seq), mod3, w)

    y_s, _ = _trunk(x_sample.reshape(dbsz * dseq, D_MODEL), dbsz, dseq, lambda row: row // dseq,
                    (state_ret, state_gdn, state_ssd), (None, None, None), _rope_tables(dseq),
                    _dft_tables(dseq), _filter_features(dseq), mod3, w)

    return (y_p.reshape(bsz, seq, D_MODEL), y_s.reshape(dbsz, dseq, D_MODEL), new_ret, new_gdn, new_ssd)
```

```python
import functools
import math

import jax
import jax.numpy as jnp
from jax import lax
from jax.experimental import pallas as pl
from jax.experimental.pallas import tpu as pltpu

F32 = jnp.float32
BF16 = jnp.bfloat16

D_MODEL = 1024
DEPTH = 4
GRID_W = 64
W_GROUP = 256
N_HEADS = 4
D_HEAD = 64
N_SSD = 128
G_SSD = 2
HY_EMB = 33
HY_HIDDEN = 64
HY_DECAY_TARGET = 1e-2
HY_FAST_PCT = 0.3
HY_SLOW_PCT = 1.5
D_FF = 4 * D_MODEL
ROPE_BASE = 10000.0
EPS = 1e-6
SMALL_W = 128
MOD_ROWS = 8
CTX_MOD_ROW = 4

TOK_TILE = 512
OUT_TILE = 512
SCAN_CHUNK = 256
GDN_CHUNK = 64
NEG_BIG = -1e30
RET_SEQS = 2
SSD_SEQS = 2
GDN_SEQS_MAX = 4
VMEM_LIMIT = 56 * 1024 * 1024

NN = (((1,), (0,)), ((), ()))
NT = (((1,), (1,)), ((), ()))
TN = (((0,), (0,)), ((), ()))


def _dot(a, b, dims=NN):
    return lax.dot_general(a.astype(BF16), b.astype(BF16), dims, preferred_element_type=F32)


def _dot_hp(a, b, dims=NN):
    a_hi, a_lo = _split(a)
    b_hi, b_lo = _split(b)
    return _dot3(a_hi, a_lo, b_hi, b_lo, dims)


def _split(x):
    hi = x.astype(BF16)
    lo = (x - hi.astype(F32)).astype(BF16)
    return hi, lo


def _dot3(a_hi, a_lo, b_hi, b_lo, dims=NN):
    d = functools.partial(lax.dot_general, dimension_numbers=dims, preferred_element_type=F32)
    return d(a_hi, b_hi) + d(a_hi, b_lo) + d(a_lo, b_hi)


def _dot01(m, x, terms=3, m_left=True):
    mb = m.astype(BF16)
    acc, r = None, x.astype(F32)
    for _ in range(terms):
        t = r.astype(BF16)
        p = lax.dot_general(mb, t, NN, preferred_element_type=F32) if m_left else \
            lax.dot_general(t, mb, NN, preferred_element_type=F32)
        acc = p if acc is None else acc + p
        r = r - t.astype(F32)
    return acc


def _sigmoid(x):
    return 1.0 / (1.0 + jnp.exp(-x))


def _silu(x):
    return x * _sigmoid(x)


def _softplus(x):
    return jnp.maximum(x, 0.0) + jnp.log1p(jnp.exp(-jnp.abs(x)))


def _iota(shape, dim):
    return lax.broadcasted_iota(jnp.int32, shape, dim)


def _params(sem):
    return pltpu.CompilerParams(dimension_semantics=sem, vmem_limit_bytes=VMEM_LIMIT)


def _head_ones(n=W_GROUP, width=D_HEAD):
    return jnp.where(_iota((n, n), 0) // width == _iota((n, n), 1) // width, 1.0, 0.0).astype(F32)


def _mod_kernel(cond_ref, w_ref, b_ref, o_ref):
    s = _silu(cond_ref[...])
    o_ref[0] = _dot(s, w_ref[0]) + b_ref[0]


def _modulation(cond8, w_mod, b_mod):
    tn = 1536
    n_out = 6 * D_MODEL
    return pl.pallas_call(
        _mod_kernel,
        out_shape=jax.ShapeDtypeStruct((DEPTH, MOD_ROWS, n_out), F32),
        grid=(DEPTH, n_out // tn),
        in_specs=[
            pl.BlockSpec((MOD_ROWS, D_MODEL), lambda l, j: (0, 0)),
            pl.BlockSpec((1, D_MODEL, tn), lambda l, j: (l, 0, j)),
            pl.BlockSpec((1, 1, tn), lambda l, j: (l, 0, j)),
        ],
        out_specs=pl.BlockSpec((1, MOD_ROWS, tn), lambda l, j: (l, 0, j)),
        compiler_params=_params(("arbitrary", "arbitrary")),
        name="modulation",
    )(cond8, w_mod, b_mod.reshape(DEPTH, 1, n_out))


def _mod_spec(layer, k, row_fn, tile):
    return pl.BlockSpec((1, 1, D_MODEL), lambda i: ((layer * MOD_ROWS + row_fn(i * tile)) * 6 + k, 0, 0))


def _rms(x, w):
    return x * lax.rsqrt(jnp.mean(x * x, axis=-1, keepdims=True) + EPS) * w


def _in_kernel(x_ref, nw_ref, sh_ref, sc_ref, wm_ref, wd_ref, ws_ref, pret, phy, pgdn, pssd, psm):
    h = _rms(x_ref[...], nw_ref[0]) * (1.0 + sc_ref[0]) + sh_ref[0]
    hb = h.astype(BF16)
    d = functools.partial(jnp.dot, preferred_element_type=F32)
    pret[...] = d(hb, wm_ref[0, :, 0:1024])
    phy[...] = d(hb, wm_ref[0, :, 1024:1792])
    pgdn[...] = d(hb, wm_ref[0, :, 1792:2816])
    pssd[...] = d(hb, wd_ref[0])
    psm[...] = d(hb, ws_ref[0])


def _in_proj(x, mod3, norm1_w3, w_main, w_ssd, w_small, layer, row_fn):
    n = x.shape[0]
    widths = (1024, 768, 1024, 1024, SMALL_W)
    return pl.pallas_call(
        _in_kernel,
        out_shape=tuple(jax.ShapeDtypeStruct((n, w), F32) for w in widths),
        grid=(n // TOK_TILE,),
        in_specs=[
            pl.BlockSpec((TOK_TILE, D_MODEL), lambda i: (i, 0)),
            pl.BlockSpec((1, 1, D_MODEL), lambda i: (layer, 0, 0)),
            _mod_spec(layer, 0, row_fn, TOK_TILE),
            _mod_spec(layer, 1, row_fn, TOK_TILE),
            pl.BlockSpec((1, D_MODEL, w_main.shape[-1]), lambda i: (layer, 0, 0), pipeline_mode=pl.Buffered(1)),
            pl.BlockSpec((1, D_MODEL, 1024), lambda i: (layer, 0, 0), pipeline_mode=pl.Buffered(1)),
            pl.BlockSpec((1, D_MODEL, SMALL_W), lambda i: (layer, 0, 0), pipeline_mode=pl.Buffered(1)),
        ],
        out_specs=tuple(pl.BlockSpec((TOK_TILE, w), lambda i: (i, 0)) for w in widths),
        compiler_params=_params(("arbitrary",)),
        name="in_proj",
    )(x, norm1_w3, mod3, mod3, w_main, w_ssd, w_small)


def _out_kernel(final, oret, ohy, ogf, ogb, zg, gnw, ossd, x_ref, g1, sh2, sc2, g2, nw2, wout, w1, w2, *rest):
    if final:
        fnw, y_ref = rest
    else:
        (y_ref,) = rest
    acc = _dot(oret[...], wout[0, 0:256, :])
    acc = acc + _dot(ohy[...], wout[0, 256:512, :])
    acc = acc + _dot(ossd[...], wout[0, 768:1024, :])
    og = ogf[...] + ogb[...]
    ms = _dot01(_head_ones(), og * og, m_left=False) * (1.0 / D_HEAD)
    ogdn = og * lax.rsqrt(ms + EPS) * gnw[...] * _silu(zg[...])
    acc = acc + _dot(ogdn, wout[0, 512:768, :])
    x1 = x_ref[...] + g1[0] * acc
    h2 = _rms(x1, nw2[0]) * (1.0 + sc2[0]) + sh2[0]
    a = _dot(h2, w1[0])
    a = jnp.square(jnp.maximum(a, 0.0))
    x2 = x1 + g2[0] * _dot(a, w2[0])
    if final:
        x2 = _rms(x2, fnw[...])
    y_ref[...] = x2


def _out_proj(mix, x, mod3, norm2_w3, w_out, w1, w2, final_norm_w, layer, row_fn, final):
    n = x.shape[0]
    tok = lambda w: pl.BlockSpec((OUT_TILE, w), lambda i: (i, 0))
    lay = lambda a, b: pl.BlockSpec((1, a, b), lambda i: (layer, 0, 0), pipeline_mode=pl.Buffered(1))
    in_specs = [tok(W_GROUP)] * 4 + [
        pl.BlockSpec((OUT_TILE, W_GROUP), lambda i: (i, 3)), pl.BlockSpec((1, W_GROUP), lambda i: (0, 0)),
        tok(W_GROUP),
        tok(D_MODEL),
        _mod_spec(layer, 2, row_fn, OUT_TILE), _mod_spec(layer, 3, row_fn, OUT_TILE),
        _mod_spec(layer, 4, row_fn, OUT_TILE), _mod_spec(layer, 5, row_fn, OUT_TILE),
        lay(1, D_MODEL), lay(D_MODEL, D_MODEL), lay(D_MODEL, D_FF), lay(D_FF, D_MODEL),
    ]
    args = list(mix) + [x, mod3, mod3, mod3, mod3, norm2_w3, w_out, w1, w2]
    if final:
        in_specs.append(pl.BlockSpec((1, D_MODEL), lambda i: (0, 0)))
        args.append(final_norm_w.reshape(1, D_MODEL))
    return pl.pallas_call(
        functools.partial(_out_kernel, final),
        out_shape=jax.ShapeDtypeStruct((n, D_MODEL), F32),
        grid=(n // OUT_TILE,),
        in_specs=in_specs,
        out_specs=tok(D_MODEL),
        compiler_params=_params(("arbitrary",)),
        name="out_proj_mlp",
    )(*args)


def _conv3_chunk(p_ref, r0, rows, seq_len, c0, width, w_ref, first, last):
    x = p_ref[pl.ds(r0, rows), c0:c0 + width]
    prev_i = jnp.maximum(r0 - 1, 0)
    next_i = jnp.minimum(r0 + rows, seq_len - 1)
    xp = p_ref[pl.ds(prev_i, 1), c0:c0 + width] * jnp.where(first, 0.0, 1.0)
    xn = p_ref[pl.ds(next_i, 1), c0:c0 + width] * jnp.where(last, 0.0, 1.0)
    row = _iota((rows, width), 0)
    x_prev = jnp.where(row == 0, xp, pltpu.roll(x, 1, 0))
    x_next = jnp.where(row == rows - 1, xn, pltpu.roll(x, rows - 1, 0))
    return x_prev * w_ref[0:1, :] + x * w_ref[1:2, :] + x_next * w_ref[2:3, :]


def _tri(n, lower):
    i, j = _iota((n, n), 0), _iota((n, n), 1)
    return jnp.where((j <= i) if lower else (j >= i), 1.0, 0.0).astype(F32)


def _lane_expand(first_row, width):
    r, c = _iota((SMALL_W, W_GROUP), 0), _iota((SMALL_W, W_GROUP), 1)
    return jnp.where(r == first_row + c // width, 1.0, 0.0).astype(F32)


def _ret_kernel(n_chunks, n_seq, rope, has_s0, has_acc, *refs):
    C = SCAN_CHUNK
    refs = list(refs)
    p_ref, la_ref, nw_ref = refs[:3]
    refs = refs[3:]
    if rope:
        cos_ref, sin_ref = refs[:2]
        refs = refs[2:]
    if has_s0:
        s0_ref = refs[0]
        refs = refs[1:]
    if has_acc:
        refs = refs[1:]
    o_ref, st_ref, sf, sb, dmat, etab, qs, ks, oscr = refs

    la_f = la_ref[0:1, :]
    la_b = la_ref[1:2, :]

    @pl.when(pl.program_id(0) == 0)
    def _():
        ri = _iota((C, W_GROUP), 0).astype(F32)
        etab[0] = jnp.exp((ri + 1.0) * la_f)
        etab[1] = jnp.exp((C - ri) * la_b)
        etab[2] = jnp.exp((C - 1.0 - ri) * la_f)
        etab[3] = jnp.exp(ri * la_b)
        ii, jj = _iota((C, C), 0), _iota((C, C), 1)
        diff = (ii - jj).astype(F32)
        for h in range(N_HEADS):
            laf_h = la_ref[0:1, D_HEAD * h:D_HEAD * h + 1]
            lab_h = la_ref[1:2, D_HEAD * h:D_HEAD * h + 1]
            dmat[h] = (jnp.exp(jnp.where(jj <= ii, diff * laf_h, NEG_BIG))
                       + jnp.exp(jnp.where(jj >= ii, -diff * lab_h, NEG_BIG)))

    dc_f = jnp.exp(C * la_f)
    dc_b = jnp.exp(C * la_b)
    bd = _head_ones()
    lane = _iota((1, W_GROUP), 1)

    seq_len = n_chunks * C
    seqs = range(n_seq)
    sf[...] = jnp.zeros_like(sf)
    sb[...] = jnp.zeros_like(sb)
    if has_s0:
        for j in seqs:
            for h in range(N_HEADS):
                sl = slice(D_HEAD * h, D_HEAD * (h + 1))
                sf[j, sl, sl] = s0_ref[j, 0, 0, h]
                sb[j, sl, sl] = s0_ref[j, 0, 1, h]

    def rot(x, cs, sn):
        halves = []
        for a in range(2):
            xh = x[:, 128 * a:128 * (a + 1)]
            lh = _iota(xh.shape, 1)
            halves.append(jnp.where(lh % D_HEAD < D_HEAD // 2, pltpu.roll(xh, 96, 1), pltpu.roll(xh, 32, 1)))
        return x * cs + jnp.concatenate(halves, axis=1) * sn

    def chunk_rows(c):
        return [pl.ds(pl.multiple_of(j * seq_len + c * C, C), C) for j in seqs]

    def fwd(c, carry):
        rows = chunk_rows(c)
        q = [p_ref[r, 0:256] for r in rows]
        k = [p_ref[r, 256:512] * (D_HEAD ** -0.5) for r in rows]
        v = [p_ref[r, 512:768] for r in rows]
        if rope:
            pos = pl.ds(pl.multiple_of(c * C, C), C)
            cs, sn = cos_ref[pos, :], sin_ref[pos, :]
            q = [rot(x, cs, sn) for x in q]
            k = [rot(x, cs, sn) for x in k]
        for r, x, y in zip(rows, q, k):
            qs[r, :] = x
            ks[r, :] = y
        o = [_dot(x * etab[0], sf[j]) for j, x in zip(seqs, q)]
        for h in range(N_HEADS):
            mh = jnp.where(lane // D_HEAD == h, 1.0, 0.0)
            att = [_dot(x * mh, y, NT) * dmat[h] for x, y in zip(q, k)]
            o = [o_ + _dot(a, z * mh) for o_, a, z in zip(o, att, v)]
        for r, o_ in zip(rows, o):
            oscr[r, :] = o_
        for j, y, z in zip(seqs, k, v):
            sf[j] = sf[j] * dc_f + bd * _dot(y * etab[2], z, TN)
        return carry

    lax.fori_loop(0, n_chunks, fwd, 0)

    def bwd(t, carry):
        rows = chunk_rows(n_chunks - 1 - t)
        q, k = [qs[r, :] for r in rows], [ks[r, :] for r in rows]
        v = [p_ref[r, 512:768] for r in rows]
        o = [oscr[r, :] + _dot(x * etab[1], sb[j]) for r, j, x in zip(rows, seqs, q)]
        for j, y, z in zip(seqs, k, v):
            sb[j] = sb[j] * dc_b + bd * _dot(y * etab[3], z, TN)
        ms = [_dot01(bd, o_ * o_, m_left=False) * (1.0 / D_HEAD) for o_ in o]
        for r, o_, m in zip(rows, o, ms):
            o_ref[r, :] = o_ * lax.rsqrt(m + EPS) * nw_ref[...] * _silu(p_ref[r, 768:1024])
        return carry

    lax.fori_loop(0, n_chunks, bwd, 0)

    for j in seqs:
        for h in range(N_HEADS):
            sl = slice(D_HEAD * h, D_HEAD * (h + 1))
            st_ref[j, 0, 0, h] = sf[j, sl, sl]
            st_ref[j, 0, 1, h] = sb[j, sl, sl]


def rows_fit(rows):
    return rows * 1024 * 4 <= (8 << 20)


def _state_target(acc, bsz, n_seq, layer, tail, args, in_specs, out_index):
    if acc is None:
        shape = jax.ShapeDtypeStruct((bsz, 1, 2) + tail, F32)
        return shape, pl.BlockSpec((n_seq, 1, 2) + tail, lambda i, *_: (i, 0, 0, 0, 0, 0)), {}
    alias = {len(args): out_index}
    args.append(acc)
    in_specs.append(pl.BlockSpec(memory_space=pl.ANY))
    shape = jax.ShapeDtypeStruct(acc.shape, F32)
    return shape, pl.BlockSpec((n_seq, 1, 2) + tail, lambda i, *_: (i, layer, 0, 0, 0, 0)), alias


def _retention(p_ret, la_lane, nw, bsz, seq_len, rope_tabs, s0, layer, acc):
    n_chunks = seq_len // SCAN_CHUNK
    n_seq = RET_SEQS if rows_fit(RET_SEQS * seq_len) else 1
    rows = n_seq * seq_len
    rope = rope_tabs is not None
    has_s0 = s0 is not None
    big = {} if rows_fit(rows) else dict(pipeline_mode=pl.Buffered(1))
    const = lambda a, b: pl.BlockSpec((a, b), lambda i: (0, 0))
    in_specs = [pl.BlockSpec((rows, 1024), lambda i: (i, 0), **big), const(2, W_GROUP), const(1, W_GROUP)]
    args = [p_ret, la_lane, nw]
    if rope:
        in_specs += [pl.BlockSpec((seq_len, W_GROUP), lambda i: (0, 0), **big)] * 2
        args += list(rope_tabs)
    if has_s0:
        in_specs.append(pl.BlockSpec((n_seq, 1, 2, N_HEADS, D_HEAD, D_HEAD), lambda i: (i, layer, 0, 0, 0, 0)))
        args.append(s0)
    st_shape, st_spec, alias = _state_target(acc, bsz, n_seq, layer, (N_HEADS, D_HEAD, D_HEAD), args, in_specs, 1)
    return pl.pallas_call(
        functools.partial(_ret_kernel, n_chunks, n_seq, rope, has_s0, acc is not None),
        out_shape=(jax.ShapeDtypeStruct((bsz * seq_len, W_GROUP), F32), st_shape),
        grid=(bsz // n_seq,),
        in_specs=in_specs,
        out_specs=(pl.BlockSpec((rows, W_GROUP), lambda i: (i, 0)), st_spec),
        input_output_aliases=alias,
        scratch_shapes=[pltpu.VMEM((n_seq, W_GROUP, W_GROUP), F32), pltpu.VMEM((n_seq, W_GROUP, W_GROUP), F32),
                        pltpu.VMEM((N_HEADS, SCAN_CHUNK, SCAN_CHUNK), F32),
                        pltpu.VMEM((4, SCAN_CHUNK, W_GROUP), F32),
                        pltpu.VMEM((rows, W_GROUP), F32), pltpu.VMEM((rows, W_GROUP), F32),
                        pltpu.VMEM((rows, W_GROUP), F32)],
        compiler_params=_params(("arbitrary",)),
        name="retention",
    )(*args)


def _ssd_kernel(n_chunks, n_seq, has_s0, has_acc, *refs):
    C = SCAN_CHUNK
    refs = list(refs)
    p_ref, ps_ref, cw_ref, cb_ref, dtb_ref, alog_ref, dl_ref, nw_ref = refs[:8]
    refs = refs[8:]
    if has_s0:
        s0_ref = refs[0]
        refs = refs[1:]
    if has_acc:
        refs = refs[1:]
    o_ref, st_ref, sf, sb, xs_s, bm_s, cm_s, oscr = refs
    seq_len = n_chunks * C
    seqs = range(n_seq)

    def prep(c, carry):
        r0 = pl.multiple_of(c * C, C)
        cs = lax.rem(c, n_chunks)
        y = _conv3_chunk(p_ref, r0, C, n_seq * seq_len, 0, 768, cw_ref, cs == 0, cs == n_chunks - 1) + cb_ref[...]
        y = _silu(y)
        rows = pl.ds(r0, C)
        xs_s[rows, :] = y[:, 0:256]
        bm_s[rows, :] = y[:, 256:512]
        cm_s[rows, :] = y[:, 512:768]
        return carry

    lax.fori_loop(0, n_seq * n_chunks, prep, 0)

    sf[...] = jnp.zeros_like(sf)
    sb[...] = jnp.zeros_like(sb)
    if has_s0:
        for j in seqs:
            for h in range(N_HEADS):
                g = h // 2
                sf[j, N_SSD * g:N_SSD * (g + 1), D_HEAD * h:D_HEAD * (h + 1)] = s0_ref[j, 0, 0, h]
                sb[j, N_SSD * g:N_SSD * (g + 1), D_HEAD * h:D_HEAD * (h + 1)] = s0_ref[j, 0, 1, h]

    ltri, utri = _tri(C, True), _tri(C, False)
    exp_f, exp_b = _lane_expand(16, D_HEAD), _lane_expand(20, D_HEAD)
    gmask = _head_ones(W_GROUP, N_SSD)
    lane_s = _iota((1, SMALL_W), 1)
    dt_mask = jnp.where((lane_s >= 16) & (lane_s < 24), 1.0, 0.0)
    lane = _iota((1, W_GROUP), 1)
    ii, jj = _iota((C, C), 0), _iota((C, C), 1)

    def chunk_rows(c):
        return [pl.ds(pl.multiple_of(j * seq_len + c * C, C), C) for j in seqs]

    def gates(rows):
        dt = [_softplus(ps_ref[r, :] + dtb_ref[...]) * dt_mask for r in rows]
        la = [x * (-jnp.exp(alog_ref[...])) for x in dt]
        return dt, la

    def fwd(c, carry):
        rows = chunk_rows(c)
        dt, la = gates(rows)
        cum_p = [_dot01(ltri, x) for x in la]
        cum_s = [_dot01(utri, x) for x in la]
        cum_pt, cum_st, dt_t = [x.T for x in cum_p], [x.T for x in cum_s], [x.T for x in dt]
        x_f = [_dot01(exp_f, x, m_left=False) for x in cum_p]
        dt_f = [_dot01(exp_f, x, m_left=False) for x in dt]
        xs, bm, cm = [xs_s[r, :] for r in rows], [bm_s[r, :] for r in rows], [cm_s[r, :] for r in rows]
        s_g = [[_dot(c_[:, N_SSD * g:N_SSD * (g + 1)], b_[:, N_SSD * g:N_SSD * (g + 1)], NT) for g in range(G_SSD)]
               for c_, b_ in zip(cm, bm)]
        o = [_dot(c_, sf[j]) * jnp.exp(x) for j, c_, x in zip(seqs, cm, x_f)]
        for h in range(N_HEADS):
            f, b = 16 + h, 20 + h
            mh = jnp.where(lane // D_HEAD == h, 1.0, 0.0)
            dec = [(jnp.exp(jnp.where(jj <= ii, cp[:, f:f + 1] - cpt[f:f + 1, :], NEG_BIG)) * dtt[f:f + 1, :]
                    + jnp.exp(jnp.where(jj >= ii, cs_[:, b:b + 1] - cst[b:b + 1, :], NEG_BIG)) * dtt[b:b + 1, :])
                   for cp, cpt, cs_, cst, dtt in zip(cum_p, cum_pt, cum_s, cum_st, dt_t)]
            o = [o_ + _dot(sg[h // 2] * d, x * mh) for o_, sg, d, x in zip(o, s_g, dec, xs)]
        for r, o_ in zip(rows, o):
            oscr[r, :] = o_
        last = [x[C - 1:C, :] for x in x_f]
        for j, l, b_, x, d, xf in zip(seqs, last, bm, xs, dt_f, x_f):
            sf[j] = sf[j] * jnp.exp(l) + gmask * _dot(b_, x * d * jnp.exp(l - xf), TN)
        return carry

    lax.fori_loop(0, n_chunks, fwd, 0)

    def bwd(t, carry):
        rows = chunk_rows(n_chunks - 1 - t)
        dt, la = gates(rows)
        cum_s = [_dot01(utri, x) for x in la]
        x_b = [_dot01(exp_b, x, m_left=False) for x in cum_s]
        dt_b = [_dot01(exp_b, x, m_left=False) for x in dt]
        xs, bm, cm = [xs_s[r, :] for r in rows], [bm_s[r, :] for r in rows], [cm_s[r, :] for r in rows]
        o = [oscr[r, :] + _dot(c_, sb[j]) * jnp.exp(x) for r, j, c_, x in zip(rows, seqs, cm, x_b)]
        first = [x[0:1, :] for x in x_b]
        for j, l, b_, x, d, xb in zip(seqs, first, bm, xs, dt_b, x_b):
            sb[j] = sb[j] * jnp.exp(l) + gmask * _dot(b_, x * d * jnp.exp(l - xb), TN)
        for r, o_, x in zip(rows, o, xs):
            y = (o_ + x * dl_ref[...]) * _silu(p_ref[r, 768:1024])
            halves = [y[:, 128 * g:128 * (g + 1)] for g in range(G_SSD)]
            halves = [yh * lax.rsqrt(jnp.mean(yh * yh, axis=-1, keepdims=True) + EPS) for yh in halves]
            o_ref[r, :] = jnp.concatenate(halves, axis=1) * nw_ref[...]
        return carry

    lax.fori_loop(0, n_chunks, bwd, 0)

    for j in seqs:
        for h in range(N_HEADS):
            g = h // 2
            st_ref[j, 0, 0, h] = sf[j, N_SSD * g:N_SSD * (g + 1), D_HEAD * h:D_HEAD * (h + 1)]
            st_ref[j, 0, 1, h] = sb[j, N_SSD * g:N_SSD * (g + 1), D_HEAD * h:D_HEAD * (h + 1)]


def _ssd(p_ssd, p_small, conv_w, conv_b, dtb, alog, d_lane, nw, bsz, seq_len, s0, layer, acc):
    n_chunks = seq_len // SCAN_CHUNK
    n_seq = SSD_SEQS if rows_fit(SSD_SEQS * seq_len) else 1
    rows = n_seq * seq_len
    has_s0 = s0 is not None
    big = {} if rows_fit(rows) else dict(pipeline_mode=pl.Buffered(1))
    const = lambda a, b: pl.BlockSpec((a, b), lambda i: (0, 0))
    in_specs = [pl.BlockSpec((rows, 1024), lambda i: (i, 0), **big),
                pl.BlockSpec((rows, SMALL_W), lambda i: (i, 0), **big),
                const(3, 768), const(1, 768), const(1, SMALL_W), const(1, SMALL_W), const(1, W_GROUP),
                const(1, W_GROUP)]
    args = [p_ssd, p_small, conv_w, conv_b, dtb, alog, d_lane, nw]
    if has_s0:
        in_specs.append(pl.BlockSpec((n_seq, 1, 2, N_HEADS, N_SSD, D_HEAD), lambda i: (i, layer, 0, 0, 0, 0)))
        args.append(s0)
    st_shape, st_spec, alias = _state_target(acc, bsz, n_seq, layer, (N_HEADS, N_SSD, D_HEAD), args, in_specs, 1)
    return pl.pallas_call(
        functools.partial(_ssd_kernel, n_chunks, n_seq, has_s0, acc is not None),
        out_shape=(jax.ShapeDtypeStruct((bsz * seq_len, W_GROUP), F32), st_shape),
        grid=(bsz // n_seq,),
        in_specs=in_specs,
        out_specs=(pl.BlockSpec((rows, W_GROUP), lambda i: (i, 0)), st_spec),
        input_output_aliases=alias,
        scratch_shapes=[pltpu.VMEM((n_seq, W_GROUP, W_GROUP), F32), pltpu.VMEM((n_seq, W_GROUP, W_GROUP), F32)]
        + [pltpu.VMEM((rows, W_GROUP), F32)] * 4,
        compiler_params=_params(("arbitrary",)),
        name="ssd",
    )(*args)


def _gdn_prep_kernel(n_pre, p_ref, cw_ref, q_ref, k_ref, v_ref):
    CP = SCAN_CHUNK
    bd = _head_ones()

    def prep(c, carry):
        r0 = pl.multiple_of(c * CP, CP)
        y = _silu(_conv3_chunk(p_ref, r0, CP, n_pre * CP, 0, 768, cw_ref, c == 0, c == n_pre - 1))
        q, k, v = y[:, 0:256], y[:, 256:512], y[:, 512:768]
        rows = pl.ds(r0, CP)
        q_ref[rows, :] = q * lax.rsqrt(_dot01(bd, q * q, m_left=False) + EPS) * (D_HEAD ** -0.5)
        k_ref[rows, :] = k * lax.rsqrt(_dot01(bd, k * k, m_left=False) + EPS)
        v_ref[rows, :] = v
        return carry

    lax.fori_loop(0, n_pre, prep, 0)


def _gdn_scan_kernel(n_seq, n_blk, has_s0, has_acc, *refs):
    C = GDN_CHUNK
    R = N_HEADS * C
    SUB = SCAN_CHUNK // C
    refs = list(refs)
    qf, kf, vf, gf, qb, kb, vb, gb, dtb_ref, alog_ref = refs[:10]
    refs = refs[10:]
    if has_s0:
        s0_ref = refs[0]
        refs = refs[1:]
    if has_acc:
        refs = refs[1:]
    of_ref, ob_ref, st_ref, sf, sb = refs
    t = pl.program_id(1)

    @pl.when(t == 0)
    def _():
        sf[...] = jnp.zeros_like(sf)
        sb[...] = jnp.zeros_like(sb)
        if has_s0:
            for j in range(n_seq):
                for h in range(N_HEADS):
                    sl = slice(D_HEAD * h, D_HEAD * (h + 1))
                    sf[j, sl, sl] = s0_ref[j, 0, 0, h]
                    sb[j, sl, sl] = s0_ref[j, 0, 1, h]

    lane_s = _iota((1, SMALL_W), 1)
    la_mask = jnp.where(lane_s < 8, 1.0, 0.0)
    same = (_iota((R, R), 0) // C) == (_iota((R, R), 1) // C)
    lane_head = _iota((1, W_GROUP), 1) // D_HEAD
    pad_rows = jnp.zeros((SMALL_W - C, SMALL_W), F32)
    ic = _iota((C, R), 0)
    jc = _iota((C, R), 1) % C
    zero_b = jnp.zeros((R, R), BF16)

    def expand(x):
        xb = x.astype(BF16)
        return jnp.where(same, jnp.concatenate([xb] * N_HEADS, axis=0), zero_b)

    def per_lane(x, lane0):
        cols = [jnp.broadcast_to(x[:, lane0 + h:lane0 + h + 1], (C, W_GROUP)) for h in range(N_HEADS)]
        return jnp.where(lane_head == 0, cols[0],
                         jnp.where(lane_head == 1, cols[1], jnp.where(lane_head == 2, cols[2], cols[3])))

    def chunks(streams):
        seqs, offs, dirs = zip(*streams)
        fwd = [d == 0 for d in dirs]
        base = [4 * d for d in dirs]
        rows = [pl.ds(pl.multiple_of(r0, C), C) for r0 in offs]
        pick = lambda a_f, a_b: [(a_f if f else a_b) for f in fwd]
        q_refs, k_refs, v_refs, g_refs = pick(qf, qb), pick(kf, kb), pick(vf, vb), pick(gf, gb)
        s_refs, o_refs = pick(sf, sb), pick(of_ref, ob_ref)
        g = [r[j, rw, :] for r, j, rw in zip(g_refs, seqs, rows)]
        la = [-jnp.exp(alog_ref[...]) * _softplus(x + dtb_ref[...]) * la_mask for x in g]
        beta = [_sigmoid(x) for x in g]
        cum = [_dot01(_tri(C, f), x) for f, x in zip(fwd, la)]
        last_row = [x[C - 1:C, :] if f else x[0:1, :] for f, x in zip(fwd, cum)]
        cum_t = [jnp.concatenate([x, pad_rows], axis=0).T for x in cum]
        cum_row = []
        for x, b in zip(cum_t, base):
            pc = [x[b + h:b + h + 1, :] for h in range(N_HEADS)]
            cum_row.append(jnp.concatenate([pc[0] + pltpu.roll(pc[1], C, 1), pc[2] + pltpu.roll(pc[3], C, 1)],
                                           axis=1))
        cum_n = [per_lane(x, b) for x, b in zip(cum, base)]
        beta_n = [per_lane(x, 8 + b) for x, b in zip(beta, base)]
        last_lane = []
        for l, b in zip(last_row, base):
            ll = jnp.zeros((1, W_GROUP), F32)
            for h in range(N_HEADS):
                ll = jnp.where(lane_head == h, l[:, b + h:b + h + 1], ll)
            last_lane.append(ll)
        ecum_n = [jnp.exp(x) for x in cum_n]
        kdec_n = [jnp.exp(l - x) for l, x in zip(last_lane, cum_n)]
        incl = [(jc <= ic) if f else (jc >= ic) for f in fwd]
        strict = [(jc < ic) if f else (jc > ic) for f in fwd]
        dec = [jnp.exp(jnp.where(m, x - y, NEG_BIG)) for m, x, y in zip(incl, cum_n, cum_row)]
        q = [r[j, rw, :] for r, j, rw in zip(q_refs, seqs, rows)]
        k = [r[j, rw, :] for r, j, rw in zip(k_refs, seqs, rows)]
        v = [r[j, rw, :] for r, j, rw in zip(v_refs, seqs, rows)]
        k_x = [expand(x) for x in k]
        kk = [_dot(x, y, NT) for x, y in zip(k, k_x)]
        qk = [_dot(x, y, NT) for x, y in zip(q, k_x)]
        a = [jnp.where(m, d * x, 0.0) * b for m, d, x, b in zip(strict, dec, kk, beta_n)]
        n = [-jnp.where((ic // 2) == (jc // 2), x, 0.0) for x in a]
        s = 2
        while s < C:
            blk = ((ic // (2 * s)) == (jc // (2 * s))) & ((ic // s) != (jc // s))
            lo = [jnp.where(blk, x, 0.0) for x in a]
            y = [x + _dot(x, expand(m)) for x, m in zip(lo, n)]
            n = [m - (x + _dot(m, expand(x))) for m, x in zip(n, y)]
            s *= 2
        kp = [x * (b * e) for x, b, e in zip(k, beta_n, ecum_n)]
        vp = [x * b for x, b in zip(v, beta_n)]
        wm = [x + _dot(m, expand(x)) for m, x in zip(n, kp)]
        um = [x + _dot(m, expand(x)) for m, x in zip(n, vp)]
        st = [r[j] for r, j in zip(s_refs, seqs)]
        v_new = [u - _dot(w_, s_) for u, w_, s_ in zip(um, wm, st)]
        o = [_dot(x * e, s_) + _dot(y * d, expand(vn))
             for x, e, s_, y, d, vn in zip(q, ecum_n, st, qk, dec, v_new)]
        for j, rw, o_r, x in zip(seqs, rows, o_refs, o):
            o_r[j, rw, :] = x
        for s_r, j, s_, l, x, kd, vn in zip(s_refs, seqs, st, last_lane, k, kdec_n, v_new):
            s_r[j] = s_ * jnp.exp(l) + jnp.where(same, _dot(x * kd, vn, TN), 0.0)

    def scan_body(u, carry):
        streams = []
        for j in range(n_seq):
            streams += [(j, u * C, 0), (j, (SUB - 1 - u) * C, 1)]
        chunks(streams)
        return carry

    lax.fori_loop(0, SUB, scan_body, 0)

    @pl.when(t == n_blk - 1)
    def _():
        for j in range(n_seq):
            for h in range(N_HEADS):
                sl = slice(D_HEAD * h, D_HEAD * (h + 1))
                st_ref[j, 0, 0, h] = sf[j, sl, sl]
                st_ref[j, 0, 1, h] = sb[j, sl, sl]


def _gdn(p_gdn, p_small, conv_w, dtb, alog, bsz, seq_len, s0, layer, acc):
    n_tok = bsz * seq_len
    n_blk = seq_len // SCAN_CHUNK
    n_seq = min(GDN_SEQS_MAX, bsz)
    has_s0 = s0 is not None
    tok = jax.ShapeDtypeStruct((n_tok, W_GROUP), F32)

    q, k, v = pl.pallas_call(
        functools.partial(_gdn_prep_kernel, n_blk),
        out_shape=(tok, tok, tok),
        grid=(bsz,),
        in_specs=[pl.BlockSpec((seq_len, 768), lambda i: (i, 0)), pl.BlockSpec((3, 768), lambda i: (0, 0))],
        out_specs=(pl.BlockSpec((seq_len, W_GROUP), lambda i: (i, 0)),) * 3,
        compiler_params=_params(("arbitrary",)),
        name="gated_delta_prep",
    )(p_gdn, conv_w)

    seq3 = lambda a: a.reshape(bsz, seq_len, a.shape[-1])
    fwd_blk = lambda w: pl.BlockSpec((n_seq, SCAN_CHUNK, w), lambda i, t: (i, t, 0))
    bwd_blk = lambda w: pl.BlockSpec((n_seq, SCAN_CHUNK, w), lambda i, t: (i, n_blk - 1 - t, 0))
    const = lambda a, b: pl.BlockSpec((a, b), lambda i, t: (0, 0))
    q3, k3, v3, g3 = seq3(q), seq3(k), seq3(v), seq3(p_small)
    in_specs = [fwd_blk(W_GROUP)] * 3 + [fwd_blk(SMALL_W)] + [bwd_blk(W_GROUP)] * 3 + [bwd_blk(SMALL_W)] \
        + [const(1, SMALL_W), const(1, SMALL_W)]
    args = [q3, k3, v3, g3, q3, k3, v3, g3, dtb, alog]
    if has_s0:
        in_specs.append(pl.BlockSpec((n_seq, 1, 2, N_HEADS, D_HEAD, D_HEAD), lambda i, t: (i, layer, 0, 0, 0, 0)))
        args.append(s0)
    seq_out = jax.ShapeDtypeStruct((bsz, seq_len, W_GROUP), F32)
    st_shape, st_spec, alias = _state_target(acc, bsz, n_seq, layer, (N_HEADS, D_HEAD, D_HEAD), args, in_specs, 2)
    o_f, o_b, state = pl.pallas_call(
        functools.partial(_gdn_scan_kernel, n_seq, n_blk, has_s0, acc is not None),
        out_shape=(seq_out, seq_out, st_shape),
        grid=(bsz // n_seq, n_blk),
        in_specs=in_specs,
        out_specs=(fwd_blk(W_GROUP), bwd_blk(W_GROUP), st_spec),
        input_output_aliases=alias,
        scratch_shapes=[pltpu.VMEM((n_seq, W_GROUP, W_GROUP), F32), pltpu.VMEM((n_seq, W_GROUP, W_GROUP), F32)],
        compiler_params=_params(("arbitrary", "arbitrary")),
        name="gated_delta_scan",
    )(*args)
    return o_f.reshape(n_tok, W_GROUP), o_b.reshape(n_tok, W_GROUP), state


def _dft_tables(seq_len):
    period = 8 * seq_len
    theta = math.pi / (4 * seq_len)
    fine, coarse = 64, seq_len // 64
    odd = 2 * jnp.arange(seq_len, dtype=jnp.int32) + 1
    ka = ((2 * fine * jnp.arange(coarse, dtype=jnp.int32))[:, None] * odd[None, :]) % period
    kb = (odd[:fine, None] * odd[None, :]) % period
    ang_a = ka.astype(F32)[:, None, :] * theta
    ang_b = kb.astype(F32)[None, :, :] * theta
    ca, sa, cb, sb = jnp.cos(ang_a), jnp.sin(ang_a), jnp.cos(ang_b), jnp.sin(ang_b)
    c2 = (ca * cb - sa * sb).reshape(seq_len, seq_len).astype(BF16)
    s2 = (sa * cb + ca * sb).reshape(seq_len, seq_len).astype(BF16)
    half = odd.astype(F32) * theta
    return c2, s2, jnp.stack([jnp.cos(half), jnp.sin(half)], axis=1)


def _filter_features(seq_len):
    t = jnp.linspace(0.0, 1.0, seq_len, dtype=F32)[:, None]
    bands = (HY_EMB - 1) // 2
    w = 2.0 * math.pi * jnp.arange(seq_len, dtype=F32)[:, None] / seq_len
    f = jnp.linspace(1e-4, bands - 1, bands, dtype=F32)[None, :]
    z = jnp.concatenate([t, jnp.cos(f * w), -jnp.sin(f * w)], axis=-1)
    z = jnp.pad(z, ((0, 0), (0, 128 - HY_EMB)))
    deltas = jnp.abs(jnp.linspace(math.log(HY_DECAY_TARGET) / HY_SLOW_PCT,
                                  math.log(HY_DECAY_TARGET) / HY_FAST_PCT, W_GROUP, dtype=F32))
    return z, jnp.exp(-t * deltas)


def _hyena_kernel(n_chunks, p_ref, cw_ref, cb_ref, bias_ref, nw_ref, z_ref, dec_ref, fr_ref, w1_ref, b1_ref,
                  w2_ref, b2_ref, w3_ref, c2_ref, s2_ref, ph_ref, o_ref,
                  hre_s, him_s, x0_s, u_s, ub_s, a_s, b_s, yre_s, yim_s):
    C = SCAN_CHUNK
    seq_len = n_chunks * C

    @pl.when(pl.program_id(0) == 0)
    def _():
        fr = fr_ref[0]

        def taps_chunk(c, carry):
            r0 = pl.multiple_of(c * C, C)
            rows = pl.ds(r0, C)
            h = jnp.sin(fr * (_dot_hp(z_ref[rows, :], w1_ref[0]) + b1_ref[0]))
            h = jnp.sin(fr * (_dot_hp(h, w2_ref[0]) + b2_ref[0]))
            taps = _dot_hp(h, w3_ref[0])
            hf = taps[:, 0:W_GROUP] * dec_ref[rows, :]
            hb = taps[:, W_GROUP:2 * W_GROUP] * dec_ref[rows, :]
            hb = jnp.where(_iota(hb.shape, 0) + r0 == 0, 0.0, hb)
            yre_s[rows, :] = (hf + hb).astype(BF16)
            yim_s[rows, :] = (hf - hb).astype(BF16)
            return carry

        lax.fori_loop(0, n_chunks, taps_chunk, 0)
        a_s[...] = _dot(c2_ref[...], yre_s[...])
        b_s[...] = _dot(s2_ref[...], yre_s[...])

        def hre_chunk(c, carry):
            rows = pl.ds(pl.multiple_of(c * C, C), C)
            hre_s[rows, :] = ph_ref[rows, 0:1] * a_s[rows, :] + ph_ref[rows, 1:2] * b_s[rows, :]
            return carry

        lax.fori_loop(0, n_chunks, hre_chunk, 0)
        a_s[...] = _dot(c2_ref[...], yim_s[...])
        b_s[...] = _dot(s2_ref[...], yim_s[...])

        def him_chunk(c, carry):
            rows = pl.ds(pl.multiple_of(c * C, C), C)
            him_s[rows, :] = ph_ref[rows, 1:2] * a_s[rows, :] - ph_ref[rows, 0:1] * b_s[rows, :]
            return carry

        lax.fori_loop(0, n_chunks, him_chunk, 0)

    def prep(c, carry):
        r0 = pl.multiple_of(c * C, C)
        y = _conv3_chunk(p_ref, r0, C, seq_len, 0, 768, cw_ref, c == 0, c == n_chunks - 1) + cb_ref[...]
        rows = pl.ds(r0, C)
        u = y[:, 512:768] * y[:, 256:512]
        x0_s[rows, :] = y[:, 0:256]
        u_s[rows, :] = u
        ub_s[rows, :] = u.astype(BF16)
        return carry

    lax.fori_loop(0, n_chunks, prep, 0)

    a_s[...] = _dot(c2_ref[...], ub_s[...])
    b_s[...] = _dot(s2_ref[...], ub_s[...])

    def spectrum(c, carry):
        rows = pl.ds(pl.multiple_of(c * C, C), C)
        ure, su, hre, him = a_s[rows, :], b_s[rows, :], hre_s[rows, :], him_s[rows, :]
        yre_s[rows, :] = (ure * hre + su * him).astype(BF16)
        yim_s[rows, :] = (ure * him - su * hre).astype(BF16)
        return carry

    lax.fori_loop(0, n_chunks, spectrum, 0)
    a_s[...] = _dot(c2_ref[...], yre_s[...])
    b_s[...] = _dot(s2_ref[...], yim_s[...])

    def fin(c, carry):
        rows = pl.ds(pl.multiple_of(c * C, C), C)
        conv = (a_s[rows, :] - b_s[rows, :]) * (1.0 / seq_len)
        y = x0_s[rows, :] * (conv + u_s[rows, :] * bias_ref[...])
        o_ref[rows, :] = _rms(y, nw_ref[...])
        return carry

    lax.fori_loop(0, n_chunks, fin, 0)


def _hyena(p_hy, conv_w, conv_b, bias, nw, tabs, feats, freq, w1, b1, w2, b2, w3, bsz, seq_len, layer):
    c2, s2, phase = tabs
    z, dec = feats
    one = dict(pipeline_mode=pl.Buffered(1)) if seq_len > 1024 else {}
    const = lambda a, b: pl.BlockSpec((a, b), lambda i: (0, 0), **one)
    lay = lambda a, b: pl.BlockSpec((1, a, b), lambda i: (layer, 0, 0))
    acc = pltpu.VMEM((seq_len, W_GROUP), F32)
    half = pltpu.VMEM((seq_len, W_GROUP), BF16)
    return pl.pallas_call(
        functools.partial(_hyena_kernel, seq_len // SCAN_CHUNK),
        out_shape=jax.ShapeDtypeStruct((bsz * seq_len, W_GROUP), F32),
        grid=(bsz,),
        in_specs=[pl.BlockSpec((seq_len, 768), lambda i: (i, 0), **one), const(3, 768), const(1, 768),
                  const(1, W_GROUP), const(1, W_GROUP), const(seq_len, 128), const(seq_len, W_GROUP),
                  lay(1, HY_HIDDEN), lay(128, HY_HIDDEN), lay(1, HY_HIDDEN), lay(HY_HIDDEN, HY_HIDDEN),
                  lay(1, HY_HIDDEN), lay(HY_HIDDEN, 2 * W_GROUP),
                  const(seq_len, seq_len), const(seq_len, seq_len), const(seq_len, 2)],
        out_specs=pl.BlockSpec((seq_len, W_GROUP), lambda i: (i, 0)),
        scratch_shapes=[acc, acc, acc, acc, half, acc, acc, half, half],
        compiler_params=_params(("arbitrary",)),
        name="hyena",
    )(p_hy, conv_w, conv_b, bias, nw, z, dec, freq, w1, b1, w2, b2, w3, c2, s2, phase)


def _rope_tables(seq_len):
    rows = seq_len // GRID_W
    r = jnp.repeat(jnp.arange(rows), GRID_W).astype(F32)
    col = jnp.tile(jnp.arange(GRID_W), rows).astype(F32)
    quarter = D_HEAD // 4
    inv = ROPE_BASE ** (-jnp.arange(quarter, dtype=F32) / quarter)
    ang = jnp.concatenate([r[:, None] * inv, col[:, None] * inv], axis=-1)
    cos, sin = jnp.cos(ang), jnp.sin(ang)
    cos_full = jnp.tile(jnp.concatenate([cos, cos], axis=-1), (1, N_HEADS))
    sin_signed = jnp.tile(jnp.concatenate([-sin, sin], axis=-1), (1, N_HEADS))
    return cos_full, sin_signed


def _small_vec(first_lane, vals):
    flat = vals.reshape(-1)
    return jnp.zeros((1, SMALL_W), F32).at[0, first_lane:first_lane + flat.shape[0]].set(flat)


def _trunk(x, bsz, seq_len, row_fn, states, finals, rope_tabs, hy_tabs, hy_feats, mod3, w):
    st_ret, st_gdn, st_ssd = states
    f_ret, f_gdn, f_ssd = finals
    for l in range(DEPTH):
        p_ret, p_hy, p_gdn, p_ssd, p_small = _in_proj(x, mod3, w['norm1'], w['w_main'], w['w_ssd'],
                                                      w['w_small'], l, row_fn)
        o_ret, s_ret = _retention(p_ret, w['ret_la'][l], w['ret_norm'][l], bsz, seq_len, rope_tabs, st_ret, l,
                                  f_ret)
        o_hy = _hyena(p_hy, w['hy_conv_w'][l], w['hy_conv_b'][l], w['hy_bias'][l], w['hy_norm'][l],
                      hy_tabs, hy_feats, w['hy_freq'], w['hy_w1'], w['hy_b1'], w['hy_w2'], w['hy_b2'],
                      w['hy_w3'], bsz, seq_len, l)
        og_f, og_b, s_gdn = _gdn(p_gdn, p_small, w['gdn_conv_w'][l], w['gdn_dtb'][l], w['gdn_alog'][l],
                                 bsz, seq_len, st_gdn, l, f_gdn)
        o_ssd, s_ssd = _ssd(p_ssd, p_small, w['ssd_conv_w'][l], w['ssd_conv_b'][l], w['ssd_dtb'][l],
                            w['ssd_alog'][l], w['ssd_dl'][l], w['ssd_norm'][l], bsz, seq_len, st_ssd, l, f_ssd)
        x = _out_proj((o_ret, o_hy, og_f, og_b, p_gdn, w['gdn_norm'][l], o_ssd), x, mod3, w['norm2'],
                      w['w_out'], w['mlp_w1'], w['mlp_w2'], w['final_norm'], l, row_fn, l == DEPTH - 1)
        if f_ret is not None:
            f_ret, f_gdn, f_ssd = s_ret, s_gdn, s_ssd
    return x, (f_ret, f_gdn, f_ssd)


def kernel(x_prompt, x_sample, state_ret, state_gdn, state_ssd, c, c_ctx, norm1_w, norm2_w, w_mod, b_mod, w_in, w_out, ret_log_decay, ret_norm_w, hy_conv_w, hy_conv_b, hy_freq, hy_w1, hy_b1, hy_w2, hy_b2, hy_w3, hy_bias, hy_norm_w, gdn_conv_w, gdn_A_log, gdn_dt_bias, gdn_norm_w, ssd_conv_w, ssd_conv_b, ssd_A_log, ssd_dt_bias, ssd_D, ssd_norm_w, mlp_w1, mlp_w2, final_norm_w):
    bsz, seq, _ = x_prompt.shape
    dbsz, dseq, _ = x_sample.shape

    cond8 = jnp.concatenate([c, c_ctx[None, :], jnp.zeros((MOD_ROWS - dbsz - 1, D_MODEL), F32)], axis=0)
    mod = _modulation(cond8, w_mod, b_mod)
    mod3 = mod.reshape(DEPTH * MOD_ROWS * 6, 1, D_MODEL)

    w_main = w_in.astype(BF16)
    w_ssd = w_main[:, :, 2832:3856]
    w_small = jnp.concatenate([w_main[:, :, 2816:2832], w_main[:, :, 3856:3864],
                               jnp.zeros((DEPTH, D_MODEL, SMALL_W - 24), BF16)], axis=-1)
    w = dict(
        norm1=norm1_w.reshape(DEPTH, 1, D_MODEL), norm2=norm2_w.reshape(DEPTH, 1, D_MODEL),
        w_main=w_main, w_ssd=w_ssd, w_small=w_small, w_out=w_out.astype(BF16),
        mlp_w1=mlp_w1.astype(BF16), mlp_w2=mlp_w2.astype(BF16), final_norm=final_norm_w,
        ret_la=[jnp.repeat(ret_log_decay[l], D_HEAD, axis=-1) for l in range(DEPTH)],
        ret_norm=[ret_norm_w[l][None, :] for l in range(DEPTH)],
        hy_freq=hy_freq.reshape(DEPTH, 1, HY_HIDDEN),
        hy_w1=jnp.pad(hy_w1, ((0, 0), (0, 128 - HY_EMB), (0, 0))),
        hy_b1=hy_b1.reshape(DEPTH, 1, HY_HIDDEN), hy_w2=hy_w2, hy_b2=hy_b2.reshape(DEPTH, 1, HY_HIDDEN),
        hy_w3=hy_w3,
        hy_conv_w=hy_conv_w, hy_conv_b=[hy_conv_b[l][None, :] for l in range(DEPTH)],
        hy_bias=[hy_bias[l][None, :] for l in range(DEPTH)], hy_norm=[hy_norm_w[l][None, :] for l in range(DEPTH)],
        gdn_conv_w=gdn_conv_w,
        gdn_dtb=[_small_vec(0, gdn_dt_bias[l]) for l in range(DEPTH)],
        gdn_alog=[_small_vec(0, gdn_A_log[l]) for l in range(DEPTH)],
        gdn_norm=[gdn_norm_w[l][None, :] for l in range(DEPTH)],
        ssd_conv_w=ssd_conv_w, ssd_conv_b=[ssd_conv_b[l][None, :] for l in range(DEPTH)],
        ssd_dtb=[_small_vec(16, ssd_dt_bias[l]) for l in range(DEPTH)],
        ssd_alog=[_small_vec(16, ssd_A_log[l]) for l in range(DEPTH)],
        ssd_dl=[jnp.repeat(ssd_D[l], D_HEAD)[None, :] for l in range(DEPTH)],
        ssd_norm=[ssd_norm_w[l][None, :] for l in range(DEPTH)],
    )

    finals = (jnp.zeros((bsz, DEPTH, 2, N_HEADS, D_HEAD, D_HEAD), F32),
              jnp.zeros((bsz, DEPTH, 2, N_HEADS, D_HEAD, D_HEAD), F32),
              jnp.zeros((bsz, DEPTH, 2, N_HEADS, N_SSD, D_HEAD), F32))
    y_p, (new_ret, new_gdn, new_ssd) = _trunk(
        x_prompt.reshape(bsz * seq, D_MODEL), bsz, seq, lambda row: CTX_MOD_ROW, (None, None, None), finals,
        None, _dft_tables(seq), _filter_features(seq), mod3, w)

    y_s, _ = _trunk(x_sample.reshape(dbsz * dseq, D_MODEL), dbsz, dseq, lambda row: row // dseq,
                    (state_ret, state_gdn, state_ssd), (None, None, None), _rope_tables(dseq),
                    _dft_tables(dseq), _filter_features(dseq), mod3, w)

    return (y_p.reshape(bsz, seq, D_MODEL), y_s.reshape(dbsz, dseq, D_MODEL), new_ret, new_gdn, new_ssd)
```

```python
import functools
import math

import jax
import jax.numpy as jnp
from jax import lax
from jax.experimental import pallas as pl
from jax.experimental.pallas import tpu as pltpu

F32 = jnp.float32
BF16 = jnp.bfloat16

D_MODEL = 1024
DEPTH = 4
GRID_W = 64
W_GROUP = 256
N_HEADS = 4
D_HEAD = 64
N_SSD = 128
G_SSD = 2
HY_EMB = 33
HY_HIDDEN = 64
HY_DECAY_TARGET = 1e-2
HY_FAST_PCT = 0.3
HY_SLOW_PCT = 1.5
D_FF = 4 * D_MODEL
ROPE_BASE = 10000.0
EPS = 1e-6
SMALL_W = 128
MOD_ROWS = 8
CTX_MOD_ROW = 4

TOK_TILE = 512
OUT_TILE = 512
SCAN_CHUNK = 256
GDN_CHUNK = 64
NEG_BIG = -1e30
RET_SEQS = 2
SSD_SEQS = 2
GDN_SEQS_MAX = 4
VMEM_LIMIT = 56 * 1024 * 1024

NN = (((1,), (0,)), ((), ()))
NT = (((1,), (1,)), ((), ()))
TN = (((0,), (0,)), ((), ()))


def _dot(a, b, dims=NN):
    return lax.dot_general(a.astype(BF16), b.astype(BF16), dims, preferred_element_type=F32)


def _dot_hp(a, b, dims=NN):
    a_hi, a_lo = _split(a)
    b_hi, b_lo = _split(b)
    return _dot3(a_hi, a_lo, b_hi, b_lo, dims)


def _split(x):
    hi = x.astype(BF16)
    lo = (x - hi.astype(F32)).astype(BF16)
    return hi, lo


def _dot3(a_hi, a_lo, b_hi, b_lo, dims=NN):
    d = functools.partial(lax.dot_general, dimension_numbers=dims, preferred_element_type=F32)
    return d(a_hi, b_hi) + d(a_hi, b_lo) + d(a_lo, b_hi)


def _dot01(m, x, terms=3, m_left=True):
    mb = m.astype(BF16)
    acc, r = None, x.astype(F32)
    for _ in range(terms):
        t = r.astype(BF16)
        p = lax.dot_general(mb, t, NN, preferred_element_type=F32) if m_left else \
            lax.dot_general(t, mb, NN, preferred_element_type=F32)
        acc = p if acc is None else acc + p
        r = r - t.astype(F32)
    return acc


def _sigmoid(x):
    return 1.0 / (1.0 + jnp.exp(-x))


def _silu(x):
    return x * _sigmoid(x)


def _softplus(x):
    return jnp.maximum(x, 0.0) + jnp.log1p(jnp.exp(-jnp.abs(x)))


def _iota(shape, dim):
    return lax.broadcasted_iota(jnp.int32, shape, dim)


def _params(sem):
    return pltpu.CompilerParams(dimension_semantics=sem, vmem_limit_bytes=VMEM_LIMIT)


def _head_ones(n=W_GROUP, width=D_HEAD):
    return jnp.where(_iota((n, n), 0) // width == _iota((n, n), 1) // width, 1.0, 0.0).astype(F32)


def _mod_kernel(cond_ref, w_ref, b_ref, o_ref):
    s = _silu(cond_ref[...])
    o_ref[0] = _dot(s, w_ref[0]) + b_ref[0]


def _modulation(cond8, w_mod, b_mod):
    tn = 1536
    n_out = 6 * D_MODEL
    return pl.pallas_call(
        _mod_kernel,
        out_shape=jax.ShapeDtypeStruct((DEPTH, MOD_ROWS, n_out), F32),
        grid=(DEPTH, n_out // tn),
        in_specs=[
            pl.BlockSpec((MOD_ROWS, D_MODEL), lambda l, j: (0, 0)),
            pl.BlockSpec((1, D_MODEL, tn), lambda l, j: (l, 0, j)),
            pl.BlockSpec((1, 1, tn), lambda l, j: (l, 0, j)),
        ],
        out_specs=pl.BlockSpec((1, MOD_ROWS, tn), lambda l, j: (l, 0, j)),
        compiler_params=_params(("arbitrary", "arbitrary")),
        name="modulation",
    )(cond8, w_mod, b_mod.reshape(DEPTH, 1, n_out))


def _mod_spec(layer, k, row_fn, tile):
    return pl.BlockSpec((1, 1, D_MODEL), lambda i: ((layer * MOD_ROWS + row_fn(i * tile)) * 6 + k, 0, 0))


def _rms(x, w):
    return x * lax.rsqrt(jnp.mean(x * x, axis=-1, keepdims=True) + EPS) * w


def _in_kernel(x_ref, nw_ref, sh_ref, sc_ref, wm_ref, wd_ref, ws_ref, pret, phy, pgdn, pssd, psm):
    h = _rms(x_ref[...], nw_ref[0]) * (1.0 + sc_ref[0]) + sh_ref[0]
    hb = h.astype(BF16)
    d = functools.partial(jnp.dot, preferred_element_type=F32)
    pret[...] = d(hb, wm_ref[0, :, 0:1024])
    phy[...] = d(hb, wm_ref[0, :, 1024:1792])
    pgdn[...] = d(hb, wm_ref[0, :, 1792:2816])
    pssd[...] = d(hb, wd_ref[0])
    psm[...] = d(hb, ws_ref[0])


def _in_proj(x, mod3, norm1_w3, w_main, w_ssd, w_small, layer, row_fn):
    n = x.shape[0]
    widths = (1024, 768, 1024, 1024, SMALL_W)
    return pl.pallas_call(
        _in_kernel,
        out_shape=tuple(jax.ShapeDtypeStruct((n, w), F32) for w in widths),
        grid=(n // TOK_TILE,),
        in_specs=[
            pl.BlockSpec((TOK_TILE, D_MODEL), lambda i: (i, 0)),
            pl.BlockSpec((1, 1, D_MODEL), lambda i: (layer, 0, 0)),
            _mod_spec(layer, 0, row_fn, TOK_TILE),
            _mod_spec(layer, 1, row_fn, TOK_TILE),
            pl.BlockSpec((1, D_MODEL, w_main.shape[-1]), lambda i: (layer, 0, 0), pipeline_mode=pl.Buffered(1)),
            pl.BlockSpec((1, D_MODEL, 1024), lambda i: (layer, 0, 0), pipeline_mode=pl.Buffered(1)),
            pl.BlockSpec((1, D_MODEL, SMALL_W), lambda i: (layer, 0, 0), pipeline_mode=pl.Buffered(1)),
        ],
        out_specs=tuple(pl.BlockSpec((TOK_TILE, w), lambda i: (i, 0)) for w in widths),
        compiler_params=_params(("arbitrary",)),
        name="in_proj",
    )(x, norm1_w3, mod3, mod3, w_main, w_ssd, w_small)


def _out_kernel(final, oret, ohy, ogf, ogb, zg, gnw, ossd, x_ref, g1, sh2, sc2, g2, nw2, wout, w1, w2, *rest):
    if final:
        fnw, y_ref = rest
    else:
        (y_ref,) = rest
    acc = _dot(oret[...], wout[0, 0:256, :])
    acc = acc + _dot(ohy[...], wout[0, 256:512, :])
    acc = acc + _dot(ossd[...], wout[0, 768:1024, :])
    og = ogf[...] + ogb[...]
    ms = _dot01(_head_ones(), og * og, m_left=False) * (1.0 / D_HEAD)
    ogdn = og * lax.rsqrt(ms + EPS) * gnw[...] * _silu(zg[...])
    acc = acc + _dot(ogdn, wout[0, 512:768, :])
    x1 = x_ref[...] + g1[0] * acc
    h2 = _rms(x1, nw2[0]) * (1.0 + sc2[0]) + sh2[0]
    a = _dot(h2, w1[0])
    a = jnp.square(jnp.maximum(a, 0.0))
    x2 = x1 + g2[0] * _dot(a, w2[0])
    if final:
        x2 = _rms(x2, fnw[...])
    y_ref[...] = x2


def _out_proj(mix, x, mod3, norm2_w3, w_out, w1, w2, final_norm_w, layer, row_fn, final):
    n = x.shape[0]
    tok = lambda w: pl.BlockSpec((OUT_TILE, w), lambda i: (i, 0))
    lay = lambda a, b: pl.BlockSpec((1, a, b), lambda i: (layer, 0, 0), pipeline_mode=pl.Buffered(1))
    in_specs = [tok(W_GROUP)] * 4 + [
        pl.BlockSpec((OUT_TILE, W_GROUP), lambda i: (i, 3)), pl.BlockSpec((1, W_GROUP), lambda i: (0, 0)),
        tok(W_GROUP),
        tok(D_MODEL),
        _mod_spec(layer, 2, row_fn, OUT_TILE), _mod_spec(layer, 3, row_fn, OUT_TILE),
        _mod_spec(layer, 4, row_fn, OUT_TILE), _mod_spec(layer, 5, row_fn, OUT_TILE),
        lay(1, D_MODEL), lay(D_MODEL, D_MODEL), lay(D_MODEL, D_FF), lay(D_FF, D_MODEL),
    ]
    args = list(mix) + [x, mod3, mod3, mod3, mod3, norm2_w3, w_out, w1, w2]
    if final:
        in_specs.append(pl.BlockSpec((1, D_MODEL), lambda i: (0, 0)))
        args.append(final_norm_w.reshape(1, D_MODEL))
    return pl.pallas_call(
        functools.partial(_out_kernel, final),
        out_shape=jax.ShapeDtypeStruct((n, D_MODEL), F32),
        grid=(n // OUT_TILE,),
        in_specs=in_specs,
        out_specs=tok(D_MODEL),
        compiler_params=_params(("arbitrary",)),
        name="out_proj_mlp",
    )(*args)


def _conv3_chunk(p_ref, r0, rows, seq_len, c0, width, w_ref, first, last):
    x = p_ref[pl.ds(r0, rows), c0:c0 + width]
    prev_i = jnp.maximum(r0 - 1, 0)
    next_i = jnp.minimum(r0 + rows, seq_len - 1)
    xp = p_ref[pl.ds(prev_i, 1), c0:c0 + width] * jnp.where(first, 0.0, 1.0)
    xn = p_ref[pl.ds(next_i, 1), c0:c0 + width] * jnp.where(last, 0.0, 1.0)
    row = _iota((rows, width), 0)
    x_prev = jnp.where(row == 0, xp, pltpu.roll(x, 1, 0))
    x_next = jnp.where(row == rows - 1, xn, pltpu.roll(x, rows - 1, 0))
    return x_prev * w_ref[0:1, :] + x * w_ref[1:2, :] + x_next * w_ref[2:3, :]


def _tri(n, lower):
    i, j = _iota((n, n), 0), _iota((n, n), 1)
    return jnp.where((j <= i) if lower else (j >= i), 1.0, 0.0).astype(F32)


def _lane_expand(first_row, width):
    r, c = _iota((SMALL_W, W_GROUP), 0), _iota((SMALL_W, W_GROUP), 1)
    return jnp.where(r == first_row + c // width, 1.0, 0.0).astype(F32)


def _ret_kernel(n_chunks, n_seq, rope, has_s0, has_acc, *refs):
    C = SCAN_CHUNK
    refs = list(refs)
    p_ref, la_ref, nw_ref = refs[:3]
    refs = refs[3:]
    if rope:
        cos_ref, sin_ref = refs[:2]
        refs = refs[2:]
    if has_s0:
        s0_ref = refs[0]
        refs = refs[1:]
    if has_acc:
        refs = refs[1:]
    o_ref, st_ref, sf, sb, dmat, etab, qs, ks, oscr = refs

    la_f = la_ref[0:1, :]
    la_b = la_ref[1:2, :]

    @pl.when(pl.program_id(0) == 0)
    def _():
        ri = _iota((C, W_GROUP), 0).astype(F32)
        etab[0] = jnp.exp((ri + 1.0) * la_f)
        etab[1] = jnp.exp((C - ri) * la_b)
        etab[2] = jnp.exp((C - 1.0 - ri) * la_f)
        etab[3] = jnp.exp(ri * la_b)
        ii, jj = _iota((C, C), 0), _iota((C, C), 1)
        diff = (ii - jj).astype(F32)
        for h in range(N_HEADS):
            laf_h = la_ref[0:1, D_HEAD * h:D_HEAD * h + 1]
            lab_h = la_ref[1:2, D_HEAD * h:D_HEAD * h + 1]
            dmat[h] = (jnp.exp(jnp.where(jj <= ii, diff * laf_h, NEG_BIG))
                       + jnp.exp(jnp.where(jj >= ii, -diff * lab_h, NEG_BIG)))

    dc_f = jnp.exp(C * la_f)
    dc_b = jnp.exp(C * la_b)
    bd = _head_ones()
    lane = _iota((1, W_GROUP), 1)

    seq_len = n_chunks * C
    seqs = range(n_seq)
    sf[...] = jnp.zeros_like(sf)
    sb[...] = jnp.zeros_like(sb)
    if has_s0:
        for j in seqs:
            for h in range(N_HEADS):
                sl = slice(D_HEAD * h, D_HEAD * (h + 1))
                sf[j, sl, sl] = s0_ref[j, 0, 0, h]
                sb[j, sl, sl] = s0_ref[j, 0, 1, h]

    def rot(x, cs, sn):
        halves = []
        for a in range(2):
            xh = x[:, 128 * a:128 * (a + 1)]
            lh = _iota(xh.shape, 1)
            halves.append(jnp.where(lh % D_HEAD < D_HEAD // 2, pltpu.roll(xh, 96, 1), pltpu.roll(xh, 32, 1)))
        return x * cs + jnp.concatenate(halves, axis=1) * sn

    def chunk_rows(c):
        return [pl.ds(pl.multiple_of(j * seq_len + c * C, C), C) for j in seqs]

    def fwd(c, carry):
        rows = chunk_rows(c)
        q = [p_ref[r, 0:256] for r in rows]
        k = [p_ref[r, 256:512] * (D_HEAD ** -0.5) for r in rows]
        v = [p_ref[r, 512:768] for r in rows]
        if rope:
            pos = pl.ds(pl.multiple_of(c * C, C), C)
            cs, sn = cos_ref[pos, :], sin_ref[pos, :]
            q = [rot(x, cs, sn) for x in q]
            k = [rot(x, cs, sn) for x in k]
        for r, x, y in zip(rows, q, k):
            qs[r, :] = x
            ks[r, :] = y
        o = [_dot(x * etab[0], sf[j]) for j, x in zip(seqs, q)]
        for h in range(N_HEADS):
            mh = jnp.where(lane // D_HEAD == h, 1.0, 0.0)
            att = [_dot(x * mh, y, NT) * dmat[h] for x, y in zip(q, k)]
            o = [o_ + _dot(a, z * mh) for o_, a, z in zip(o, att, v)]
        for r, o_ in zip(rows, o):
            oscr[r, :] = o_
        for j, y, z in zip(seqs, k, v):
            sf[j] = sf[j] * dc_f + bd * _dot(y * etab[2], z, TN)
        return carry

    lax.fori_loop(0, n_chunks, fwd, 0)

    def bwd(t, carry):
        rows = chunk_rows(n_chunks - 1 - t)
        q, k = [qs[r, :] for r in rows], [ks[r, :] for r in rows]
        v = [p_ref[r, 512:768] for r in rows]
        o = [oscr[r, :] + _dot(x * etab[1], sb[j]) for r, j, x in zip(rows, seqs, q)]
        for j, y, z in zip(seqs, k, v):
            sb[j] = sb[j] * dc_b + bd * _dot(y * etab[3], z, TN)
        ms = [_dot01(bd, o_ * o_, m_left=False) * (1.0 / D_HEAD) for o_ in o]
        for r, o_, m in zip(rows, o, ms):
            o_ref[r, :] = o_ * lax.rsqrt(m + EPS) * nw_ref[...] * _silu(p_ref[r, 768:1024])
        return carry

    lax.fori_loop(0, n_chunks, bwd, 0)

    for j in seqs:
        for h in range(N_HEADS):
            sl = slice(D_HEAD * h, D_HEAD * (h + 1))
            st_ref[j, 0, 0, h] = sf[j, sl, sl]
            st_ref[j, 0, 1, h] = sb[j, sl, sl]


def rows_fit(rows):
    return rows * 1024 * 4 <= (8 << 20)


def _state_target(acc, bsz, n_seq, layer, tail, args, in_specs, out_index):
    if acc is None:
        shape = jax.ShapeDtypeStruct((bsz, 1, 2) + tail, F32)
        return shape, pl.BlockSpec((n_seq, 1, 2) + tail, lambda i, *_: (i, 0, 0, 0, 0, 0)), {}
    alias = {len(args): out_index}
    args.append(acc)
    in_specs.append(pl.BlockSpec(memory_space=pl.ANY))
    shape = jax.ShapeDtypeStruct(acc.shape, F32)
    return shape, pl.BlockSpec((n_seq, 1, 2) + tail, lambda i, *_: (i, layer, 0, 0, 0, 0)), alias


def _retention(p_ret, la_lane, nw, bsz, seq_len, rope_tabs, s0, layer, acc):
    n_chunks = seq_len // SCAN_CHUNK
    n_seq = RET_SEQS if rows_fit(RET_SEQS * seq_len) else 1
    rows = n_seq * seq_len
    rope = rope_tabs is not None
    has_s0 = s0 is not None
    big = {} if rows_fit(rows) else dict(pipeline_mode=pl.Buffered(1))
    const = lambda a, b: pl.BlockSpec((a, b), lambda i: (0, 0))
    in_specs = [pl.BlockSpec((rows, 1024), lambda i: (i, 0), **big), const(2, W_GROUP), const(1, W_GROUP)]
    args = [p_ret, la_lane, nw]
    if rope:
        in_specs += [pl.BlockSpec((seq_len, W_GROUP), lambda i: (0, 0), **big)] * 2
        args += list(rope_tabs)
    if has_s0:
        in_specs.append(pl.BlockSpec((n_seq, 1, 2, N_HEADS, D_HEAD, D_HEAD), lambda i: (i, layer, 0, 0, 0, 0)))
        args.append(s0)
    st_shape, st_spec, alias = _state_target(acc, bsz, n_seq, layer, (N_HEADS, D_HEAD, D_HEAD), args, in_specs, 1)
    return pl.pallas_call(
        functools.partial(_ret_kernel, n_chunks, n_seq, rope, has_s0, acc is not None),
        out_shape=(jax.ShapeDtypeStruct((bsz * seq_len, W_GROUP), F32), st_shape),
        grid=(bsz // n_seq,),
        in_specs=in_specs,
        out_specs=(pl.BlockSpec((rows, W_GROUP), lambda i: (i, 0)), st_spec),
        input_output_aliases=alias,
        scratch_shapes=[pltpu.VMEM((n_seq, W_GROUP, W_GROUP), F32), pltpu.VMEM((n_seq, W_GROUP, W_GROUP), F32),
                        pltpu.VMEM((N_HEADS, SCAN_CHUNK, SCAN_CHUNK), F32),
                        pltpu.VMEM((4, SCAN_CHUNK, W_GROUP), F32),
                        pltpu.VMEM((rows, W_GROUP), F32), pltpu.VMEM((rows, W_GROUP), F32),
                        pltpu.VMEM((rows, W_GROUP), F32)],
        compiler_params=_params(("arbitrary",)),
        name="retention",
    )(*args)


def _ssd_kernel(n_chunks, n_seq, has_s0, has_acc, *refs):
    C = SCAN_CHUNK
    refs = list(refs)
    p_ref, ps_ref, cw_ref, cb_ref, dtb_ref, alog_ref, dl_ref, nw_ref = refs[:8]
    refs = refs[8:]
    if has_s0:
        s0_ref = refs[0]
        refs = refs[1:]
    if has_acc:
        refs = refs[1:]
    o_ref, st_ref, sf, sb, xs_s, bm_s, cm_s, oscr = refs
    seq_len = n_chunks * C
    seqs = range(n_seq)

    def prep(c, carry):
        r0 = pl.multiple_of(c * C, C)
        cs = lax.rem(c, n_chunks)
        y = _conv3_chunk(p_ref, r0, C, n_seq * seq_len, 0, 768, cw_ref, cs == 0, cs == n_chunks - 1) + cb_ref[...]
        y = _silu(y)
        rows = pl.ds(r0, C)
        xs_s[rows, :] = y[:, 0:256]
        bm_s[rows, :] = y[:, 256:512]
        cm_s[rows, :] = y[:, 512:768]
        return carry

    lax.fori_loop(0, n_seq * n_chunks, prep, 0)

    sf[...] = jnp.zeros_like(sf)
    sb[...] = jnp.zeros_like(sb)
    if has_s0:
        for j in seqs:
            for h in range(N_HEADS):
                g = h // 2
                sf[j, N_SSD * g:N_SSD * (g + 1), D_HEAD * h:D_HEAD * (h + 1)] = s0_ref[j, 0, 0, h]
                sb[j, N_SSD * g:N_SSD * (g + 1), D_HEAD * h:D_HEAD * (h + 1)] = s0_ref[j, 0, 1, h]

    ltri, utri = _tri(C, True), _tri(C, False)
    exp_f, exp_b = _lane_expand(16, D_HEAD), _lane_expand(20, D_HEAD)
    gmask = _head_ones(W_GROUP, N_SSD)
    lane_s = _iota((1, SMALL_W), 1)
    dt_mask = jnp.where((lane_s >= 16) & (lane_s < 24), 1.0, 0.0)
    lane = _iota((1, W_GROUP), 1)
    ii, jj = _iota((C, C), 0), _iota((C, C), 1)

    def chunk_rows(c):
        return [pl.ds(pl.multiple_of(j * seq_len + c * C, C), C) for j in seqs]

    def gates(rows):
        dt = [_softplus(ps_ref[r, :] + dtb_ref[...]) * dt_mask for r in rows]
        la = [x * (-jnp.exp(alog_ref[...])) for x in dt]
        return dt, la

    def fwd(c, carry):
        rows = chunk_rows(c)
        dt, la = gates(rows)
        cum_p = [_dot01(ltri, x) for x in la]
        cum_s = [_dot01(utri, x) for x in la]
        cum_pt, cum_st, dt_t = [x.T for x in cum_p], [x.T for x in cum_s], [x.T for x in dt]
        x_f = [_dot01(exp_f, x, m_left=False) for x in cum_p]
        dt_f = [_dot01(exp_f, x, m_left=False) for x in dt]
        xs, bm, cm = [xs_s[r, :] for r in rows], [bm_s[r, :] for r in rows], [cm_s[r, :] for r in rows]
        s_g = [[_dot(c_[:, N_SSD * g:N_SSD * (g + 1)], b_[:, N_SSD * g:N_SSD * (g + 1)], NT) for g in range(G_SSD)]
               for c_, b_ in zip(cm, bm)]
        o = [_dot(c_, sf[j]) * jnp.exp(x) for j, c_, x in zip(seqs, cm, x_f)]
        for h in range(N_HEADS):
            f, b = 16 + h, 20 + h
            mh = jnp.where(lane // D_HEAD == h, 1.0, 0.0)
            dec = [jnp.exp(jnp.where(jj <= ii, cp[:, f:f + 1] - cpt[f:f + 1, :], cs_[:, b:b + 1] - cst[b:b + 1, :]))
                   * jnp.where(jj < ii, dtt[f:f + 1, :],
                               jnp.where(jj > ii, dtt[b:b + 1, :], dtt[f:f + 1, :] + dtt[b:b + 1, :]))
                   for cp, cpt, cs_, cst, dtt in zip(cum_p, cum_pt, cum_s, cum_st, dt_t)]
            o = [o_ + _dot(sg[h // 2] * d, x * mh) for o_, sg, d, x in zip(o, s_g, dec, xs)]
        for r, o_ in zip(rows, o):
            oscr[r, :] = o_
        last = [x[C - 1:C, :] for x in x_f]
        for j, l, b_, x, d, xf in zip(seqs, last, bm, xs, dt_f, x_f):
            sf[j] = sf[j] * jnp.exp(l) + gmask * _dot(b_, x * d * jnp.exp(l - xf), TN)
        return carry

    lax.fori_loop(0, n_chunks, fwd, 0)

    def bwd(t, carry):
        rows = chunk_rows(n_chunks - 1 - t)
        dt, la = gates(rows)
        cum_s = [_dot01(utri, x) for x in la]
        x_b = [_dot01(exp_b, x, m_left=False) for x in cum_s]
        dt_b = [_dot01(exp_b, x, m_left=False) for x in dt]
        xs, bm, cm = [xs_s[r, :] for r in rows], [bm_s[r, :] for r in rows], [cm_s[r, :] for r in rows]
        o = [oscr[r, :] + _dot(c_, sb[j]) * jnp.exp(x) for r, j, c_, x in zip(rows, seqs, cm, x_b)]
        first = [x[0:1, :] for x in x_b]
        for j, l, b_, x, d, xb in zip(seqs, first, bm, xs, dt_b, x_b):
            sb[j] = sb[j] * jnp.exp(l) + gmask * _dot(b_, x * d * jnp.exp(l - xb), TN)
        for r, o_, x in zip(rows, o, xs):
            y = (o_ + x * dl_ref[...]) * _silu(p_ref[r, 768:1024])
            halves = [y[:, 128 * g:128 * (g + 1)] for g in range(G_SSD)]
            halves = [yh * lax.rsqrt(jnp.mean(yh * yh, axis=-1, keepdims=True) + EPS) for yh in halves]
            o_ref[r, :] = jnp.concatenate(halves, axis=1) * nw_ref[...]
        return carry

    lax.fori_loop(0, n_chunks, bwd, 0)

    for j in seqs:
        for h in range(N_HEADS):
            g = h // 2
            st_ref[j, 0, 0, h] = sf[j, N_SSD * g:N_SSD * (g + 1), D_HEAD * h:D_HEAD * (h + 1)]
            st_ref[j, 0, 1, h] = sb[j, N_SSD * g:N_SSD * (g + 1), D_HEAD * h:D_HEAD * (h + 1)]


def _ssd(p_ssd, p_small, conv_w, conv_b, dtb, alog, d_lane, nw, bsz, seq_len, s0, layer, acc):
    n_chunks = seq_len // SCAN_CHUNK
    n_seq = SSD_SEQS if rows_fit(SSD_SEQS * seq_len) else 1
    rows = n_seq * seq_len
    has_s0 = s0 is not None
    big = {} if rows_fit(rows) else dict(pipeline_mode=pl.Buffered(1))
    const = lambda a, b: pl.BlockSpec((a, b), lambda i: (0, 0))
    in_specs = [pl.BlockSpec((rows, 1024), lambda i: (i, 0), **big),
                pl.BlockSpec((rows, SMALL_W), lambda i: (i, 0), **big),
                const(3, 768), const(1, 768), const(1, SMALL_W), const(1, SMALL_W), const(1, W_GROUP),
                const(1, W_GROUP)]
    args = [p_ssd, p_small, conv_w, conv_b, dtb, alog, d_lane, nw]
    if has_s0:
        in_specs.append(pl.BlockSpec((n_seq, 1, 2, N_HEADS, N_SSD, D_HEAD), lambda i: (i, layer, 0, 0, 0, 0)))
        args.append(s0)
    st_shape, st_spec, alias = _state_target(acc, bsz, n_seq, layer, (N_HEADS, N_SSD, D_HEAD), args, in_specs, 1)
    return pl.pallas_call(
        functools.partial(_ssd_kernel, n_chunks, n_seq, has_s0, acc is not None),
        out_shape=(jax.ShapeDtypeStruct((bsz * seq_len, W_GROUP), F32), st_shape),
        grid=(bsz // n_seq,),
        in_specs=in_specs,
        out_specs=(pl.BlockSpec((rows, W_GROUP), lambda i: (i, 0)), st_spec),
        input_output_aliases=alias,
        scratch_shapes=[pltpu.VMEM((n_seq, W_GROUP, W_GROUP), F32), pltpu.VMEM((n_seq, W_GROUP, W_GROUP), F32)]
        + [pltpu.VMEM((rows, W_GROUP), F32)] * 4,
        compiler_params=_params(("arbitrary",)),
        name="ssd",
    )(*args)


def _gdn_prep_kernel(n_pre, p_ref, cw_ref, q_ref, k_ref, v_ref):
    CP = SCAN_CHUNK
    bd = _head_ones()

    def prep(c, carry):
        r0 = pl.multiple_of(c * CP, CP)
        y = _silu(_conv3_chunk(p_ref, r0, CP, n_pre * CP, 0, 768, cw_ref, c == 0, c == n_pre - 1))
        q, k, v = y[:, 0:256], y[:, 256:512], y[:, 512:768]
        rows = pl.ds(r0, CP)
        q_ref[rows, :] = q * lax.rsqrt(_dot01(bd, q * q, m_left=False) + EPS) * (D_HEAD ** -0.5)
        k_ref[rows, :] = k * lax.rsqrt(_dot01(bd, k * k, m_left=False) + EPS)
        v_ref[rows, :] = v
        return carry

    lax.fori_loop(0, n_pre, prep, 0)


def _gdn_scan_kernel(n_seq, n_blk, has_s0, has_acc, *refs):
    C = GDN_CHUNK
    R = N_HEADS * C
    SUB = SCAN_CHUNK // C
    refs = list(refs)
    qf, kf, vf, gf, qb, kb, vb, gb, dtb_ref, alog_ref = refs[:10]
    refs = refs[10:]
    if has_s0:
        s0_ref = refs[0]
        refs = refs[1:]
    if has_acc:
        refs = refs[1:]
    of_ref, ob_ref, st_ref, sf, sb = refs
    t = pl.program_id(1)

    @pl.when(t == 0)
    def _():
        sf[...] = jnp.zeros_like(sf)
        sb[...] = jnp.zeros_like(sb)
        if has_s0:
            for j in range(n_seq):
                for h in range(N_HEADS):
                    sl = slice(D_HEAD * h, D_HEAD * (h + 1))
                    sf[j, sl, sl] = s0_ref[j, 0, 0, h]
                    sb[j, sl, sl] = s0_ref[j, 0, 1, h]

    lane_s = _iota((1, SMALL_W), 1)
    la_mask = jnp.where(lane_s < 8, 1.0, 0.0)
    same = (_iota((R, R), 0) // C) == (_iota((R, R), 1) // C)
    lane_head = _iota((1, W_GROUP), 1) // D_HEAD
    pad_rows = jnp.zeros((SMALL_W - C, SMALL_W), F32)
    ic = _iota((C, R), 0)
    jc = _iota((C, R), 1) % C
    zero_b = jnp.zeros((R, R), BF16)

    def expand(x):
        xb = x.astype(BF16)
        return jnp.where(same, jnp.concatenate([xb] * N_HEADS, axis=0), zero_b)

    def per_lane(x, lane0):
        cols = [jnp.broadcast_to(x[:, lane0 + h:lane0 + h + 1], (C, W_GROUP)) for h in range(N_HEADS)]
        return jnp.where(lane_head == 0, cols[0],
                         jnp.where(lane_head == 1, cols[1], jnp.where(lane_head == 2, cols[2], cols[3])))

    def chunks(streams):
        seqs, offs, dirs = zip(*streams)
        fwd = [d == 0 for d in dirs]
        base = [4 * d for d in dirs]
        rows = [pl.ds(pl.multiple_of(r0, C), C) for r0 in offs]
        pick = lambda a_f, a_b: [(a_f if f else a_b) for f in fwd]
        q_refs, k_refs, v_refs, g_refs = pick(qf, qb), pick(kf, kb), pick(vf, vb), pick(gf, gb)
        s_refs, o_refs = pick(sf, sb), pick(of_ref, ob_ref)
        g = [r[j, rw, :] for r, j, rw in zip(g_refs, seqs, rows)]
        la = [-jnp.exp(alog_ref[...]) * _softplus(x + dtb_ref[...]) * la_mask for x in g]
        beta = [_sigmoid(x) for x in g]
        cum = [_dot01(_tri(C, f), x) for f, x in zip(fwd, la)]
        last_row = [x[C - 1:C, :] if f else x[0:1, :] for f, x in zip(fwd, cum)]
        cum_t = [jnp.concatenate([x, pad_rows], axis=0).T for x in cum]
        cum_row = []
        for x, b in zip(cum_t, base):
            pc = [x[b + h:b + h + 1, :] for h in range(N_HEADS)]
            cum_row.append(jnp.concatenate([pc[0] + pltpu.roll(pc[1], C, 1), pc[2] + pltpu.roll(pc[3], C, 1)],
                                           axis=1))
        cum_n = [per_lane(x, b) for x, b in zip(cum, base)]
        beta_n = [per_lane(x, 8 + b) for x, b in zip(beta, base)]
        last_lane = []
        for l, b in zip(last_row, base):
            ll = jnp.zeros((1, W_GROUP), F32)
            for h in range(N_HEADS):
                ll = jnp.where(lane_head == h, l[:, b + h:b + h + 1], ll)
            last_lane.append(ll)
        ecum_n = [jnp.exp(x) for x in cum_n]
        kdec_n = [jnp.exp(l - x) for l, x in zip(last_lane, cum_n)]
        incl = [(jc <= ic) if f else (jc >= ic) for f in fwd]
        strict = [(jc < ic) if f else (jc > ic) for f in fwd]
        dec = [jnp.exp(jnp.where(m, x - y, NEG_BIG)) for m, x, y in zip(incl, cum_n, cum_row)]
        q = [r[j, rw, :] for r, j, rw in zip(q_refs, seqs, rows)]
        k = [r[j, rw, :] for r, j, rw in zip(k_refs, seqs, rows)]
        v = [r[j, rw, :] for r, j, rw in zip(v_refs, seqs, rows)]
        k_x = [expand(x) for x in k]
        kk = [_dot(x, y, NT) for x, y in zip(k, k_x)]
        qk = [_dot(x, y, NT) for x, y in zip(q, k_x)]
        a = [jnp.where(m, d * x, 0.0) * b for m, d, x, b in zip(strict, dec, kk, beta_n)]
        n = [-jnp.where((ic // 2) == (jc // 2), x, 0.0) for x in a]
        s = 2
        while s < C:
            blk = ((ic // (2 * s)) == (jc // (2 * s))) & ((ic // s) != (jc // s))
            lo = [jnp.where(blk, x, 0.0) for x in a]
            y = [x + _dot(x, expand(m)) for x, m in zip(lo, n)]
            n = [m - (x + _dot(m, expand(x))) for m, x in zip(n, y)]
            s *= 2
        kp = [x * (b * e) for x, b, e in zip(k, beta_n, ecum_n)]
        vp = [x * b for x, b in zip(v, beta_n)]
        wm = [x + _dot(m, expand(x)) for m, x in zip(n, kp)]
        um = [x + _dot(m, expand(x)) for m, x in zip(n, vp)]
        st = [r[j] for r, j in zip(s_refs, seqs)]
        v_new = [u - _dot(w_, s_) for u, w_, s_ in zip(um, wm, st)]
        o = [_dot(x * e, s_) + _dot(y * d, expand(vn))
             for x, e, s_, y, d, vn in zip(q, ecum_n, st, qk, dec, v_new)]
        for j, rw, o_r, x in zip(seqs, rows, o_refs, o):
            o_r[j, rw, :] = x
        for s_r, j, s_, l, x, kd, vn in zip(s_refs, seqs, st, last_lane, k, kdec_n, v_new):
            s_r[j] = s_ * jnp.exp(l) + jnp.where(same, _dot(x * kd, vn, TN), 0.0)

    def scan_body(u, carry):
        streams = []
        for j in range(n_seq):
            streams += [(j, u * C, 0), (j, (SUB - 1 - u) * C, 1)]
        chunks(streams)
        return carry

    lax.fori_loop(0, SUB, scan_body, 0)

    @pl.when(t == n_blk - 1)
    def _():
        for j in range(n_seq):
            for h in range(N_HEADS):
                sl = slice(D_HEAD * h, D_HEAD * (h + 1))
                st_ref[j, 0, 0, h] = sf[j, sl, sl]
                st_ref[j, 0, 1, h] = sb[j, sl, sl]


def _gdn(p_gdn, p_small, conv_w, dtb, alog, bsz, seq_len, s0, layer, acc):
    n_tok = bsz * seq_len
    n_blk = seq_len // SCAN_CHUNK
    n_seq = min(GDN_SEQS_MAX, bsz)
    has_s0 = s0 is not None
    tok = jax.ShapeDtypeStruct((n_tok, W_GROUP), F32)

    q, k, v = pl.pallas_call(
        functools.partial(_gdn_prep_kernel, n_blk),
        out_shape=(tok, tok, tok),
        grid=(bsz,),
        in_specs=[pl.BlockSpec((seq_len, 768), lambda i: (i, 0)), pl.BlockSpec((3, 768), lambda i: (0, 0))],
        out_specs=(pl.BlockSpec((seq_len, W_GROUP), lambda i: (i, 0)),) * 3,
        compiler_params=_params(("arbitrary",)),
        name="gated_delta_prep",
    )(p_gdn, conv_w)

    seq3 = lambda a: a.reshape(bsz, seq_len, a.shape[-1])
    fwd_blk = lambda w: pl.BlockSpec((n_seq, SCAN_CHUNK, w), lambda i, t: (i, t, 0))
    bwd_blk = lambda w: pl.BlockSpec((n_seq, SCAN_CHUNK, w), lambda i, t: (i, n_blk - 1 - t, 0))
    const = lambda a, b: pl.BlockSpec((a, b), lambda i, t: (0, 0))
    q3, k3, v3, g3 = seq3(q), seq3(k), seq3(v), seq3(p_small)
    in_specs = [fwd_blk(W_GROUP)] * 3 + [fwd_blk(SMALL_W)] + [bwd_blk(W_GROUP)] * 3 + [bwd_blk(SMALL_W)] \
        + [const(1, SMALL_W), const(1, SMALL_W)]
    args = [q3, k3, v3, g3, q3, k3, v3, g3, dtb, alog]
    if has_s0:
        in_specs.append(pl.BlockSpec((n_seq, 1, 2, N_HEADS, D_HEAD, D_HEAD), lambda i, t: (i, layer, 0, 0, 0, 0)))
        args.append(s0)
    seq_out = jax.ShapeDtypeStruct((bsz, seq_len, W_GROUP), F32)
    st_shape, st_spec, alias = _state_target(acc, bsz, n_seq, layer, (N_HEADS, D_HEAD, D_HEAD), args, in_specs, 2)
    o_f, o_b, state = pl.pallas_call(
        functools.partial(_gdn_scan_kernel, n_seq, n_blk, has_s0, acc is not None),
        out_shape=(seq_out, seq_out, st_shape),
        grid=(bsz // n_seq, n_blk),
        in_specs=in_specs,
        out_specs=(fwd_blk(W_GROUP), bwd_blk(W_GROUP), st_spec),
        input_output_aliases=alias,
        scratch_shapes=[pltpu.VMEM((n_seq, W_GROUP, W_GROUP), F32), pltpu.VMEM((n_seq, W_GROUP, W_GROUP), F32)],
        compiler_params=_params(("arbitrary", "arbitrary")),
        name="gated_delta_scan",
    )(*args)
    return o_f.reshape(n_tok, W_GROUP), o_b.reshape(n_tok, W_GROUP), state


def _dft_tables(seq_len):
    period = 8 * seq_len
    theta = math.pi / (4 * seq_len)
    fine, coarse = 64, seq_len // 64
    odd = 2 * jnp.arange(seq_len, dtype=jnp.int32) + 1
    ka = ((2 * fine * jnp.arange(coarse, dtype=jnp.int32))[:, None] * odd[None, :]) % period
    kb = (odd[:fine, None] * odd[None, :]) % period
    ang_a = ka.astype(F32)[:, None, :] * theta
    ang_b = kb.astype(F32)[None, :, :] * theta
    ca, sa, cb, sb = jnp.cos(ang_a), jnp.sin(ang_a), jnp.cos(ang_b), jnp.sin(ang_b)
    c2 = (ca * cb - sa * sb).reshape(seq_len, seq_len).astype(BF16)
    s2 = (sa * cb + ca * sb).reshape(seq_len, seq_len).astype(BF16)
    half = odd.astype(F32) * theta
    return c2, s2, jnp.stack([jnp.cos(half), jnp.sin(half)], axis=1)


def _filter_features(seq_len):
    t = jnp.linspace(0.0, 1.0, seq_len, dtype=F32)[:, None]
    bands = (HY_EMB - 1) // 2
    w = 2.0 * math.pi * jnp.arange(seq_len, dtype=F32)[:, None] / seq_len
    f = jnp.linspace(1e-4, bands - 1, bands, dtype=F32)[None, :]
    z = jnp.concatenate([t, jnp.cos(f * w), -jnp.sin(f * w)], axis=-1)
    z = jnp.pad(z, ((0, 0), (0, 128 - HY_EMB)))
    deltas = jnp.abs(jnp.linspace(math.log(HY_DECAY_TARGET) / HY_SLOW_PCT,
                                  math.log(HY_DECAY_TARGET) / HY_FAST_PCT, W_GROUP, dtype=F32))
    return z, jnp.exp(-t * deltas)


def _hyena_kernel(n_chunks, p_ref, cw_ref, cb_ref, bias_ref, nw_ref, z_ref, dec_ref, fr_ref, w1_ref, b1_ref,
                  w2_ref, b2_ref, w3_ref, c2_ref, s2_ref, ph_ref, o_ref,
                  hre_s, him_s, x0_s, u_s, ub_s, a_s, b_s, yre_s, yim_s):
    C = SCAN_CHUNK
    seq_len = n_chunks * C

    @pl.when(pl.program_id(0) == 0)
    def _():
        fr = fr_ref[0]

        def taps_chunk(c, carry):
            r0 = pl.multiple_of(c * C, C)
            rows = pl.ds(r0, C)
            h = jnp.sin(fr * (_dot_hp(z_ref[rows, :], w1_ref[0]) + b1_ref[0]))
            h = jnp.sin(fr * (_dot_hp(h, w2_ref[0]) + b2_ref[0]))
            taps = _dot_hp(h, w3_ref[0])
            hf = taps[:, 0:W_GROUP] * dec_ref[rows, :]
            hb = taps[:, W_GROUP:2 * W_GROUP] * dec_ref[rows, :]
            hb = jnp.where(_iota(hb.shape, 0) + r0 == 0, 0.0, hb)
            yre_s[rows, :] = (hf + hb).astype(BF16)
            yim_s[rows, :] = (hf - hb).astype(BF16)
            return carry

        lax.fori_loop(0, n_chunks, taps_chunk, 0)
        a_s[...] = _dot(c2_ref[...], yre_s[...])
        b_s[...] = _dot(s2_ref[...], yre_s[...])

        def hre_chunk(c, carry):
            rows = pl.ds(pl.multiple_of(c * C, C), C)
            hre_s[rows, :] = ph_ref[rows, 0:1] * a_s[rows, :] + ph_ref[rows, 1:2] * b_s[rows, :]
            return carry

        lax.fori_loop(0, n_chunks, hre_chunk, 0)
        a_s[...] = _dot(c2_ref[...], yim_s[...])
        b_s[...] = _dot(s2_ref[...], yim_s[...])

        def him_chunk(c, carry):
            rows = pl.ds(pl.multiple_of(c * C, C), C)
            him_s[rows, :] = ph_ref[rows, 1:2] * a_s[rows, :] - ph_ref[rows, 0:1] * b_s[rows, :]
            return carry

        lax.fori_loop(0, n_chunks, him_chunk, 0)

    def prep(c, carry):
        r0 = pl.multiple_of(c * C, C)
        y = _conv3_chunk(p_ref, r0, C, seq_len, 0, 768, cw_ref, c == 0, c == n_chunks - 1) + cb_ref[...]
        rows = pl.ds(r0, C)
        u = y[:, 512:768] * y[:, 256:512]
        x0_s[rows, :] = y[:, 0:256]
        u_s[rows, :] = u
        ub_s[rows, :] = u.astype(BF16)
        return carry

    lax.fori_loop(0, n_chunks, prep, 0)

    a_s[...] = _dot(c2_ref[...], ub_s[...])
    b_s[...] = _dot(s2_ref[...], ub_s[...])

    def spectrum(c, carry):
        rows = pl.ds(pl.multiple_of(c * C, C), C)
        ure, su, hre, him = a_s[rows, :], b_s[rows, :], hre_s[rows, :], him_s[rows, :]
        yre_s[rows, :] = (ure * hre + su * him).astype(BF16)
        yim_s[rows, :] = (ure * him - su * hre).astype(BF16)
        return carry

    lax.fori_loop(0, n_chunks, spectrum, 0)
    a_s[...] = _dot(c2_ref[...], yre_s[...])
    b_s[...] = _dot(s2_ref[...], yim_s[...])

    def fin(c, carry):
        rows = pl.ds(pl.multiple_of(c * C, C), C)
        conv = (a_s[rows, :] - b_s[rows, :]) * (1.0 / seq_len)
        y = x0_s[rows, :] * (conv + u_s[rows, :] * bias_ref[...])
        o_ref[rows, :] = _rms(y, nw_ref[...])
        return carry

    lax.fori_loop(0, n_chunks, fin, 0)


def _hyena(p_hy, conv_w, conv_b, bias, nw, tabs, feats, freq, w1, b1, w2, b2, w3, bsz, seq_len, layer):
    c2, s2, phase = tabs
    z, dec = feats
    one = dict(pipeline_mode=pl.Buffered(1)) if seq_len > 1024 else {}
    const = lambda a, b: pl.BlockSpec((a, b), lambda i: (0, 0), **one)
    lay = lambda a, b: pl.BlockSpec((1, a, b), lambda i: (layer, 0, 0))
    acc = pltpu.VMEM((seq_len, W_GROUP), F32)
    half = pltpu.VMEM((seq_len, W_GROUP), BF16)
    return pl.pallas_call(
        functools.partial(_hyena_kernel, seq_len // SCAN_CHUNK),
        out_shape=jax.ShapeDtypeStruct((bsz * seq_len, W_GROUP), F32),
        grid=(bsz,),
        in_specs=[pl.BlockSpec((seq_len, 768), lambda i: (i, 0), **one), const(3, 768), const(1, 768),
                  const(1, W_GROUP), const(1, W_GROUP), const(seq_len, 128), const(seq_len, W_GROUP),
                  lay(1, HY_HIDDEN), lay(128, HY_HIDDEN), lay(1, HY_HIDDEN), lay(HY_HIDDEN, HY_HIDDEN),
                  lay(1, HY_HIDDEN), lay(HY_HIDDEN, 2 * W_GROUP),
                  const(seq_len, seq_len), const(seq_len, seq_len), const(seq_len, 2)],
        out_specs=pl.BlockSpec((seq_len, W_GROUP), lambda i: (i, 0)),
        scratch_shapes=[acc, acc, acc, acc, half, acc, acc, half, half],
        compiler_params=_params(("arbitrary",)),
        name="hyena",
    )(p_hy, conv_w, conv_b, bias, nw, z, dec, freq, w1, b1, w2, b2, w3, c2, s2, phase)


def _rope_tables(seq_len):
    rows = seq_len // GRID_W
    r = jnp.repeat(jnp.arange(rows), GRID_W).astype(F32)
    col = jnp.tile(jnp.arange(GRID_W), rows).astype(F32)
    quarter = D_HEAD // 4
    inv = ROPE_BASE ** (-jnp.arange(quarter, dtype=F32) / quarter)
    ang = jnp.concatenate([r[:, None] * inv, col[:, None] * inv], axis=-1)
    cos, sin = jnp.cos(ang), jnp.sin(ang)
    cos_full = jnp.tile(jnp.concatenate([cos, cos], axis=-1), (1, N_HEADS))
    sin_signed = jnp.tile(jnp.concatenate([-sin, sin], axis=-1), (1, N_HEADS))
    return cos_full, sin_signed


def _small_vec(first_lane, vals):
    flat = vals.reshape(-1)
    return jnp.zeros((1, SMALL_W), F32).at[0, first_lane:first_lane + flat.shape[0]].set(flat)


def _trunk(x, bsz, seq_len, row_fn, states, finals, rope_tabs, hy_tabs, hy_feats, mod3, w):
    st_ret, st_gdn, st_ssd = states
    f_ret, f_gdn, f_ssd = finals
    for l in range(DEPTH):
        p_ret, p_hy, p_gdn, p_ssd, p_small = _in_proj(x, mod3, w['norm1'], w['w_main'], w['w_ssd'],
                                                      w['w_small'], l, row_fn)
        o_ret, s_ret = _retention(p_ret, w['ret_la'][l], w['ret_norm'][l], bsz, seq_len, rope_tabs, st_ret, l,
                                  f_ret)
        o_hy = _hyena(p_hy, w['hy_conv_w'][l], w['hy_conv_b'][l], w['hy_bias'][l], w['hy_norm'][l],
                      hy_tabs, hy_feats, w['hy_freq'], w['hy_w1'], w['hy_b1'], w['hy_w2'], w['hy_b2'],
                      w['hy_w3'], bsz, seq_len, l)
        og_f, og_b, s_gdn = _gdn(p_gdn, p_small, w['gdn_conv_w'][l], w['gdn_dtb'][l], w['gdn_alog'][l],
                                 bsz, seq_len, st_gdn, l, f_gdn)
        o_ssd, s_ssd = _ssd(p_ssd, p_small, w['ssd_conv_w'][l], w['ssd_conv_b'][l], w['ssd_dtb'][l],
                            w['ssd_alog'][l], w['ssd_dl'][l], w['ssd_norm'][l], bsz, seq_len, st_ssd, l, f_ssd)
        x = _out_proj((o_ret, o_hy, og_f, og_b, p_gdn, w['gdn_norm'][l], o_ssd), x, mod3, w['norm2'],
                      w['w_out'], w['mlp_w1'], w['mlp_w2'], w['final_norm'], l, row_fn, l == DEPTH - 1)
        if f_ret is not None:
            f_ret, f_gdn, f_ssd = s_ret, s_gdn, s_ssd
    return x, (f_ret, f_gdn, f_ssd)


def kernel(x_prompt, x_sample, state_ret, state_gdn, state_ssd, c, c_ctx, norm1_w, norm2_w, w_mod, b_mod, w_in, w_out, ret_log_decay, ret_norm_w, hy_conv_w, hy_conv_b, hy_freq, hy_w1, hy_b1, hy_w2, hy_b2, hy_w3, hy_bias, hy_norm_w, gdn_conv_w, gdn_A_log, gdn_dt_bias, gdn_norm_w, ssd_conv_w, ssd_conv_b, ssd_A_log, ssd_dt_bias, ssd_D, ssd_norm_w, mlp_w1, mlp_w2, final_norm_w):
    bsz, seq, _ = x_prompt.shape
    dbsz, dseq, _ = x_sample.shape

    cond8 = jnp.concatenate([c, c_ctx[None, :], jnp.zeros((MOD_ROWS - dbsz - 1, D_MODEL), F32)], axis=0)
    mod = _modulation(cond8, w_mod, b_mod)
    mod3 = mod.reshape(DEPTH * MOD_ROWS * 6, 1, D_MODEL)

    w_main = w_in.astype(BF16)
    w_ssd = w_main[:, :, 2832:3856]
    w_small = jnp.concatenate([w_main[:, :, 2816:2832], w_main[:, :, 3856:3864],
                               jnp.zeros((DEPTH, D_MODEL, SMALL_W - 24), BF16)], axis=-1)
    w = dict(
        norm1=norm1_w.reshape(DEPTH, 1, D_MODEL), norm2=norm2_w.reshape(DEPTH, 1, D_MODEL),
        w_main=w_main, w_ssd=w_ssd, w_small=w_small, w_out=w_out.astype(BF16),
        mlp_w1=mlp_w1.astype(BF16), mlp_w2=mlp_w2.astype(BF16), final_norm=final_norm_w,
        ret_la=[jnp.repeat(ret_log_decay[l], D_HEAD, axis=-1) for l in range(DEPTH)],
        ret_norm=[ret_norm_w[l][None, :] for l in range(DEPTH)],
        hy_freq=hy_freq.reshape(DEPTH, 1, HY_HIDDEN),
        hy_w1=jnp.pad(hy_w1, ((0, 0), (0, 128 - HY_EMB), (0, 0))),
        hy_b1=hy_b1.reshape(DEPTH, 1, HY_HIDDEN), hy_w2=hy_w2, hy_b2=hy_b2.reshape(DEPTH, 1, HY_HIDDEN),
        hy_w3=hy_w3,
        hy_conv_w=hy_conv_w, hy_conv_b=[hy_conv_b[l][None, :] for l in range(DEPTH)],
        hy_bias=[hy_bias[l][None, :] for l in range(DEPTH)], hy_norm=[hy_norm_w[l][None, :] for l in range(DEPTH)],
        gdn_conv_w=gdn_conv_w,
        gdn_dtb=[_small_vec(0, gdn_dt_bias[l]) for l in range(DEPTH)],
        gdn_alog=[_small_vec(0, gdn_A_log[l]) for l in range(DEPTH)],
        gdn_norm=[gdn_norm_w[l][None, :] for l in range(DEPTH)],
        ssd_conv_w=ssd_conv_w, ssd_conv_b=[ssd_conv_b[l][None, :] for l in range(DEPTH)],
        ssd_dtb=[_small_vec(16, ssd_dt_bias[l]) for l in range(DEPTH)],
        ssd_alog=[_small_vec(16, ssd_A_log[l]) for l in range(DEPTH)],
        ssd_dl=[jnp.repeat(ssd_D[l], D_HEAD)[None, :] for l in range(DEPTH)],
        ssd_norm=[ssd_norm_w[l][None, :] for l in range(DEPTH)],
    )

    finals = (jnp.zeros((bsz, DEPTH, 2, N_HEADS, D_HEAD, D_HEAD), F32),
              jnp.zeros((bsz, DEPTH, 2, N_HEADS, D_HEAD, D_HEAD), F32),
              jnp.zeros((bsz, DEPTH, 2, N_HEADS, N_SSD, D_HEAD), F32))
    y_p, (new_ret, new_gdn, new_ssd) = _trunk(
        x_prompt.reshape(bsz * seq, D_MODEL), bsz, seq, lambda row: CTX_MOD_ROW, (None, None, None), finals,
        None, _dft_tables(seq), _filter_features(seq), mod3, w)

    y_s, _ = _trunk(x_sample.reshape(dbsz * dseq, D_MODEL), dbsz, dseq, lambda row: row // dseq,
                    (state_ret, state_gdn, state_ssd), (None, None, None), _rope_tables(dseq),
                    _dft_tables(dseq), _filter_features(dseq), mod3, w)

    return (y_p.reshape(bsz, seq, D_MODEL), y_s.reshape(dbsz, dseq, D_MODEL), new_ret, new_gdn, new_ssd)
```
